```python
import math
import jax
import jax.numpy as jnp
from jax import lax
import numpy as np

D_MODEL = 1024
BATCH = 8
SEQ = 2048
DEPTH = 4

CTX_LEN = 256
GRID_W = 64
HALF_W = D_MODEL // 2
OUT_W = 2 * HALF_W
NA_HEADS = 8
NA_HD = HALF_W // NA_HEADS
NA_WIN_ROWS = 8
NA_WIN_COLS = 16
DN_HEADS = 4
DN_DK = HALF_W // DN_HEADS
DN_DV = DN_DK
WA_HEADS = 8
WA_KV_HEADS = 2
WA_GROUP = WA_HEADS // WA_KV_HEADS
WA_HD = HALF_W // WA_HEADS
WINDOW = 128
SSD_HEADS = 8
SSD_HD = HALF_W // SSD_HEADS
SSD_INNER = SSD_HEADS * SSD_HD
SSD_GROUPS = 2
SSD_STATE = 128
CHUNK = 64
CONV_K = 5
ROPE_BASE = 10000.0
EPS = 1e-6
NEG_INF = -1e30
FF_DENSE = 2816
N_EXPERTS = 8
TOP_K = 2
FF_EXPERT = 3584
N_EVEN = (DEPTH + 1) // 2
N_ODD = DEPTH // 2
EV_SIZES = (NA_HEADS * NA_HD, NA_HEADS * NA_HD, NA_HEADS * NA_HD, DN_HEADS * DN_DK, DN_HEADS * DN_DK, DN_HEADS * DN_DV, DN_HEADS * DN_DV, 4 * DN_HEADS)
EV_IN = sum(EV_SIZES)
OD_SIZES = (WA_HEADS * WA_HD, WA_KV_HEADS * WA_HD, WA_KV_HEADS * WA_HD, SSD_INNER, SSD_INNER, SSD_GROUPS * SSD_STATE, SSD_GROUPS * SSD_STATE, 2 * SSD_HEADS)
OD_IN = sum(OD_SIZES)

kernel_name = 'hybrid_natten_deltanet_swa_ssd_moe_dit'

F32 = jnp.float32


def rmsnorm(x, w):
    xf = x.astype(F32)
    y = xf * lax.rsqrt(jnp.mean(xf * xf, axis=-1, keepdims=True) + EPS)
    return (y * w.astype(F32)).astype(x.dtype)


def l2norm(x):
    xf = x.astype(F32)
    return xf * lax.rsqrt(jnp.sum(xf * xf, axis=-1, keepdims=True) + EPS)


def modulate(h, shift, scale):
    return h * (1 + scale) + shift


def split_cols(p, sizes):
    cuts = [int(v) for v in np.cumsum(sizes)[:-1]]
    return jnp.split(p, cuts, axis=-1)


def to_heads(t, n):
    b, s, _ = t.shape
    return t.reshape(b, s, n, -1).transpose(0, 2, 1, 3)


def from_heads(t):
    b, n, s, d = t.shape
    return t.transpose(0, 2, 1, 3).reshape(b, s, n * d)


def flip_time(z, rev, axis):
    return jnp.flip(z, axis) if rev else z


def conv_centred(x, w):
    k = w.shape[0]
    return lax.conv_general_dilated(x, w[:, None, :].astype(x.dtype), window_strides=(1,), padding=[(k // 2, k // 2)], dimension_numbers=('NWC', 'WIO', 'NWC'), feature_group_count=x.shape[-1])


def rope_1d(x, pos):
    half = x.shape[-1] // 2
    freqs = ROPE_BASE ** (-jnp.arange(half, dtype=F32) / half)
    ang = pos.astype(F32)[:, None] * freqs[None, :]
    cos, sin = jnp.cos(ang).astype(x.dtype), jnp.sin(ang).astype(x.dtype)
    x1, x2 = x[..., :half], x[..., half:]
    return jnp.concatenate([x1 * cos - x2 * sin, x1 * sin + x2 * cos], axis=-1)


def axial_rope(x):
    t = jnp.arange(x.shape[-2])
    h = x.shape[-1] // 2
    return jnp.concatenate([rope_1d(x[..., :h], t // GRID_W), rope_1d(x[..., h:], t % GRID_W)], axis=-1)


def context_attention(qc, kc, vc, sink=None):
    s = jnp.einsum('bkgqd,bkcd->bkgqc', qc, kc).astype(F32) * qc.shape[-1] ** -0.5
    if sink is not None:
        s = jnp.concatenate([s, jnp.broadcast_to(sink.astype(F32)[None, :, :, None, None], s.shape[:-1] + (1,))], axis=-1)
    p = jax.nn.softmax(s, axis=-1)[..., :kc.shape[2]].astype(vc.dtype)
    return jnp.einsum('bkgqc,bkcd->bkgqd', p, vc)


def neighbourhood_attention(q, k, v, kc, vc, rpb):
    b, h, s, d = q.shape
    rows = s // GRID_W
    kr = min(NA_WIN_ROWS, rows)
    r = jnp.arange(rows)
    row_idx = jnp.clip(r - kr // 2, 0, rows - kr)[:, None] + jnp.arange(kr)[None, :]
    col = jnp.arange(GRID_W)
    col_start = jnp.clip(col - NA_WIN_COLS // 2, 0, GRID_W - NA_WIN_COLS)
    col_ok = (col[None, :] >= col_start[:, None]) & (col[None, :] < col_start[:, None] + NA_WIN_COLS)
    dr = row_idx - r[:, None] + NA_WIN_ROWS - 1
    dc = jnp.clip(col[None, :] - col[:, None] + NA_WIN_COLS - 1, 0, 2 * NA_WIN_COLS - 2)
    bias = rpb[:, dr][..., dc].astype(F32).transpose(0, 1, 3, 2, 4)
    bias = jnp.where(col_ok[None, None, :, None, :], bias, NEG_INF).reshape(h, rows, GRID_W, kr * GRID_W)
    qg = q.reshape(b, h, rows, GRID_W, d)
    kg = k.reshape(b, h, rows, GRID_W, d)[:, :, row_idx].reshape(b, h, rows, kr * GRID_W, d)
    vg = v.reshape(b, h, rows, GRID_W, d)[:, :, row_idx].reshape(b, h, rows, kr * GRID_W, d)
    scale = d ** -0.5
    s_loc = jnp.einsum('bhrqd,bhrkd->bhrqk', qg, kg).astype(F32) * scale + bias
    s_ctx = jnp.einsum('bhrqd,bhcd->bhrqc', qg, kc).astype(F32) * scale
    p = jax.nn.softmax(jnp.concatenate([s_loc, s_ctx], axis=-1), axis=-1).astype(v.dtype)
    nk = kr * GRID_W
    o = jnp.einsum('bhrqk,bhrkd->bhrqd', p[..., :nk], vg) + jnp.einsum('bhrqc,bhcd->bhrqd', p[..., nk:], vc)
    return o.reshape(b, h, s, d)


def window_attention(q, k, v, kc, vc, sink):
    b, nkv, g, s, d = q.shape
    blk = WINDOW
    nb = s // blk

    def bands(t):
        tp = jnp.pad(t, ((0, 0), (0, 0), (blk, blk), (0, 0))).reshape(b, nkv, nb + 2, blk, d)
        return jnp.concatenate([tp[:, :, :-2], tp[:, :, 1:-1], tp[:, :, 2:]], axis=3)

    kb, vb = bands(k), bands(v)
    qb = q.reshape(b, nkv, g, nb, blk, d)
    qpos = jnp.arange(nb)[:, None] * blk + jnp.arange(blk)[None, :]
    kpos = (jnp.arange(nb)[:, None] - 1) * blk + jnp.arange(3 * blk)[None, :]
    kp = kpos[:, None, :]
    ok = (jnp.abs(kp - qpos[:, :, None]) <= WINDOW) & (kp >= 0) & (kp < s)
    scale = d ** -0.5
    s_loc = jnp.where(ok, jnp.einsum('bkgnqd,bknjd->bkgnqj', qb, kb).astype(F32) * scale, NEG_INF)
    s_ctx = jnp.einsum('bkgnqd,bkcd->bkgnqc', qb, kc).astype(F32) * scale
    s_sink = jnp.broadcast_to(sink.astype(F32)[None, :, :, None, None, None], s_ctx.shape[:-1] + (1,))
    p = jax.nn.softmax(jnp.concatenate([s_loc, s_ctx, s_sink], axis=-1), axis=-1).astype(v.dtype)
    nl, nc = 3 * blk, kc.shape[2]
    o = jnp.einsum('bkgnqj,bknjd->bkgnqd', p[..., :nl], vb) + jnp.einsum('bkgnqc,bkcd->bkgnqd', p[..., nl:nl + nc], vc)
    return o.reshape(b, nkv, g, s, d)


def gated_delta_chunked(q, k, v, beta, g, state0):
    b, h, t, dk = k.shape
    dv = v.shape[-1]
    n = t // CHUNK
    q, k, v = (z.reshape(b, h, n, CHUNK, -1) for z in (q, k, v))
    beta = beta.reshape(b, h, n, CHUNK)
    gc = jnp.cumsum(g.reshape(b, h, n, CHUNK), axis=-1)
    idx = jnp.arange(CHUNK)
    incl = idx[:, None] >= idx[None, :]
    strict = idx[:, None] > idx[None, :]
    decay = jnp.exp(jnp.where(incl, gc[..., :, None] - gc[..., None, :], -jnp.inf))
    kb = k * beta[..., None]
    a_low = jnp.where(strict, jnp.einsum('bhncd,bhnjd->bhncj', kb, k) * decay, 0.0)
    rhs = jnp.concatenate([v * beta[..., None], kb * jnp.exp(gc)[..., None]], axis=-1)
    sol = lax.linalg.triangular_solve(a_low + jnp.eye(CHUNK, dtype=k.dtype), rhs, left_side=True, lower=True, unit_diagonal=True)
    u, w = sol[..., :dv], sol[..., dv:]
    attn = jnp.einsum('bhncd,bhnjd->bhncj', q, k) * decay
    q_dec = q * jnp.exp(gc)[..., None]
    g_last = gc[..., -1]
    k_dec = k * jnp.exp(g_last[..., None] - gc)[..., None]

    def step(state, xs):
        u_c, w_c, a_c, qd_c, kd_c, gl_c = xs
        v_new = u_c - jnp.einsum('bhcd,bhde->bhce', w_c, state)
        o_c = jnp.einsum('bhcd,bhde->bhce', qd_c, state) + jnp.einsum('bhcj,bhje->bhce', a_c, v_new)
        state = state * jnp.exp(gl_c)[..., None, None] + jnp.einsum('bhcd,bhce->bhde', kd_c, v_new)
        return state, o_c

    xs = tuple(jnp.moveaxis(z, 2, 0) for z in (u, w, attn, q_dec, k_dec, g_last))
    state, o = lax.scan(step, state0, xs)
    return jnp.moveaxis(o, 0, 2).reshape(b, h, t, dv), state


def delta_inputs(q, k, v, gates, conv_w, a_log, dt_bias):
    qkv = jax.nn.silu(conv_centred(jnp.concatenate([q, k, v], axis=-1), conv_w))
    q, k, v = jnp.split(qkv, [DN_HEADS * DN_DK, 2 * DN_HEADS * DN_DK], axis=-1)
    q = l2norm(to_heads(q, DN_HEADS)) * DN_DK ** -0.5
    k = l2norm(to_heads(k, DN_HEADS))
    v = to_heads(v, DN_HEADS).astype(F32)
    b, t, _ = gates.shape
    raw = gates.astype(F32).reshape(b, t, 2, 2, DN_HEADS).transpose(2, 3, 0, 4, 1)
    beta = jax.nn.sigmoid(raw[0])
    g = -jnp.exp(a_log.astype(F32))[:, None, :, None] * jax.nn.softplus(raw[1] + dt_bias.astype(F32)[:, None, :, None])
    return q, k, v, beta, g


def bidir_gated_delta(ctx_in, lat_in):
    qc, kc, vc, bc, gcx = ctx_in
    ql, kl, vl, bl, gl = lat_in
    b, h, _, dk = kc.shape
    out_c = out_l = 0.0
    for d, rev in enumerate((False, True)):
        f = functools_flip(rev, 2)
        state0 = jnp.zeros((b, h, dk, vc.shape[-1]), F32)
        oc, st = gated_delta_chunked(f(qc), f(kc), f(vc), f(bc[d]), f(gcx[d]), state0)
        ol, _ = gated_delta_chunked(f(ql), f(kl), f(vl), f(bl[d]), f(gl[d]), st)
        out_c = out_c + f(oc)
        out_l = out_l + f(ol)
    return out_c, out_l


def functools_flip(rev, axis):
    return lambda z: flip_time(z, rev, axis)


def delta_gate_norm(o, z, w):
    o = rmsnorm(o, w)
    return (from_heads(o) * jax.nn.silu(z.astype(F32))).astype(z.dtype)


def ssd_chunked(x, dt, a, bm, cm, state0):
    b, t, h, p = x.shape
    n = t // CHUNK
    xd = (x * dt[..., None]).reshape(b, n, CHUNK, h, p)
    bm = bm.reshape(b, n, CHUNK, h, -1)
    cm = cm.reshape(b, n, CHUNK, h, -1)
    acum = jnp.cumsum((dt * a).reshape(b, n, CHUNK, h).transpose(0, 3, 1, 2), axis=-1)
    idx = jnp.arange(CHUNK)
    incl = idx[:, None] >= idx[None, :]
    lmat = jnp.exp(jnp.where(incl, acum[..., :, None] - acum[..., None, :], -jnp.inf))
    y_diag = jnp.einsum('bnlhs,bnjhs,bhnlj,bnjhp->bnlhp', cm, bm, lmat, xd)
    decay_states = jnp.exp(acum[..., -1:] - acum)
    states = jnp.einsum('bnlhs,bhnl,bnlhp->bnhps', bm, decay_states, xd)
    chunk_decay = jnp.exp(acum[..., -1])

    def step(hs, xs):
        st_c, dec_c = xs
        return hs * dec_c[..., None, None] + st_c, hs

    final, prev = lax.scan(step, state0, (jnp.moveaxis(states, 1, 0), jnp.moveaxis(chunk_decay, 2, 0)))
    prev = jnp.moveaxis(prev, 0, 1)
    y_off = jnp.einsum('bnlhs,bnhps,bhnl->bnlhp', cm, prev, jnp.exp(acum))
    return (y_diag + y_off).reshape(b, t, h, p), final


def ssd_inputs(x, bm, cm, dt_raw, conv_w, dt_bias):
    xbc = jax.nn.silu(conv_centred(jnp.concatenate([x, bm, cm], axis=-1), conv_w)).astype(F32)
    x, bm, cm = jnp.split(xbc, [SSD_INNER, SSD_INNER + SSD_GROUPS * SSD_STATE], axis=-1)
    b, t, _ = x.shape
    rep = SSD_HEADS // SSD_GROUPS
    x = x.reshape(b, t, SSD_HEADS, SSD_HD)
    bm = jnp.repeat(bm.reshape(b, t, SSD_GROUPS, SSD_STATE), rep, axis=2)
    cm = jnp.repeat(cm.reshape(b, t, SSD_GROUPS, SSD_STATE), rep, axis=2)
    dt = jax.nn.softplus(dt_raw.astype(F32).reshape(b, t, 2, SSD_HEADS) + dt_bias.astype(F32))
    return x, bm, cm, dt


def bidir_ssd(ctx_in, lat_in, a, d_skip):
    xc, bc, cc, dtc = ctx_in
    xl, bl, cl, dtl = lat_in
    b, _, h, p = xc.shape
    skip = d_skip.astype(F32)[:, None]
    out_c, out_l = xc * skip, xl * skip
    for d, rev in enumerate((False, True)):
        f = functools_flip(rev, 1)
        state0 = jnp.zeros((b, h, p, SSD_STATE), F32)
        yc, st = ssd_chunked(f(xc), f(dtc[:, :, d]), a[d], f(bc), f(cc), state0)
        yl, _ = ssd_chunked(f(xl), f(dtl[:, :, d]), a[d], f(bl), f(cl), st)
        out_c = out_c + f(yc)
        out_l = out_l + f(yl)
    return out_c, out_l


def ssd_gate_norm(y, z, w):
    b, t = z.shape[:2]
    y = y.reshape(b, t, SSD_INNER) * jax.nn.silu(z.astype(F32))
    y = rmsnorm(y.reshape(b, t, SSD_GROUPS, -1), w.reshape(SSD_GROUPS, -1))
    return y.reshape(b, t, SSD_INNER).astype(z.dtype)


def even_mixer(h, hc, w_in, w_out, rpb, conv_w, a_log, dt_bias, norm_w, with_ctx):
    nq, nk, nv, dq, dk, dv, dz, dg = split_cols(h @ w_in, EV_SIZES)
    cnq, cnk, cnv, cdq, cdk, cdv, cdz, cdg = split_cols(hc @ w_in, EV_SIZES)
    kc_a, vc_a = to_heads(cnk, NA_HEADS), to_heads(cnv, NA_HEADS)
    o_a = from_heads(neighbourhood_attention(to_heads(nq, NA_HEADS), to_heads(nk, NA_HEADS), to_heads(nv, NA_HEADS), kc_a, vc_a, rpb))
    oc_b, ol_b = bidir_gated_delta(delta_inputs(cdq, cdk, cdv, cdg, conv_w, a_log, dt_bias), delta_inputs(dq, dk, dv, dg, conv_w, a_log, dt_bias))
    y = jnp.concatenate([o_a, delta_gate_norm(ol_b, dz, norm_w)], axis=-1) @ w_out
    if not with_ctx:
        return y, None
    oc_a = context_attention(to_heads(cnq, NA_HEADS)[:, :, None], kc_a, vc_a)[:, :, 0]
    yc = jnp.concatenate([from_heads(oc_a), delta_gate_norm(oc_b, cdz, norm_w)], axis=-1) @ w_out
    return y, yc


def odd_mixer(h, hc, w_in, w_out, sink, conv_w, a_log, dt_bias, d_skip, norm_w, with_ctx):
    wq, wk, wv, z, xs, bs, cs, dtr = split_cols(h @ w_in, OD_SIZES)
    cwq, cwk, cwv, cz, cxs, cbs, ccs, cdtr = split_cols(hc @ w_in, OD_SIZES)
    b, s, _ = h.shape
    sink = sink.reshape(WA_KV_HEADS, WA_GROUP)
    q = axial_rope(to_heads(wq, WA_HEADS)).reshape(b, WA_KV_HEADS, WA_GROUP, s, WA_HD)
    k = axial_rope(to_heads(wk, WA_KV_HEADS))
    v = to_heads(wv, WA_KV_HEADS)
    kc, vc = to_heads(cwk, WA_KV_HEADS), to_heads(cwv, WA_KV_HEADS)
    o_c = from_heads(window_attention(q, k, v, kc, vc, sink).reshape(b, WA_HEADS, s, WA_HD))
    a = -jnp.exp(a_log.astype(F32))
    yc_d, yl_d = bidir_ssd(ssd_inputs(cxs, cbs, ccs, cdtr, conv_w, dt_bias), ssd_inputs(xs, bs, cs, dtr, conv_w, dt_bias), a, d_skip)
    y = jnp.concatenate([o_c, ssd_gate_norm(yl_d, z, norm_w)], axis=-1) @ w_out
    if not with_ctx:
        return y, None
    l = hc.shape[1]
    qc = to_heads(cwq, WA_HEADS).reshape(b, WA_KV_HEADS, WA_GROUP, l, WA_HD)
    oc_c = from_heads(context_attention(qc, kc, vc, sink).reshape(b, WA_HEADS, l, WA_HD))
    yc = jnp.concatenate([oc_c, ssd_gate_norm(yc_d, cz, norm_w)], axis=-1) @ w_out
    return y, yc


def swiglu(h, w1, w3, w2):
    return (jax.nn.silu(h @ w1) * (h @ w3)) @ w2


def moe_swiglu(h, router, w1, w3, w2):
    logits = (h @ router).astype(F32)
    top_v, top_i = lax.top_k(logits, TOP_K)
    gates = jax.nn.softmax(top_v, axis=-1)
    weight = jnp.sum(jax.nn.one_hot(top_i, N_EXPERTS, dtype=F32) * gates[..., None], axis=-2).astype(h.dtype)
    out = jnp.zeros_like(h)
    for e in range(N_EXPERTS):
        out = out + weight[..., e:e + 1] * swiglu(h, w1[e], w3[e], w2[e])
    return out


def setup_inputs(seed: int = 0) -> dict:
    key = jax.random.key(seed)
    ks = iter(list(jax.random.split(key, 40)))
    d = D_MODEL
    ne, no = N_EVEN, N_ODD

    def nrm(shape, scale):
        return jax.random.normal(next(ks), shape, F32) * scale

    def gain(shape):
        return 1.0 + nrm(shape, 0.05)

    def a_log(shape):
        return jnp.log(jax.random.uniform(next(ks), shape, F32, 1.0, 16.0))

    def dt_bias(shape):
        dt = jnp.exp(jax.random.uniform(next(ks), shape, F32, math.log(1e-3), math.log(1e-1)))
        return dt + jnp.log(-jnp.expm1(-dt))

    return {
        'x': nrm((BATCH, SEQ, d), 1.0),
        'c': nrm((BATCH, d), 1.0),
        'ctx': nrm((BATCH, CTX_LEN, d), 1.0),
        'c_ctx': nrm((d,), 1.0),
        'ada_w': nrm((DEPTH, d, 6 * d), 0.5 * d ** -0.5),
        'ada_b': nrm((DEPTH, 6 * d), 0.01),
        'norm_w': gain((DEPTH, 4, d)),
        'ev_w_in': nrm((ne, d, EV_IN), d ** -0.5),
        'ev_w_out': nrm((ne, OUT_W, d), OUT_W ** -0.5),
        'na_rpb': nrm((ne, NA_HEADS, 2 * NA_WIN_ROWS - 1, 2 * NA_WIN_COLS - 1), 0.02),
        'dn_conv': nrm((ne, CONV_K, 2 * DN_HEADS * DN_DK + DN_HEADS * DN_DV), CONV_K ** -0.5),
        'dn_a_log': a_log((ne, 2, DN_HEADS)),
        'dn_dt_bias': dt_bias((ne, 2, DN_HEADS)),
        'dn_norm': gain((ne, DN_DV)),
        'ffn_w1': nrm((ne, d, FF_DENSE), d ** -0.5),
        'ffn_w3': nrm((ne, d, FF_DENSE), d ** -0.5),
        'ffn_w2': nrm((ne, FF_DENSE, d), FF_DENSE ** -0.5),
        'od_w_in': nrm((no, d, OD_IN), d ** -0.5),
        'od_w_out': nrm((no, OUT_W, d), OUT_W ** -0.5),
        'wa_sink': nrm((no, WA_HEADS), 1.0),
        'ssd_conv': nrm((no, CONV_K, SSD_INNER + 2 * SSD_GROUPS * SSD_STATE), CONV_K ** -0.5),
        'ssd_a_log': a_log((no, 2, SSD_HEADS)),
        'ssd_dt_bias': dt_bias((no, 2, SSD_HEADS)),
        'ssd_d': gain((no, SSD_HEADS)),
        'ssd_norm': gain((no, SSD_INNER)),
        'moe_router': nrm((no, d, N_EXPERTS), d ** -0.5),
        'moe_w1': nrm((no, N_EXPERTS, d, FF_EXPERT), d ** -0.5),
        'moe_w3': nrm((no, N_EXPERTS, d, FF_EXPERT), d ** -0.5),
        'moe_w2': nrm((no, N_EXPERTS, FF_EXPERT, d), FF_EXPERT ** -0.5),
    }


def reference(x, c, ctx, c_ctx, ada_w, ada_b, norm_w, ev_w_in, ev_w_out, na_rpb, dn_conv, dn_a_log, dn_dt_bias, dn_norm, ffn_w1, ffn_w3, ffn_w2, od_w_in, od_w_out, wa_sink, ssd_conv, ssd_a_log, ssd_dt_bias, ssd_d, ssd_norm, moe_router, moe_w1, moe_w3, moe_w2):
    sc, scc = jax.nn.silu(c), jax.nn.silu(c_ctx)
    xc = ctx
    for layer in range(DEPTH):
        i = layer // 2
        with_ctx = layer < DEPTH - 1
        m = jnp.split((sc @ ada_w[layer] + ada_b[layer])[:, None, :], 6, axis=-1)
        mc = jnp.split(scc @ ada_w[layer] + ada_b[layer], 6, axis=-1)
        nw = norm_w[layer]
        h = modulate(rmsnorm(x, nw[0]), m[0], m[1])
        hc = modulate(rmsnorm(xc, nw[0]), mc[0], mc[1])
        if layer % 2 == 0:
            y, yc = even_mixer(h, hc, ev_w_in[i], ev_w_out[i], na_rpb[i], dn_conv[i], dn_a_log[i], dn_dt_bias[i], dn_norm[i], with_ctx)
            ffn = functools_ffn_dense(ffn_w1[i], ffn_w3[i], ffn_w2[i])
        else:
            y, yc = odd_mixer(h, hc, od_w_in[i], od_w_out[i], wa_sink[i], ssd_conv[i], ssd_a_log[i], ssd_dt_bias[i], ssd_d[i], ssd_norm[i], with_ctx)
            ffn = functools_ffn_moe(moe_router[i], moe_w1[i], moe_w3[i], moe_w2[i])
        x = x + m[2] * rmsnorm(y, nw[1])
        x = x + m[5] * rmsnorm(ffn(modulate(rmsnorm(x, nw[2]), m[3], m[4])), nw[3])
        if with_ctx:
            xc = xc + mc[2] * rmsnorm(yc, nw[1])
            xc = xc + mc[5] * rmsnorm(ffn(modulate(rmsnorm(xc, nw[2]), mc[3], mc[4])), nw[3])
    return x


def functools_ffn_dense(w1, w3, w2):
    return lambda t: swiglu(t, w1, w3, w2)


def functools_ffn_moe(router, w1, w3, w2):
    return lambda t: moe_swiglu(t, router, w1, w3, w2)
```

```python
import functools
import math

import jax
import jax.numpy as jnp
import numpy as np
from jax import lax
from jax.experimental import pallas as pl
from jax.experimental.pallas import tpu as pltpu

F32 = jnp.float32
BF16 = jnp.bfloat16
HIGHEST = lax.Precision.HIGHEST

EPS = 1e-6
NEG_INF = -1e30
GRID_W = 64
CHUNK = 64
CONV_K = 5
ROPE_BASE = 10000.0
NA_HEADS, NA_HD, NA_WIN_ROWS, NA_WIN_COLS = 8, 64, 8, 16
DN_HEADS, DN_DK = 4, 128
WA_HEADS, WA_KV_HEADS, WA_HD, WINDOW = 8, 2, 64, 128
WA_GROUP = WA_HEADS // WA_KV_HEADS
SSD_HEADS, SSD_HD, SSD_GROUPS, SSD_STATE = 8, 64, 2, 128
N_EXPERTS, TOP_K = 8, 2

LANES = 128
VMEM_LIMIT = 56 * 1024 * 1024
TOK_TILE = 512


def _params(*sem):
    return pltpu.CompilerParams(dimension_semantics=sem, vmem_limit_bytes=VMEM_LIMIT)


def _resident(shape):
    nd = len(shape)
    return pl.BlockSpec(shape, lambda *_: (0,) * nd, pipeline_mode=pl.Buffered(1))


def _silu(x):
    return x * jax.nn.sigmoid(x)


def _rms(x, w):
    return x * lax.rsqrt(jnp.mean(x * x, axis=-1, keepdims=True) + EPS) * w


def _dot(a, b):
    return jnp.dot(a, b, preferred_element_type=F32)


def _dot_nt(a, b):
    return lax.dot_general(a, b, (((1,), (1,)), ((), ())), preferred_element_type=F32)


def _dot_tn(a, b):
    return lax.dot_general(a, b, (((0,), (0,)), ((), ())), preferred_element_type=F32)


def _ada_body(s_ref, w_ref, b_ref, o_ref):
    s = _silu(s_ref[...])
    o_ref[0] = jnp.dot(s, w_ref[0], precision=HIGHEST, preferred_element_type=F32) + b_ref[0]


def ada_modulation(cond, ada_w, ada_b):
    depth, d, n = ada_w.shape
    r = cond.shape[0]
    tn = 1536
    return pl.pallas_call(
        _ada_body,
        out_shape=jax.ShapeDtypeStruct((depth, r, n), F32),
        grid=(depth, n // tn),
        in_specs=[
            pl.BlockSpec((r, d), lambda l, j: (0, 0)),
            pl.BlockSpec((1, d, tn), lambda l, j: (l, 0, j)),
            pl.BlockSpec((1, 1, tn), lambda l, j: (l, 0, j)),
        ],
        out_specs=pl.BlockSpec((1, r, tn), lambda l, j: (l, 0, j)),
        compiler_params=_params("arbitrary", "arbitrary"),
        name="ada_modulation",
    )(cond, ada_w, ada_b.reshape(depth, 1, n))


def _mod_rows(n_lat_tiles, tiles_per_batch, n_batch):
    def index(i):
        return (jnp.minimum(i // tiles_per_batch, n_batch), 0, 0)
    del n_lat_tiles
    return index


def _norm_mod(x, mod_ref, nw_ref, row):
    y = _rms(x, nw_ref[row:row + 1, :])
    return y * (1.0 + mod_ref[0, row + 1:row + 2, :]) + mod_ref[0, row:row + 1, :]


def _proj_even_body(x_ref, mod_ref, nw_ref, w_ref, att_ref, dqkv_ref, dz_ref, dg_ref):
    hb = _norm_mod(x_ref[...], mod_ref, nw_ref, 0).astype(BF16)
    att_ref[...] = _dot(hb, w_ref[:, 0:1536]).astype(BF16)
    dqkv_ref[...] = _dot(hb, w_ref[:, 1536:3072])
    dz_ref[...] = _dot(hb, w_ref[:, 3072:3584])
    dg_ref[...] = _dot(hb, w_ref[:, 3584:3712])


def _swap16(x):
    lane = lax.broadcasted_iota(jnp.int32, x.shape, 1)
    up = pltpu.roll(x, LANES - 16, 1)
    down = pltpu.roll(x, 16, 1)
    return jnp.where((lane % 32) < 16, up, down)


def _proj_odd_body(x_ref, mod_ref, nw_ref, w_ref, cos_ref, sin_ref, qkv_ref, z_ref, xbc_ref, dt_ref):
    hb = _norm_mod(x_ref[...], mod_ref, nw_ref, 0).astype(BF16)
    cos, sin = cos_ref[...], sin_ref[...]
    for c in range(5):
        t = _dot(hb, w_ref[:, c * LANES:(c + 1) * LANES])
        qkv_ref[:, c * LANES:(c + 1) * LANES] = (t * cos + _swap16(t) * sin).astype(BF16)
    qkv_ref[:, 640:768] = _dot(hb, w_ref[:, 640:768]).astype(BF16)
    z_ref[...] = _dot(hb, w_ref[:, 768:1280])
    xbc_ref[...] = _dot(hb, w_ref[:, 1280:2304])
    dt_ref[...] = _dot(hb, w_ref[:, 2304:2432])


def _pad_cols(w, n):
    return jnp.pad(w, ((0, 0), (0, n - w.shape[1])))


def prep_w_in_even(w):
    q_scale = jnp.concatenate([jnp.full((NA_HEADS * NA_HD,), NA_HD ** -0.5, F32), jnp.ones((w.shape[1] - NA_HEADS * NA_HD,), F32)])
    return _pad_cols(w * q_scale[None, :], 3712).astype(BF16)


def prep_w_in_odd(w):
    q_scale = jnp.concatenate([jnp.full((WA_HEADS * WA_HD,), WA_HD ** -0.5, F32), jnp.ones((w.shape[1] - WA_HEADS * WA_HD,), F32)])
    return _pad_cols(w * q_scale[None, :], 2432).astype(BF16)


def _tok_spec(width, tm=TOK_TILE):
    return pl.BlockSpec((tm, width), lambda i: (i, 0))


def proj_in_even(x_all, mod, nw, w, n_batch, seq):
    t_all, d = x_all.shape
    tm = TOK_TILE
    tpb = seq // tm
    return pl.pallas_call(
        _proj_even_body,
        out_shape=(jax.ShapeDtypeStruct((t_all, 1536), BF16), jax.ShapeDtypeStruct((t_all, 1536), F32),
                   jax.ShapeDtypeStruct((t_all, 512), F32), jax.ShapeDtypeStruct((t_all, LANES), F32)),
        grid=(t_all // tm,),
        in_specs=[_tok_spec(d), pl.BlockSpec((1, 6, d), _mod_rows(None, tpb, n_batch)),
                  _resident(nw.shape), _resident(w.shape)],
        out_specs=(_tok_spec(1536), _tok_spec(1536), _tok_spec(512), _tok_spec(LANES)),
        compiler_params=_params("parallel"),
        name="proj_in_even",
    )(x_all, mod, nw, w)


def proj_in_odd(x_all, mod, nw, w, cos_t, sin_t, n_batch, seq):
    t_all, d = x_all.shape
    tm = TOK_TILE
    tpb = seq // tm

    def rope_rows(i):
        return (jnp.where(i < n_batch * tpb, i % tpb, tpb), 0)

    return pl.pallas_call(
        _proj_odd_body,
        out_shape=(jax.ShapeDtypeStruct((t_all, 768), BF16), jax.ShapeDtypeStruct((t_all, 512), F32),
                   jax.ShapeDtypeStruct((t_all, 1024), F32), jax.ShapeDtypeStruct((t_all, LANES), F32)),
        grid=(t_all // tm,),
        in_specs=[_tok_spec(d), pl.BlockSpec((1, 6, d), _mod_rows(None, tpb, n_batch)),
                  _resident(nw.shape), _resident(w.shape),
                  pl.BlockSpec((tm, LANES), rope_rows), pl.BlockSpec((tm, LANES), rope_rows)],
        out_specs=(_tok_spec(768), _tok_spec(512), _tok_spec(1024), _tok_spec(LANES)),
        compiler_params=_params("parallel"),
        name="proj_in_odd",
    )(x_all, mod, nw, w, cos_t, sin_t)


def rope_tables(seq, tm=TOK_TILE):
    half = WA_HD // 4
    freqs = ROPE_BASE ** (-jnp.arange(half, dtype=F32) / half)
    t = jnp.arange(seq)
    ang_r = (t // GRID_W).astype(F32)[:, None] * freqs[None, :]
    ang_c = (t % GRID_W).astype(F32)[:, None] * freqs[None, :]
    cos = jnp.concatenate([jnp.cos(ang_r)] * 2 + [jnp.cos(ang_c)] * 2, axis=-1)
    sin = jnp.concatenate([-jnp.sin(ang_r), jnp.sin(ang_r), -jnp.sin(ang_c), jnp.sin(ang_c)], axis=-1)
    cos = jnp.concatenate([jnp.tile(cos, (1, 2)), jnp.ones((tm, LANES), F32)], axis=0)
    sin = jnp.concatenate([jnp.tile(sin, (1, 2)), jnp.zeros((tm, LANES), F32)], axis=0)
    return cos, sin


def _proj_out_body(group, gate_first, a_ref, r_ref, z_ref, gw_ref, w_ref, x_ref, mod_ref, nw_ref, o_ref):
    r = r_ref[...]
    gate = _silu(z_ref[...])
    if gate_first:
        r = r * gate
    parts = []
    for c in range(r.shape[1] // group):
        rc = r[:, c * group:(c + 1) * group]
        parts.append(_rms(rc, gw_ref[:, c * group:(c + 1) * group]))
    r = jnp.concatenate(parts, axis=1)
    if not gate_first:
        r = r * gate
    half = a_ref.shape[1]
    y = _dot(a_ref[...], w_ref[0:half, :]) + _dot(r.astype(BF16), w_ref[half:, :])
    o_ref[...] = x_ref[...] + mod_ref[0, 2:3, :] * _rms(y, nw_ref[1:2, :])


def proj_out(att, rec, z, gate_w, w_out, x_all, mod, nw, n_batch, seq, n_tok, group, gate_first):
    d = x_all.shape[1]
    tm = TOK_TILE
    tpb = seq // tm
    half = att.shape[1]
    return pl.pallas_call(
        functools.partial(_proj_out_body, group, gate_first),
        out_shape=jax.ShapeDtypeStruct((n_tok, d), F32),
        grid=(n_tok // tm,),
        in_specs=[_tok_spec(half), _tok_spec(half), _tok_spec(half), _resident(gate_w.shape),
                  _resident(w_out.shape), _tok_spec(d), pl.BlockSpec((1, 6, d), _mod_rows(None, tpb, n_batch)),
                  _resident(nw.shape)],
        out_specs=_tok_spec(d),
        compiler_params=_params("parallel"),
        name="proj_out",
    )(att, rec, z, gate_w, w_out, x_all, mod, nw)


def _ffn_dense_body(n_split, x_ref, mod_ref, nw_ref, w1_ref, w3_ref, w2_ref, o_ref):
    x = x_ref[...]
    y = _rms(x, nw_ref[2:3, :])
    hb = (y * (1.0 + mod_ref[0, 4:5, :]) + mod_ref[0, 3:4, :]).astype(BF16)
    ff = w1_ref.shape[1]
    step = ff // n_split
    acc = None
    for c in range(n_split):
        sl = slice(c * step, (c + 1) * step)
        act = (_silu(_dot(hb, w1_ref[:, sl])) * _dot(hb, w3_ref[:, sl])).astype(BF16)
        part = _dot(act, w2_ref[sl, :])
        acc = part if acc is None else acc + part
    o_ref[...] = x + mod_ref[0, 5:6, :] * _rms(acc, nw_ref[3:4, :])


def ffn_dense(x_all, mod, nw, w1, w3, w2, n_batch, seq, n_tok):
    d = x_all.shape[1]
    tm = TOK_TILE
    tpb = seq // tm
    return pl.pallas_call(
        functools.partial(_ffn_dense_body, 2),
        out_shape=jax.ShapeDtypeStruct((n_tok, d), F32),
        grid=(n_tok // tm,),
        in_specs=[_tok_spec(d), pl.BlockSpec((1, 6, d), _mod_rows(None, tpb, n_batch)), _resident(nw.shape),
                  _resident(w1.shape), _resident(w3.shape), _resident(w2.shape)],
        out_specs=_tok_spec(d),
        compiler_params=_params("parallel"),
        name="ffn_dense",
    )(x_all, mod, nw, w1, w3, w2)


MOE_TILE = 512
MOE_FF_TILE = 512


def _moe_route_body(x_ref, mod_ref, nw_ref, r_ref, h_ref, idx_ref, gate_ref):
    y = _rms(x_ref[...], nw_ref[2:3, :])
    h = y * (1.0 + mod_ref[0, 4:5, :]) + mod_ref[0, 3:4, :]
    h_ref[...] = h.astype(BF16)
    logits = jnp.dot(h, r_ref[...], precision=HIGHEST, preferred_element_type=F32)
    lane = lax.broadcasted_iota(jnp.int32, logits.shape, 1)
    logits = jnp.where(lane < N_EXPERTS, logits, -jnp.inf)
    v1 = jnp.max(logits, axis=-1, keepdims=True)
    i1 = jnp.min(jnp.where(logits == v1, lane, LANES), axis=-1, keepdims=True)
    rest = jnp.where(lane == i1, -jnp.inf, logits)
    v2 = jnp.max(rest, axis=-1, keepdims=True)
    i2 = jnp.min(jnp.where(rest == v2, lane, LANES), axis=-1, keepdims=True)
    e2 = jnp.exp(v2 - v1)
    g1 = 1.0 / (1.0 + e2)
    idx_ref[...] = jnp.where(lane == 0, i1, jnp.where(lane == 1, i2, 0))
    gate_ref[...] = jnp.where(lane == 0, g1, jnp.where(lane == 1, e2 * g1, 0.0))


def moe_route(x_all, mod, nw, router, n_batch, seq, n_tok):
    d = x_all.shape[1]
    tm = TOK_TILE
    tpb = seq // tm
    return pl.pallas_call(
        _moe_route_body,
        out_shape=(jax.ShapeDtypeStruct((n_tok, d), BF16), jax.ShapeDtypeStruct((n_tok, LANES), jnp.int32),
                   jax.ShapeDtypeStruct((n_tok, LANES), F32)),
        grid=(n_tok // tm,),
        in_specs=[_tok_spec(d), pl.BlockSpec((1, 6, d), _mod_rows(None, tpb, n_batch)), _resident(nw.shape),
                  _resident(router.shape)],
        out_specs=(_tok_spec(d), _tok_spec(LANES), _tok_spec(LANES)),
        compiler_params=_params("parallel"),
        name="moe_route",
    )(x_all, mod, nw, router)


def _moe_gmm_body(te_ref, nv_ref, x_ref, w1_ref, w3_ref, w2_ref, o_ref, acc_ref):
    i, f = pl.program_id(0), pl.program_id(1)

    @pl.when(i < nv_ref[0])
    def _():
        x = x_ref[...]
        act = (_silu(_dot(x, w1_ref[0])) * _dot(x, w3_ref[0])).astype(BF16)
        part = _dot(act, w2_ref[0])

        @pl.when(f == 0)
        def _():
            acc_ref[...] = part

        @pl.when(f > 0)
        def _():
            acc_ref[...] += part

        @pl.when(f == pl.num_programs(1) - 1)
        def _():
            o_ref[...] = acc_ref[...]

    @pl.when((i >= nv_ref[0]) & (f == pl.num_programs(1) - 1))
    def _():
        o_ref[...] = jnp.zeros(o_ref.shape, F32)


def moe_gmm(tile_expert, n_valid, xs, w1, w3, w2):
    p, d = xs.shape
    ff = w1.shape[2]
    tm, tf = MOE_TILE, MOE_FF_TILE
    nf = ff // tf

    def wcol(i, f, te, nv):
        live = i < nv[0]
        return (te[i], 0, jnp.where(live, f, nf - 1))

    def wrow(i, f, te, nv):
        live = i < nv[0]
        return (te[i], jnp.where(live, f, nf - 1), 0)

    return pl.pallas_call(
        _moe_gmm_body,
        out_shape=jax.ShapeDtypeStruct((p, d), F32),
        grid_spec=pltpu.PrefetchScalarGridSpec(
            num_scalar_prefetch=2,
            grid=(p // tm, nf),
            in_specs=[pl.BlockSpec((tm, d), lambda i, f, te, nv: (i, 0)),
                      pl.BlockSpec((1, d, tf), wcol), pl.BlockSpec((1, d, tf), wcol), pl.BlockSpec((1, tf, d), wrow)],
            out_specs=pl.BlockSpec((tm, d), lambda i, f, te, nv: (i, 0)),
            scratch_shapes=[pltpu.VMEM((tm, d), F32)]),
        compiler_params=_params("arbitrary", "arbitrary"),
        name="moe_gmm",
    )(tile_expert, n_valid, xs, w1, w3, w2)


def moe_dispatch(idx, n_tok):
    tm = MOE_TILE
    n_assign = n_tok * TOP_K
    p = n_assign + N_EXPERTS * tm
    e_flat = idx[:, :TOP_K].reshape(n_assign)
    onehot = (e_flat[:, None] == jnp.arange(N_EXPERTS)[None, :]).astype(jnp.int32)
    rank = jnp.take_along_axis(jnp.cumsum(onehot, axis=0) - onehot, e_flat[:, None], axis=1)[:, 0]
    counts = jnp.sum(onehot, axis=0)
    padded = ((counts + tm - 1) // tm) * tm
    ends = jnp.cumsum(padded)
    starts = ends - padded
    pos = starts[e_flat] + rank
    src_tok = jnp.zeros((p,), jnp.int32).at[pos].set(jnp.arange(n_assign, dtype=jnp.int32) // TOP_K)
    tile_start = jnp.arange(p // tm, dtype=jnp.int32) * tm
    tile_expert = jnp.minimum(jnp.sum((tile_start[:, None] >= ends[None, :]).astype(jnp.int32), axis=1), N_EXPERTS - 1)
    n_valid = (ends[-1] // tm).astype(jnp.int32).reshape(1)
    last_live = jnp.take(tile_expert, jnp.maximum(n_valid - 1, 0))
    tile_expert = jnp.where(tile_start < ends[-1], tile_expert, last_live).astype(jnp.int32)
    return src_tok, pos.reshape(n_tok, TOP_K), tile_expert, n_valid


def _moe_combine_body(y1_ref, y2_ref, gate_ref, x_ref, mod_ref, nw_ref, o_ref):
    g = gate_ref[...]
    ffn = y1_ref[...] * g[:, 0:1] + y2_ref[...] * g[:, 1:2]
    o_ref[...] = x_ref[...] + mod_ref[0, 5:6, :] * _rms(ffn, nw_ref[3:4, :])


def moe_combine(y1, y2, gates, x_all, mod, nw, n_batch, seq, n_tok):
    d = x_all.shape[1]
    tm = TOK_TILE
    tpb = seq // tm
    return pl.pallas_call(
        _moe_combine_body,
        out_shape=jax.ShapeDtypeStruct((n_tok, d), F32),
        grid=(n_tok // tm,),
        in_specs=[_tok_spec(d), _tok_spec(d), _tok_spec(LANES), _tok_spec(d),
                  pl.BlockSpec((1, 6, d), _mod_rows(None, tpb, n_batch)), _resident(nw.shape)],
        out_specs=_tok_spec(d),
        compiler_params=_params("parallel"),
        name="moe_combine",
    )(y1, y2, gates, x_all, mod, nw)


def moe_layer(x_all, mod, nw, router, w1, w3, w2, n_batch, seq, n_tok):
    h, idx, gates = moe_route(x_all, mod, nw, router, n_batch, seq, n_tok)
    src_tok, pos, tile_expert, n_valid = moe_dispatch(idx, n_tok)
    ys = moe_gmm(tile_expert, n_valid, jnp.take(h, src_tok, axis=0), w1, w3, w2)
    y1 = jnp.take(ys, pos[:, 0], axis=0)
    y2 = jnp.take(ys, pos[:, 1], axis=0)
    return moe_combine(y1, y2, gates, x_all, mod, nw, n_batch, seq, n_tok)


def na_bias_tables(rpb):
    kr = NA_WIN_ROWS
    p = jnp.arange(kr)
    dr = jnp.arange(kr)[None, :] + (NA_WIN_ROWS - 1) - p[:, None]
    col = jnp.arange(GRID_W)
    col_start = jnp.clip(col - NA_WIN_COLS // 2, 0, GRID_W - NA_WIN_COLS)
    col_ok = (col[None, :] >= col_start[:, None]) & (col[None, :] < col_start[:, None] + NA_WIN_COLS)
    dc = jnp.clip(col[None, :] - col[:, None] + NA_WIN_COLS - 1, 0, 2 * NA_WIN_COLS - 2)
    bias = rpb[:, dr][..., dc].astype(F32).transpose(0, 1, 3, 2, 4)
    bias = jnp.where(col_ok[None, None, :, None, :], bias, NEG_INF)
    return bias.reshape(rpb.shape[0], kr, GRID_W, kr * GRID_W)


def _softmax_pv(scores, values, sink=None):
    m = functools.reduce(jnp.maximum, [jnp.max(s, axis=-1, keepdims=True) for s in scores])
    if sink is not None:
        m = jnp.maximum(m, sink)
    ps = [jnp.exp(s - m) for s in scores]
    denom = functools.reduce(lambda a, b: a + b, [jnp.sum(p, axis=-1, keepdims=True) for p in ps])
    if sink is not None:
        denom = denom + jnp.exp(sink - m)
    o = functools.reduce(lambda a, b: a + b, [_dot(p.astype(BF16), v) for p, v in zip(ps, values)])
    return o / denom


def _na_body(rows, q_ref, k_ref, v_ref, qc_ref, kc_ref, vc_ref, bias_ref, o_ref, oc_ref):
    hd = NA_HD
    win = NA_WIN_ROWS * GRID_W
    kc = kc_ref[...]
    vc = vc_ref[...]

    def row_body(r, carry):
        start_row = jnp.clip(r - NA_WIN_ROWS // 2, 0, rows - NA_WIN_ROWS)
        pattern = r - start_row
        qs = pl.multiple_of(r * GRID_W, GRID_W)
        ks = pl.multiple_of(start_row * GRID_W, GRID_W)
        q2 = q_ref[pl.ds(qs, GRID_W), :]
        k2 = k_ref[pl.ds(ks, win), :]
        v2 = v_ref[pl.ds(ks, win), :]
        outs = []
        for h in range(2):
            sl = slice(h * hd, (h + 1) * hd)
            q = q2[:, sl]
            s_loc = _dot_nt(q, k2[:, sl]) + bias_ref[h, pattern]
            s_ctx = _dot_nt(q, kc[:, sl])
            outs.append(_softmax_pv([s_loc, s_ctx], [v2[:, sl], vc[:, sl]]))
        o_ref[pl.ds(qs, GRID_W), :] = jnp.concatenate(outs, axis=1).astype(o_ref.dtype)
        return carry

    lax.fori_loop(0, rows, row_body, 0)
    outs = []
    for h in range(2):
        sl = slice(h * hd, (h + 1) * hd)
        outs.append(_softmax_pv([_dot_nt(qc_ref[:, sl], kc[:, sl])], [vc[:, sl]]))
    oc_ref[...] = jnp.concatenate(outs, axis=1).astype(oc_ref.dtype)


def na_attention(att, bias, n_batch, seq, ctx_len):
    n_pair = NA_HEADS // 2
    ctx_blk0 = n_batch * seq // ctx_len
    lat = lambda off: pl.BlockSpec((seq, LANES), lambda b, j: (b, off + j))
    ctx = lambda off: pl.BlockSpec((ctx_len, LANES), lambda b, j: (ctx_blk0 + b, off + j))
    return pl.pallas_call(
        functools.partial(_na_body, seq // GRID_W),
        out_shape=(jax.ShapeDtypeStruct((n_batch * seq, NA_HEADS * NA_HD), BF16),
                   jax.ShapeDtypeStruct((n_batch * ctx_len, NA_HEADS * NA_HD), BF16)),
        grid=(n_batch, n_pair),
        in_specs=[lat(0), lat(n_pair), lat(2 * n_pair), ctx(0), ctx(n_pair), ctx(2 * n_pair),
                  pl.BlockSpec((2,) + bias.shape[1:], lambda b, j: (j, 0, 0, 0))],
        out_specs=(pl.BlockSpec((seq, LANES), lambda b, j: (b, j)), pl.BlockSpec((ctx_len, LANES), lambda b, j: (b, j))),
        compiler_params=_params("parallel", "arbitrary"),
        name="na_attention",
    )(att, att, att, att, att, att, bias)


def _wa_body(n_blk, with_ctx, sink_ref, q_ref, k_ref, v_ref, qc_ref, kc_ref, vc_ref, o_ref, oc_ref):
    hd, blk, grp = WA_HD, WINDOW, WA_GROUP
    rows = lax.broadcasted_iota(jnp.int32, (grp * blk, blk), 0) % blk
    cols = lax.broadcasted_iota(jnp.int32, (grp * blk, blk), 1)
    for kv in range(WA_KV_HEADS):
        ksl = slice(kv * hd, (kv + 1) * hd)
        kc = kc_ref[:, ksl]
        vc = vc_ref[:, ksl]

        def stack(ref, start, size):
            return jnp.concatenate([ref[pl.ds(start, size), (kv * grp + g) * hd:(kv * grp + g + 1) * hd]
                                    for g in range(grp)], axis=0)

        def unstack(o, size):
            return jnp.concatenate([o[g * size:(g + 1) * size] for g in range(grp)], axis=1)

        def sink_col(size):
            return jnp.concatenate([jnp.full((size, 1), sink_ref[kv * grp + g], F32) for g in range(grp)], axis=0)

        sink_q = sink_col(blk)

        def blk_body(n, carry):
            qs = pl.multiple_of(n * blk, blk)
            ps = pl.multiple_of(jnp.maximum(n - 1, 0) * blk, blk)
            ns = pl.multiple_of(jnp.minimum(n + 1, n_blk - 1) * blk, blk)
            q = stack(q_ref, qs, blk)
            s_prev = jnp.where((cols >= rows) & (n > 0), _dot_nt(q, k_ref[pl.ds(ps, blk), ksl]), NEG_INF)
            s_self = _dot_nt(q, k_ref[pl.ds(qs, blk), ksl])
            s_next = jnp.where((cols <= rows) & (n < n_blk - 1), _dot_nt(q, k_ref[pl.ds(ns, blk), ksl]), NEG_INF)
            s_ctx = _dot_nt(q, kc)
            o = _softmax_pv([s_prev, s_self, s_next, s_ctx],
                            [v_ref[pl.ds(ps, blk), ksl], v_ref[pl.ds(qs, blk), ksl], v_ref[pl.ds(ns, blk), ksl], vc],
                            sink=sink_q)
            o_ref[pl.ds(qs, blk), kv * grp * hd:(kv + 1) * grp * hd] = unstack(o, blk).astype(o_ref.dtype)
            return carry

        lax.fori_loop(0, n_blk, blk_body, 0)
        n_ctx = qc_ref.shape[0]
        if with_ctx:
            qc = stack(qc_ref, 0, n_ctx)
            oc = _softmax_pv([_dot_nt(qc, kc)], [vc], sink=sink_col(n_ctx))
            oc_ref[:, kv * grp * hd:(kv + 1) * grp * hd] = unstack(oc, n_ctx).astype(oc_ref.dtype)
        else:
            oc_ref[:, kv * grp * hd:(kv + 1) * grp * hd] = jnp.zeros((n_ctx, grp * hd), oc_ref.dtype)


def window_attention(qkv, sink, n_batch, seq, ctx_len, with_ctx):
    width = WA_HEADS * WA_HD
    ctx_blk0 = n_batch * seq // ctx_len
    kcol, vcol = width // LANES, width // LANES + 1
    return pl.pallas_call(
        functools.partial(_wa_body, seq // WINDOW, with_ctx),
        out_shape=(jax.ShapeDtypeStruct((n_batch * seq, width), BF16), jax.ShapeDtypeStruct((n_batch * ctx_len, width), BF16)),
        grid=(n_batch,),
        in_specs=[pl.BlockSpec(memory_space=pltpu.SMEM),
                  pl.BlockSpec((seq, width), lambda b: (b, 0)),
                  pl.BlockSpec((seq, LANES), lambda b: (b, kcol)),
                  pl.BlockSpec((seq, LANES), lambda b: (b, vcol)),
                  pl.BlockSpec((ctx_len, width), lambda b: (ctx_blk0 + b, 0)),
                  pl.BlockSpec((ctx_len, LANES), lambda b: (ctx_blk0 + b, kcol)),
                  pl.BlockSpec((ctx_len, LANES), lambda b: (ctx_blk0 + b, vcol))],
        out_specs=(pl.BlockSpec((seq, width), lambda b: (b, 0)), pl.BlockSpec((ctx_len, width), lambda b: (b, 0))),
        compiler_params=_params("parallel"),
        name="window_attention",
    )(sink, qkv, qkv, qkv, qkv, qkv, qkv)


CONV_ROWS = 256
CONV_HALO = 8


def _conv_silu_cols(src_ref, src_cols, w_ref, w_cols, pad_ref, dst_ref, dst_row0, dst_cols):
    n_rows = src_ref.shape[0]
    n_blk = n_rows // CONV_ROWS
    zeros = jnp.zeros((CONV_HALO, LANES), F32)
    pad_ref[0:CONV_HALO, :] = zeros
    pad_ref[CONV_HALO + n_rows:2 * CONV_HALO + n_rows, :] = zeros

    def copy(i, carry):
        r0 = pl.multiple_of(i * CONV_ROWS, CONV_ROWS)
        pad_ref[pl.ds(CONV_HALO + r0, CONV_ROWS), :] = src_ref[pl.ds(r0, CONV_ROWS), src_cols]
        return carry

    lax.fori_loop(0, n_blk, copy, 0)
    first = CONV_HALO - CONV_K // 2

    def conv(i, carry):
        r0 = pl.multiple_of(i * CONV_ROWS, CONV_ROWS)
        win = pad_ref[pl.ds(r0, CONV_ROWS + 2 * CONV_HALO), :]
        acc = win[first:first + CONV_ROWS] * w_ref[0:1, w_cols]
        for j in range(1, CONV_K):
            acc = acc + win[first + j:first + j + CONV_ROWS] * w_ref[j:j + 1, w_cols]
        dst_ref[pl.ds(dst_row0 + r0, CONV_ROWS), dst_cols] = _silu(acc)
        return carry

    lax.fori_loop(0, n_blk, conv, 0)


def _tri(upper):
    r = lax.broadcasted_iota(jnp.int32, (CHUNK, CHUNK), 0)
    c = lax.broadcasted_iota(jnp.int32, (CHUNK, CHUNK), 1)
    return (r <= c) if upper else (r >= c)


def _rows_as_lanes(a):
    return jnp.concatenate([a, jnp.zeros((LANES - CHUNK, LANES), F32)], axis=0).T[:, 0:CHUNK]


def _softplus(x):
    return jnp.maximum(x, 0.0) + jnp.log1p(jnp.exp(-jnp.abs(x)))


NEUMANN_STEPS = 5


def _dn_body(q_ref, k_ref, v_ref, g_ref, qc_ref, kc_ref, vc_ref, gc_ref, wq_ref, wk_ref, wv_ref, alv_ref, dtv_ref,
             o_ref, oc_ref, pad_ref, qs_ref, ks_ref, vs_ref, gb_ref, st_ref):
    n_ctx, n_lat = qc_ref.shape[0], q_ref.shape[0]
    n_rows = n_ctx + n_lat
    head = pl.program_id(1)
    full = slice(0, LANES)
    for src_c, src, w_ref, dst in ((qc_ref, q_ref, wq_ref, qs_ref), (kc_ref, k_ref, wk_ref, ks_ref), (vc_ref, v_ref, wv_ref, vs_ref)):
        _conv_silu_cols(src_c, full, w_ref, full, pad_ref, dst, 0, full)
        _conv_silu_cols(src, full, w_ref, full, pad_ref, dst, n_ctx, full)

    def l2(i, carry):
        r = pl.multiple_of(i * CONV_ROWS, CONV_ROWS)
        q = qs_ref[pl.ds(r, CONV_ROWS), :]
        qs_ref[pl.ds(r, CONV_ROWS), :] = q * lax.rsqrt(jnp.sum(q * q, axis=-1, keepdims=True) + EPS) * (DN_DK ** -0.5)
        k = ks_ref[pl.ds(r, CONV_ROWS), :]
        ks_ref[pl.ds(r, CONV_ROWS), :] = k * lax.rsqrt(jnp.sum(k * k, axis=-1, keepdims=True) + EPS)
        return carry

    lax.fori_loop(0, n_rows // CONV_ROWS, l2, 0)

    def gates(raw):
        lane = lax.broadcasted_iota(jnp.int32, raw.shape, 1)
        return jnp.where(lane < 2 * DN_HEADS, jax.nn.sigmoid(raw), alv_ref[...] * _softplus(raw + dtv_ref[...]))

    gb_ref[0:n_ctx, :] = gates(gc_ref[...])
    gb_ref[n_ctx:, :] = gates(g_ref[...])
    oc_ref[...] = jnp.zeros(oc_ref.shape, F32)
    o_ref[...] = jnp.zeros(o_ref.shape, F32)
    st_ref[...] = jnp.zeros(st_ref.shape, F32)
    lane = lax.broadcasted_iota(jnp.int32, (CHUNK, LANES), 1)
    diag = lax.broadcasted_iota(jnp.int32, (CHUNK, CHUNK), 0) == lax.broadcasted_iota(jnp.int32, (CHUNK, CHUNK), 1)

    def chunk(d, c, row0, out_ref):
        r = pl.multiple_of(row0 + c * CHUNK, CHUNK)
        ro = pl.multiple_of(c * CHUNK, CHUNK)
        keep = _tri(upper=(d == 1))
        strict = keep & jnp.logical_not(diag)
        q, k, v, gb = qs_ref[pl.ds(r, CHUNK), :], ks_ref[pl.ds(r, CHUNK), :], vs_ref[pl.ds(r, CHUNK), :], gb_ref[pl.ds(r, CHUNK), :]
        beta = jnp.sum(jnp.where(lane == d * DN_HEADS + head, gb, 0.0), axis=-1, keepdims=True)
        g = jnp.sum(jnp.where(lane == (2 + d) * DN_HEADS + head, gb, 0.0), axis=-1, keepdims=True)
        cum = jnp.dot(keep.astype(F32), jnp.broadcast_to(g, (CHUNK, LANES)), precision=HIGHEST, preferred_element_type=F32)
        cum_t = _rows_as_lanes(cum)
        decay = jnp.where(keep, jnp.exp(cum[:, 0:CHUNK] - cum_t[0:1, :]), 0.0)
        edge = cum[CHUNK - 1:CHUNK, :] if d == 0 else cum[0:1, :]
        kb = k * beta
        k16 = k.astype(BF16)
        a = jnp.where(strict, _dot_nt(kb.astype(BF16), k16) * decay, 0.0)
        attn = _dot_nt(q.astype(BF16), k16) * decay
        e_cum = jnp.exp(cum)
        rhs = jnp.concatenate([v * beta, kb * e_cum], axis=1)
        n = -a
        a16 = a.astype(BF16)
        p = _dot(a16, a16)
        for it in range(NEUMANN_STEPS):
            p16 = p.astype(BF16)
            n = n + p + _dot(n.astype(BF16), p16)
            if it + 1 < NEUMANN_STEPS:
                p = _dot(p16, p16)
        sol = rhs + _dot(n.astype(BF16), rhs.astype(BF16))
        u, w = sol[:, 0:DN_DK], sol[:, DN_DK:]
        state = st_ref[d]
        s16 = state.astype(BF16)
        v_new = u - _dot(w.astype(BF16), s16)
        out_ref[pl.ds(ro, CHUNK), :] += _dot((q * e_cum).astype(BF16), s16) + _dot(attn.astype(BF16), v_new.astype(BF16))
        st_ref[d] = state * jnp.exp(edge) + _dot_tn((k * jnp.exp(edge - cum)).astype(BF16), v_new.astype(BF16))

    def segment(n_chunks, row0, out_ref):
        def body(i, carry):
            chunk(0, i, row0, out_ref)
            chunk(1, n_chunks - 1 - i, row0, out_ref)
            return carry
        lax.fori_loop(0, n_chunks, body, 0)

    segment(n_ctx // CHUNK, 0, oc_ref)
    segment(n_lat // CHUNK, n_ctx, o_ref)


def dn_gate_vectors(a_log, dt_bias):
    def vec(v):
        return jnp.pad(v.astype(F32).reshape(1, -1), ((0, 0), (2 * DN_HEADS, LANES - 4 * DN_HEADS)))
    return vec(-jnp.exp(a_log.astype(F32))), vec(dt_bias)


def deltanet(dqkv, dg, conv_w, alv, dtv, n_batch, seq, ctx_len):
    nh = DN_HEADS
    ctx_blk0 = n_batch * seq // ctx_len
    n_rows = seq + ctx_len
    lat = lambda off: pl.BlockSpec((seq, LANES), lambda b, h: (b, off + h))
    ctx = lambda off: pl.BlockSpec((ctx_len, LANES), lambda b, h: (ctx_blk0 + b, off + h))
    cw = lambda off: pl.BlockSpec((CONV_K, LANES), lambda b, h: (0, off + h))
    vec = pl.BlockSpec((1, LANES), lambda b, h: (0, 0))
    seq_buf = pltpu.VMEM((n_rows, LANES), F32)
    return pl.pallas_call(
        _dn_body,
        out_shape=(jax.ShapeDtypeStruct((n_batch * seq, nh * DN_DK), F32), jax.ShapeDtypeStruct((n_batch * ctx_len, nh * DN_DK), F32)),
        grid=(n_batch, nh),
        in_specs=[lat(0), lat(nh), lat(2 * nh), pl.BlockSpec((seq, LANES), lambda b, h: (b, 0)),
                  ctx(0), ctx(nh), ctx(2 * nh), pl.BlockSpec((ctx_len, LANES), lambda b, h: (ctx_blk0 + b, 0)),
                  cw(0), cw(nh), cw(2 * nh), vec, vec],
        out_specs=(pl.BlockSpec((seq, LANES), lambda b, h: (b, h)), pl.BlockSpec((ctx_len, LANES), lambda b, h: (b, h))),
        scratch_shapes=[pltpu.VMEM((seq + 2 * CONV_HALO, LANES), F32), seq_buf, seq_buf, seq_buf, seq_buf,
                        pltpu.VMEM((2, DN_DK, DN_DK), F32)],
        compiler_params=_params("parallel", "arbitrary"),
        name="deltanet",
    )(dqkv, dqkv, dqkv, dg, dqkv, dqkv, dqkv, dg, conv_w, conv_w, conv_w, alv, dtv)


def _ssd_body(xbc_ref, dt_ref, xbc_c_ref, dt_c_ref, cw_ref, av_ref, dtb_ref, dsk_ref, y_ref, yc_ref,
              pad_ref, s_ref, dts_ref, da_ref, hs_ref):
    n_ctx, n_lat = xbc_c_ref.shape[0], xbc_ref.shape[0]
    inner = SSD_HEADS * SSD_HD
    hpg = SSD_HEADS // SSD_GROUPS
    gw = hpg * SSD_HD
    for cb in range(xbc_ref.shape[1] // LANES):
        cols = slice(cb * LANES, (cb + 1) * LANES)
        _conv_silu_cols(xbc_c_ref, cols, cw_ref, cols, pad_ref, s_ref, 0, cols)
        _conv_silu_cols(xbc_ref, cols, cw_ref, cols, pad_ref, s_ref, n_ctx, cols)
    dts_ref[0:n_ctx, :] = _softplus(dt_c_ref[...] + dtb_ref[...])
    dts_ref[n_ctx:, :] = _softplus(dt_ref[...] + dtb_ref[...])
    da_ref[...] = dts_ref[...] * av_ref[...]
    yc_ref[...] = s_ref[0:n_ctx, 0:inner] * dsk_ref[...]
    y_ref[...] = s_ref[n_ctx:, 0:inner] * dsk_ref[...]
    hs_ref[...] = jnp.zeros(hs_ref.shape, F32)

    def chunk(d, c, row0, out_ref):
        r = pl.multiple_of(row0 + c * CHUNK, CHUNK)
        ro = pl.multiple_of(c * CHUNK, CHUNK)
        keep = _tri(upper=(d == 1))
        acum = jnp.dot(keep.astype(F32), da_ref[pl.ds(r, CHUNK), :], precision=HIGHEST, preferred_element_type=F32)
        acum_t = _rows_as_lanes(acum)
        edge = acum[CHUNK - 1:CHUNK, :] if d == 0 else acum[0:1, :]
        dt = dts_ref[pl.ds(r, CHUNK), :]
        for g in range(SSD_GROUPS):
            lanes = [d * SSD_HEADS + g * hpg + h for h in range(hpg)]

            def expand(v, rows):
                return jnp.concatenate([jnp.broadcast_to(v[:, l:l + 1], (rows, SSD_HD)) for l in lanes], axis=1)

            x4 = s_ref[pl.ds(r, CHUNK), g * gw:(g + 1) * gw]
            bm = s_ref[pl.ds(r, CHUNK), inner + g * SSD_STATE:inner + (g + 1) * SSD_STATE].astype(BF16)
            cm = s_ref[pl.ds(r, CHUNK), inner + (SSD_GROUPS + g) * SSD_STATE:inner + (SSD_GROUPS + g + 1) * SSD_STATE].astype(BF16)
            cb_mat = _dot_nt(cm, bm)
            a4 = expand(acum, CHUNK)
            e4 = expand(edge, 1)
            xd = x4 * expand(dt, CHUNK)
            prev = hs_ref[d, g]
            y = _dot_nt(cm, prev.astype(BF16)) * jnp.exp(a4)
            diag = []
            for h, l in enumerate(lanes):
                lmat = jnp.where(keep, jnp.exp(acum[:, l:l + 1] - acum_t[l:l + 1, :]), 0.0)
                diag.append(_dot((cb_mat * lmat).astype(BF16), xd[:, h * SSD_HD:(h + 1) * SSD_HD].astype(BF16)))
            y = y + jnp.concatenate(diag, axis=1)
            out_ref[pl.ds(ro, CHUNK), g * gw:(g + 1) * gw] += y
            states = _dot_tn((xd * jnp.exp(e4 - a4)).astype(BF16), bm)
            decay = jnp.concatenate([jnp.broadcast_to(jnp.exp(edge[:, l:l + 1]), (SSD_HD, SSD_STATE)) for l in lanes], axis=0)
            hs_ref[d, g] = prev * decay + states

    def segment(n_chunks, row0, out_ref):
        def body(i, carry):
            chunk(0, i, row0, out_ref)
            chunk(1, n_chunks - 1 - i, row0, out_ref)
            return carry
        lax.fori_loop(0, n_chunks, body, 0)

    segment(n_ctx // CHUNK, 0, yc_ref)
    segment(n_lat // CHUNK, n_ctx, y_ref)


def ssd(xbc, dt, conv_w, a_vec, dtb_vec, skip_vec, n_batch, seq, ctx_len):
    width = xbc.shape[1]
    inner = SSD_HEADS * SSD_HD
    ctx_blk0 = n_batch * seq // ctx_len
    n_rows = seq + ctx_len
    return pl.pallas_call(
        _ssd_body,
        out_shape=(jax.ShapeDtypeStruct((n_batch * seq, inner), F32), jax.ShapeDtypeStruct((n_batch * ctx_len, inner), F32)),
        grid=(n_batch,),
        in_specs=[pl.BlockSpec((seq, width), lambda b: (b, 0)), pl.BlockSpec((seq, LANES), lambda b: (b, 0)),
                  pl.BlockSpec((ctx_len, width), lambda b: (ctx_blk0 + b, 0)), pl.BlockSpec((ctx_len, LANES), lambda b: (ctx_blk0 + b, 0)),
                  _resident(conv_w.shape), _resident(a_vec.shape), _resident(dtb_vec.shape), _resident(skip_vec.shape)],
        out_specs=(pl.BlockSpec((seq, inner), lambda b: (b, 0)), pl.BlockSpec((ctx_len, inner), lambda b: (b, 0))),
        scratch_shapes=[pltpu.VMEM((seq + 2 * CONV_HALO, LANES), F32), pltpu.VMEM((n_rows, width), F32),
                        pltpu.VMEM((n_rows, LANES), F32), pltpu.VMEM((n_rows, LANES), F32),
                        pltpu.VMEM((2, SSD_GROUPS, (SSD_HEADS // SSD_GROUPS) * SSD_HD, SSD_STATE), F32)],
        compiler_params=_params("parallel"),
        name="ssd",
    )(xbc, dt, xbc, dt, conv_w, a_vec, dtb_vec, skip_vec)


def ssd_vectors(a_log, dt_bias, d_skip):
    def vec(v):
        return jnp.pad(v.astype(F32).reshape(1, -1), ((0, 0), (0, LANES - 2 * SSD_HEADS)))
    return vec(-jnp.exp(a_log.astype(F32))), vec(dt_bias), jnp.repeat(d_skip.astype(F32), SSD_HD)[None, :]


def kernel(x, c, ctx, c_ctx, ada_w, ada_b, norm_w, ev_w_in, ev_w_out, na_rpb, dn_conv, dn_a_log, dn_dt_bias, dn_norm, ffn_w1, ffn_w3, ffn_w2, od_w_in, od_w_out, wa_sink, ssd_conv, ssd_a_log, ssd_dt_bias, ssd_d, ssd_norm, moe_router, moe_w1, moe_w3, moe_w2):
    n_batch, seq, d = x.shape
    ctx_len = ctx.shape[1]
    depth = ada_w.shape[0]
    n_lat, n_ctx = n_batch * seq, n_batch * ctx_len
    n_cond = -(-(n_batch + 1) // 8) * 8
    cond = jnp.concatenate([c, c_ctx[None, :], jnp.zeros((n_cond - n_batch - 1, d), F32)], axis=0)
    mod_all = ada_modulation(cond, ada_w, ada_b)
    x_all = jnp.concatenate([x.reshape(n_lat, d), ctx.reshape(n_ctx, d)], axis=0)
    cos_t, sin_t = rope_tables(seq)
    for layer in range(depth):
        i = layer // 2
        n_tok = n_lat + n_ctx if layer < depth - 1 else n_lat
        mod = mod_all[layer, :n_batch + 1].reshape(n_batch + 1, 6, d)
        nw = norm_w[layer]
        if layer % 2 == 0:
            att, dqkv, dz, dg = proj_in_even(x_all, mod, nw, prep_w_in_even(ev_w_in[i]), n_batch, seq)
            o_a, oc_a = na_attention(att, na_bias_tables(na_rpb[i]), n_batch, seq, ctx_len)
            alv, dtv = dn_gate_vectors(dn_a_log[i], dn_dt_bias[i])
            o_b, oc_b = deltanet(dqkv, dg, dn_conv[i], alv, dtv, n_batch, seq, ctx_len)
            x_all = proj_out(jnp.concatenate([o_a, oc_a], axis=0), jnp.concatenate([o_b, oc_b], axis=0), dz,
                             jnp.tile(dn_norm[i], DN_HEADS)[None, :], ev_w_out[i].astype(BF16), x_all, mod, nw,
                             n_batch, seq, n_tok, DN_DK, False)
            x_all = ffn_dense(x_all, mod, nw, ffn_w1[i].astype(BF16), ffn_w3[i].astype(BF16), ffn_w2[i].astype(BF16),
                              n_batch, seq, n_tok)
        else:
            qkv, z, xbc, dt = proj_in_odd(x_all, mod, nw, prep_w_in_odd(od_w_in[i]), cos_t, sin_t, n_batch, seq)
            o_c, oc_c = window_attention(qkv, wa_sink[i], n_batch, seq, ctx_len, n_tok > n_lat)
            a_vec, dtb_vec, skip_vec = ssd_vectors(ssd_a_log[i], ssd_dt_bias[i], ssd_d[i])
            y_d, yc_d = ssd(xbc, dt, ssd_conv[i], a_vec, dtb_vec, skip_vec, n_batch, seq, ctx_len)
            x_all = proj_out(jnp.concatenate([o_c, oc_c], axis=0), jnp.concatenate([y_d, yc_d], axis=0), z,
                             ssd_norm[i][None, :], od_w_out[i].astype(BF16), x_all, mod, nw,
                             n_batch, seq, n_tok, SSD_HEADS * SSD_HD // SSD_GROUPS, True)
            router = jnp.pad(moe_router[i], ((0, 0), (0, LANES - N_EXPERTS)))
            x_all = moe_layer(x_all, mod, nw, router, moe_w1[i].astype(BF16), moe_w3[i].astype(BF16),
                              moe_w2[i].astype(BF16), n_batch, seq, n_tok)
    return x_all[:n_lat].reshape(n_batch, seq, d)
```

```python
import functools
import math

import jax
import jax.numpy as jnp
import numpy as np
from jax import lax
from jax.experimental import pallas as pl
from jax.experimental.pallas import tpu as pltpu

F32 = jnp.float32
BF16 = jnp.bfloat16
HIGHEST = lax.Precision.HIGHEST

EPS = 1e-6
NEG_INF = -1e30
GRID_W = 64
CHUNK = 64
CONV_K = 5
ROPE_BASE = 10000.0
NA_HEADS, NA_HD, NA_WIN_ROWS, NA_WIN_COLS = 8, 64, 8, 16
DN_HEADS, DN_DK = 4, 128
WA_HEADS, WA_KV_HEADS, WA_HD, WINDOW = 8, 2, 64, 128
WA_GROUP = WA_HEADS // WA_KV_HEADS
SSD_HEADS, SSD_HD, SSD_GROUPS, SSD_STATE = 8, 64, 2, 128
N_EXPERTS, TOP_K = 8, 2

LANES = 128
VMEM_LIMIT = 56 * 1024 * 1024
TOK_TILE = 512


def _params(*sem):
    return pltpu.CompilerParams(dimension_semantics=sem, vmem_limit_bytes=VMEM_LIMIT)


def _resident(shape):
    nd = len(shape)
    return pl.BlockSpec(shape, lambda *_: (0,) * nd, pipeline_mode=pl.Buffered(1))


def _silu(x):
    return x * jax.nn.sigmoid(x)


def _rms(x, w):
    return x * lax.rsqrt(jnp.mean(x * x, axis=-1, keepdims=True) + EPS) * w


def _dot(a, b):
    return jnp.dot(a, b, preferred_element_type=F32)


def _dot_nt(a, b):
    return lax.dot_general(a, b, (((1,), (1,)), ((), ())), preferred_element_type=F32)


def _dot_tn(a, b):
    return lax.dot_general(a, b, (((0,), (0,)), ((), ())), preferred_element_type=F32)


def _ada_body(s_ref, w_ref, b_ref, o_ref):
    s = _silu(s_ref[...])
    o_ref[0] = jnp.dot(s, w_ref[0], precision=HIGHEST, preferred_element_type=F32) + b_ref[0]


def ada_modulation(cond, ada_w, ada_b):
    depth, d, n = ada_w.shape
    r = cond.shape[0]
    tn = 1536
    return pl.pallas_call(
        _ada_body,
        out_shape=jax.ShapeDtypeStruct((depth, r, n), F32),
        grid=(depth, n // tn),
        in_specs=[
            pl.BlockSpec((r, d), lambda l, j: (0, 0)),
            pl.BlockSpec((1, d, tn), lambda l, j: (l, 0, j)),
            pl.BlockSpec((1, 1, tn), lambda l, j: (l, 0, j)),
        ],
        out_specs=pl.BlockSpec((1, r, tn), lambda l, j: (l, 0, j)),
        compiler_params=_params("arbitrary", "arbitrary"),
        name="ada_modulation",
    )(cond, ada_w, ada_b.reshape(depth, 1, n))


def _mod_rows(n_lat_tiles, tiles_per_batch, n_batch):
    def index(i):
        return (jnp.minimum(i // tiles_per_batch, n_batch), 0, 0)
    del n_lat_tiles
    return index


def _norm_mod(x, mod_ref, nw_ref, row):
    y = _rms(x, nw_ref[row:row + 1, :])
    return y * (1.0 + mod_ref[0, row + 1:row + 2, :]) + mod_ref[0, row:row + 1, :]


def _proj_even_body(x_ref, mod_ref, nw_ref, w_ref, att_ref, dqkv_ref, dz_ref, dg_ref):
    hb = _norm_mod(x_ref[...], mod_ref, nw_ref, 0).astype(BF16)
    att_ref[...] = _dot(hb, w_ref[:, 0:1536]).astype(BF16)
    dqkv_ref[...] = _dot(hb, w_ref[:, 1536:3072])
    dz_ref[...] = _dot(hb, w_ref[:, 3072:3584])
    dg_ref[...] = _dot(hb, w_ref[:, 3584:3712])


def _swap16(x):
    lane = lax.broadcasted_iota(jnp.int32, x.shape, 1)
    up = pltpu.roll(x, LANES - 16, 1)
    down = pltpu.roll(x, 16, 1)
    return jnp.where((lane % 32) < 16, up, down)


def _proj_odd_body(x_ref, mod_ref, nw_ref, w_ref, cos_ref, sin_ref, qkv_ref, z_ref, xbc_ref, dt_ref):
    hb = _norm_mod(x_ref[...], mod_ref, nw_ref, 0).astype(BF16)
    cos, sin = cos_ref[...], sin_ref[...]
    for c in range(5):
        t = _dot(hb, w_ref[:, c * LANES:(c + 1) * LANES])
        qkv_ref[:, c * LANES:(c + 1) * LANES] = (t * cos + _swap16(t) * sin).astype(BF16)
    qkv_ref[:, 640:768] = _dot(hb, w_ref[:, 640:768]).astype(BF16)
    z_ref[...] = _dot(hb, w_ref[:, 768:1280])
    xbc_ref[...] = _dot(hb, w_ref[:, 1280:2304])
    dt_ref[...] = _dot(hb, w_ref[:, 2304:2432])


def _pad_cols(w, n):
    return jnp.pad(w, ((0, 0), (0, n - w.shape[1])))


def prep_w_in_even(w):
    q_scale = jnp.concatenate([jnp.full((NA_HEADS * NA_HD,), NA_HD ** -0.5, F32), jnp.ones((w.shape[1] - NA_HEADS * NA_HD,), F32)])
    return _pad_cols(w * q_scale[None, :], 3712).astype(BF16)


def prep_w_in_odd(w):
    q_scale = jnp.concatenate([jnp.full((WA_HEADS * WA_HD,), WA_HD ** -0.5, F32), jnp.ones((w.shape[1] - WA_HEADS * WA_HD,), F32)])
    return _pad_cols(w * q_scale[None, :], 2432).astype(BF16)


def _tok_spec(width, tm=TOK_TILE):
    return pl.BlockSpec((tm, width), lambda i: (i, 0))


def proj_in_even(x_all, mod, nw, w, n_batch, seq):
    t_all, d = x_all.shape
    tm = TOK_TILE
    tpb = seq // tm
    return pl.pallas_call(
        _proj_even_body,
        out_shape=(jax.ShapeDtypeStruct((t_all, 1536), BF16), jax.ShapeDtypeStruct((t_all, 1536), F32),
                   jax.ShapeDtypeStruct((t_all, 512), F32), jax.ShapeDtypeStruct((t_all, LANES), F32)),
        grid=(t_all // tm,),
        in_specs=[_tok_spec(d), pl.BlockSpec((1, 6, d), _mod_rows(None, tpb, n_batch)),
                  _resident(nw.shape), _resident(w.shape)],
        out_specs=(_tok_spec(1536), _tok_spec(1536), _tok_spec(512), _tok_spec(LANES)),
        compiler_params=_params("parallel"),
        name="proj_in_even",
    )(x_all, mod, nw, w)


def proj_in_odd(x_all, mod, nw, w, cos_t, sin_t, n_batch, seq):
    t_all, d = x_all.shape
    tm = TOK_TILE
    tpb = seq // tm

    def rope_rows(i):
        return (jnp.where(i < n_batch * tpb, i % tpb, tpb), 0)

    return pl.pallas_call(
        _proj_odd_body,
        out_shape=(jax.ShapeDtypeStruct((t_all, 768), BF16), jax.ShapeDtypeStruct((t_all, 512), F32),
                   jax.ShapeDtypeStruct((t_all, 1024), F32), jax.ShapeDtypeStruct((t_all, LANES), F32)),
        grid=(t_all // tm,),
        in_specs=[_tok_spec(d), pl.BlockSpec((1, 6, d), _mod_rows(None, tpb, n_batch)),
                  _resident(nw.shape), _resident(w.shape),
                  pl.BlockSpec((tm, LANES), rope_rows), pl.BlockSpec((tm, LANES), rope_rows)],
        out_specs=(_tok_spec(768), _tok_spec(512), _tok_spec(1024), _tok_spec(LANES)),
        compiler_params=_params("parallel"),
        name="proj_in_odd",
    )(x_all, mod, nw, w, cos_t, sin_t)


def rope_tables(seq, tm=TOK_TILE):
    half = WA_HD // 4
    freqs = ROPE_BASE ** (-jnp.arange(half, dtype=F32) / half)
    t = jnp.arange(seq)
    ang_r = (t // GRID_W).astype(F32)[:, None] * freqs[None, :]
    ang_c = (t % GRID_W).astype(F32)[:, None] * freqs[None, :]
    cos = jnp.concatenate([jnp.cos(ang_r)] * 2 + [jnp.cos(ang_c)] * 2, axis=-1)
    sin = jnp.concatenate([-jnp.sin(ang_r), jnp.sin(ang_r), -jnp.sin(ang_c), jnp.sin(ang_c)], axis=-1)
    cos = jnp.concatenate([jnp.tile(cos, (1, 2)), jnp.ones((tm, LANES), F32)], axis=0)
    sin = jnp.concatenate([jnp.tile(sin, (1, 2)), jnp.zeros((tm, LANES), F32)], axis=0)
    return cos, sin


def _proj_out_body(group, gate_first, a_ref, r_ref, z_ref, gw_ref, w_ref, x_ref, mod_ref, nw_ref, o_ref):
    r = r_ref[...]
    gate = _silu(z_ref[...])
    if gate_first:
        r = r * gate
    parts = []
    for c in range(r.shape[1] // group):
        rc = r[:, c * group:(c + 1) * group]
        parts.append(_rms(rc, gw_ref[:, c * group:(c + 1) * group]))
    r = jnp.concatenate(parts, axis=1)
    if not gate_first:
        r = r * gate
    half = a_ref.shape[1]
    y = _dot(a_ref[...], w_ref[0:half, :]) + _dot(r.astype(BF16), w_ref[half:, :])
    o_ref[...] = x_ref[...] + mod_ref[0, 2:3, :] * _rms(y, nw_ref[1:2, :])


def proj_out(att, rec, z, gate_w, w_out, x_all, mod, nw, n_batch, seq, n_tok, group, gate_first):
    d = x_all.shape[1]
    tm = TOK_TILE
    tpb = seq // tm
    half = att.shape[1]
    return pl.pallas_call(
        functools.partial(_proj_out_body, group, gate_first),
        out_shape=jax.ShapeDtypeStruct((n_tok, d), F32),
        grid=(n_tok // tm,),
        in_specs=[_tok_spec(half), _tok_spec(half), _tok_spec(half), _resident(gate_w.shape),
                  _resident(w_out.shape), _tok_spec(d), pl.BlockSpec((1, 6, d), _mod_rows(None, tpb, n_batch)),
                  _resident(nw.shape)],
        out_specs=_tok_spec(d),
        compiler_params=_params("parallel"),
        name="proj_out",
    )(att, rec, z, gate_w, w_out, x_all, mod, nw)


def _ffn_dense_body(n_split, x_ref, mod_ref, nw_ref, w1_ref, w3_ref, w2_ref, o_ref):
    x = x_ref[...]
    y = _rms(x, nw_ref[2:3, :])
    hb = (y * (1.0 + mod_ref[0, 4:5, :]) + mod_ref[0, 3:4, :]).astype(BF16)
    ff = w1_ref.shape[1]
    step = ff // n_split
    acc = None
    for c in range(n_split):
        sl = slice(c * step, (c + 1) * step)
        act = (_silu(_dot(hb, w1_ref[:, sl])) * _dot(hb, w3_ref[:, sl])).astype(BF16)
        part = _dot(act, w2_ref[sl, :])
        acc = part if acc is None else acc + part
    o_ref[...] = x + mod_ref[0, 5:6, :] * _rms(acc, nw_ref[3:4, :])


def ffn_dense(x_all, mod, nw, w1, w3, w2, n_batch, seq, n_tok):
    d = x_all.shape[1]
    tm = TOK_TILE
    tpb = seq // tm
    return pl.pallas_call(
        functools.partial(_ffn_dense_body, 2),
        out_shape=jax.ShapeDtypeStruct((n_tok, d), F32),
        grid=(n_tok // tm,),
        in_specs=[_tok_spec(d), pl.BlockSpec((1, 6, d), _mod_rows(None, tpb, n_batch)), _resident(nw.shape),
                  _resident(w1.shape), _resident(w3.shape), _resident(w2.shape)],
        out_specs=_tok_spec(d),
        compiler_params=_params("parallel"),
        name="ffn_dense",
    )(x_all, mod, nw, w1, w3, w2)


MOE_TILE = 512
MOE_FF_TILE = 512


def _moe_route_body(x_ref, mod_ref, nw_ref, r_ref, h_ref, idx_ref, gate_ref):
    y = _rms(x_ref[...], nw_ref[2:3, :])
    h = y * (1.0 + mod_ref[0, 4:5, :]) + mod_ref[0, 3:4, :]
    h_ref[...] = h.astype(BF16)
    logits = jnp.dot(h, r_ref[...], precision=HIGHEST, preferred_element_type=F32)
    lane = lax.broadcasted_iota(jnp.int32, logits.shape, 1)
    logits = jnp.where(lane < N_EXPERTS, logits, -jnp.inf)
    v1 = jnp.max(logits, axis=-1, keepdims=True)
    i1 = jnp.min(jnp.where(logits == v1, lane, LANES), axis=-1, keepdims=True)
    rest = jnp.where(lane == i1, -jnp.inf, logits)
    v2 = jnp.max(rest, axis=-1, keepdims=True)
    i2 = jnp.min(jnp.where(rest == v2, lane, LANES), axis=-1, keepdims=True)
    e2 = jnp.exp(v2 - v1)
    g1 = 1.0 / (1.0 + e2)
    idx_ref[...] = jnp.where(lane == 0, i1, jnp.where(lane == 1, i2, 0))
    gate_ref[...] = jnp.where(lane == 0, g1, jnp.where(lane == 1, e2 * g1, 0.0))


def moe_route(x_all, mod, nw, router, n_batch, seq, n_tok):
    d = x_all.shape[1]
    tm = TOK_TILE
    tpb = seq // tm
    return pl.pallas_call(
        _moe_route_body,
        out_shape=(jax.ShapeDtypeStruct((n_tok, d), BF16), jax.ShapeDtypeStruct((n_tok, LANES), jnp.int32),
                   jax.ShapeDtypeStruct((n_tok, LANES), F32)),
        grid=(n_tok // tm,),
        in_specs=[_tok_spec(d), pl.BlockSpec((1, 6, d), _mod_rows(None, tpb, n_batch)), _resident(nw.shape),
                  _resident(router.shape)],
        out_specs=(_tok_spec(d), _tok_spec(LANES), _tok_spec(LANES)),
        compiler_params=_params("parallel"),
        name="moe_route",
    )(x_all, mod, nw, router)


def _moe_gmm_body(te_ref, nv_ref, x_ref, w1_ref, w3_ref, w2_ref, o_ref, acc_ref):
    i, f = pl.program_id(0), pl.program_id(1)

    @pl.when(i < nv_ref[0])
    def _():
        x = x_ref[...]
        act = (_silu(_dot(x, w1_ref[0])) * _dot(x, w3_ref[0])).astype(BF16)
        part = _dot(act, w2_ref[0])

        @pl.when(f == 0)
        def _():
            acc_ref[...] = part

        @pl.when(f > 0)
        def _():
            acc_ref[...] += part

        @pl.when(f == pl.num_programs(1) - 1)
        def _():
            o_ref[...] = acc_ref[...].astype(o_ref.dtype)

    @pl.when((i >= nv_ref[0]) & (f == pl.num_programs(1) - 1))
    def _():
        o_ref[...] = jnp.zeros(o_ref.shape, o_ref.dtype)


def moe_gmm(tile_expert, n_valid, xs, w1, w3, w2):
    p, d = xs.shape
    ff = w1.shape[2]
    tm, tf = MOE_TILE, MOE_FF_TILE
    nf = ff // tf

    def wcol(i, f, te, nv):
        live = i < nv[0]
        return (te[i], 0, jnp.where(live, f, nf - 1))

    def wrow(i, f, te, nv):
        live = i < nv[0]
        return (te[i], jnp.where(live, f, nf - 1), 0)

    return pl.pallas_call(
        _moe_gmm_body,
        out_shape=jax.ShapeDtypeStruct((p, d), BF16),
        grid_spec=pltpu.PrefetchScalarGridSpec(
            num_scalar_prefetch=2,
            grid=(p // tm, nf),
            in_specs=[pl.BlockSpec((tm, d), lambda i, f, te, nv: (i, 0)),
                      pl.BlockSpec((1, d, tf), wcol), pl.BlockSpec((1, d, tf), wcol), pl.BlockSpec((1, tf, d), wrow)],
            out_specs=pl.BlockSpec((tm, d), lambda i, f, te, nv: (i, 0)),
            scratch_shapes=[pltpu.VMEM((tm, d), F32)]),
        compiler_params=_params("arbitrary", "arbitrary"),
        name="moe_gmm",
    )(tile_expert, n_valid, xs, w1, w3, w2)


def moe_dispatch(idx, n_tok):
    tm = MOE_TILE
    n_assign = n_tok * TOP_K
    p = n_assign + N_EXPERTS * tm
    e_flat = idx[:, :TOP_K].reshape(n_assign)
    onehot = (e_flat[:, None] == jnp.arange(N_EXPERTS)[None, :]).astype(jnp.int32)
    rank = jnp.take_along_axis(jnp.cumsum(onehot, axis=0) - onehot, e_flat[:, None], axis=1)[:, 0]
    counts = jnp.sum(onehot, axis=0)
    padded = ((counts + tm - 1) // tm) * tm
    ends = jnp.cumsum(padded)
    starts = ends - padded
    pos = starts[e_flat] + rank
    src_tok = jnp.zeros((p,), jnp.int32).at[pos].set(jnp.arange(n_assign, dtype=jnp.int32) // TOP_K)
    tile_start = jnp.arange(p // tm, dtype=jnp.int32) * tm
    tile_expert = jnp.minimum(jnp.sum((tile_start[:, None] >= ends[None, :]).astype(jnp.int32), axis=1), N_EXPERTS - 1)
    n_valid = (ends[-1] // tm).astype(jnp.int32).reshape(1)
    last_live = jnp.take(tile_expert, jnp.maximum(n_valid - 1, 0))
    tile_expert = jnp.where(tile_start < ends[-1], tile_expert, last_live).astype(jnp.int32)
    return src_tok, pos.reshape(n_tok, TOP_K), tile_expert, n_valid


def _moe_combine_body(y1_ref, y2_ref, gate_ref, x_ref, mod_ref, nw_ref, o_ref):
    g = gate_ref[...]
    ffn = y1_ref[...].astype(F32) * g[:, 0:1] + y2_ref[...].astype(F32) * g[:, 1:2]
    o_ref[...] = x_ref[...] + mod_ref[0, 5:6, :] * _rms(ffn, nw_ref[3:4, :])


def moe_combine(y1, y2, gates, x_all, mod, nw, n_batch, seq, n_tok):
    d = x_all.shape[1]
    tm = TOK_TILE
    tpb = seq // tm
    return pl.pallas_call(
        _moe_combine_body,
        out_shape=jax.ShapeDtypeStruct((n_tok, d), F32),
        grid=(n_tok // tm,),
        in_specs=[_tok_spec(d), _tok_spec(d), _tok_spec(LANES), _tok_spec(d),
                  pl.BlockSpec((1, 6, d), _mod_rows(None, tpb, n_batch)), _resident(nw.shape)],
        out_specs=_tok_spec(d),
        compiler_params=_params("parallel"),
        name="moe_combine",
    )(y1, y2, gates, x_all, mod, nw)


def moe_layer(x_all, mod, nw, router, w1, w3, w2, n_batch, seq, n_tok):
    h, idx, gates = moe_route(x_all, mod, nw, router, n_batch, seq, n_tok)
    src_tok, pos, tile_expert, n_valid = moe_dispatch(idx, n_tok)
    ys = moe_gmm(tile_expert, n_valid, jnp.take(h, src_tok, axis=0), w1, w3, w2)
    y1 = jnp.take(ys, pos[:, 0], axis=0)
    y2 = jnp.take(ys, pos[:, 1], axis=0)
    return moe_combine(y1, y2, gates, x_all, mod, nw, n_batch, seq, n_tok)


NA_QROWS = 4
NA_KROWS = NA_QROWS + NA_WIN_ROWS - 1


def _na_key_start(blk, rows):
    return jnp.clip(blk * NA_QROWS - NA_WIN_ROWS // 2, 0, rows - NA_KROWS)


def na_bias_tables(rpb, rows):
    n_blk = rows // NA_QROWS
    blk = jnp.array([0, 1, n_blk - 1])
    q_row = blk[:, None] * NA_QROWS + jnp.arange(NA_QROWS)[None, :]
    k_row = _na_key_start(blk, rows)[:, None] + jnp.arange(NA_KROWS)[None, :]
    q_start = jnp.clip(q_row - NA_WIN_ROWS // 2, 0, rows - NA_WIN_ROWS)
    row_ok = (k_row[:, None, :] >= q_start[:, :, None]) & (k_row[:, None, :] < q_start[:, :, None] + NA_WIN_ROWS)
    dr = jnp.clip(k_row[:, None, :] - q_row[:, :, None] + NA_WIN_ROWS - 1, 0, 2 * NA_WIN_ROWS - 2)
    col = jnp.arange(GRID_W)
    col_start = jnp.clip(col - NA_WIN_COLS // 2, 0, GRID_W - NA_WIN_COLS)
    col_ok = (col[None, :] >= col_start[:, None]) & (col[None, :] < col_start[:, None] + NA_WIN_COLS)
    dc = jnp.clip(col[None, :] - col[:, None] + NA_WIN_COLS - 1, 0, 2 * NA_WIN_COLS - 2)
    bias = rpb.astype(F32)[:, dr][..., dc]
    ok = row_ok[None, :, :, :, None, None] & col_ok[None, None, None, None, :, :]
    bias = jnp.where(ok, bias, NEG_INF).transpose(0, 1, 2, 4, 3, 5)
    return bias.reshape(rpb.shape[0], 3, NA_QROWS * GRID_W, NA_KROWS * GRID_W)


def _softmax_pv(scores, values, sink=None):
    m = functools.reduce(jnp.maximum, [jnp.max(s, axis=-1, keepdims=True) for s in scores])
    if sink is not None:
        m = jnp.maximum(m, sink)
    ps = [jnp.exp(s - m) for s in scores]
    denom = functools.reduce(lambda a, b: a + b, [jnp.sum(p, axis=-1, keepdims=True) for p in ps])
    if sink is not None:
        denom = denom + jnp.exp(sink - m)
    o = functools.reduce(lambda a, b: a + b, [_dot(p.astype(BF16), v) for p, v in zip(ps, values)])
    return o / denom


def _na_body(rows, q_ref, k_ref, v_ref, qc_ref, kc_ref, vc_ref, bias_ref, o_ref, oc_ref):
    hd = NA_HD
    n_q, n_k = NA_QROWS * GRID_W, NA_KROWS * GRID_W
    n_blk = rows // NA_QROWS
    kc = kc_ref[...]
    vc = vc_ref[...]

    def blk_body(blk, carry):
        pattern = jnp.where(blk == 0, 0, jnp.where(blk == n_blk - 1, 2, 1))
        qs = pl.multiple_of(blk * n_q, n_q)
        ks = pl.multiple_of(_na_key_start(blk, rows) * GRID_W, GRID_W)
        q2 = q_ref[pl.ds(qs, n_q), :]
        k2 = k_ref[pl.ds(ks, n_k), :]
        v2 = v_ref[pl.ds(ks, n_k), :]
        outs = []
        for h in range(2):
            sl = slice(h * hd, (h + 1) * hd)
            q = q2[:, sl]
            s_loc = _dot_nt(q, k2[:, sl]) + bias_ref[h, pattern]
            s_ctx = _dot_nt(q, kc[:, sl])
            outs.append(_softmax_pv([s_loc, s_ctx], [v2[:, sl], vc[:, sl]]))
        o_ref[pl.ds(qs, n_q), :] = jnp.concatenate(outs, axis=1).astype(o_ref.dtype)
        return carry

    lax.fori_loop(0, n_blk, blk_body, 0)
    outs = []
    for h in range(2):
        sl = slice(h * hd, (h + 1) * hd)
        outs.append(_softmax_pv([_dot_nt(qc_ref[:, sl], kc[:, sl])], [vc[:, sl]]))
    oc_ref[...] = jnp.concatenate(outs, axis=1).astype(oc_ref.dtype)


def na_attention(att, bias, n_batch, seq, ctx_len):
    n_pair = NA_HEADS // 2
    ctx_blk0 = n_batch * seq // ctx_len
    lat = lambda off: pl.BlockSpec((seq, LANES), lambda b, j: (b, off + j))
    ctx = lambda off: pl.BlockSpec((ctx_len, LANES), lambda b, j: (ctx_blk0 + b, off + j))
    return pl.pallas_call(
        functools.partial(_na_body, seq // GRID_W),
        out_shape=(jax.ShapeDtypeStruct((n_batch * seq, NA_HEADS * NA_HD), BF16),
                   jax.ShapeDtypeStruct((n_batch * ctx_len, NA_HEADS * NA_HD), BF16)),
        grid=(n_batch, n_pair),
        in_specs=[lat(0), lat(n_pair), lat(2 * n_pair), ctx(0), ctx(n_pair), ctx(2 * n_pair),
                  pl.BlockSpec((2,) + bias.shape[1:], lambda b, j: (j, 0, 0, 0))],
        out_specs=(pl.BlockSpec((seq, LANES), lambda b, j: (b, j)), pl.BlockSpec((ctx_len, LANES), lambda b, j: (b, j))),
        compiler_params=_params("parallel", "arbitrary"),
        name="na_attention",
    )(att, att, att, att, att, att, bias)


def _wa_body(n_blk, with_ctx, sink_ref, q_ref, k_ref, v_ref, qc_ref, kc_ref, vc_ref, o_ref, oc_ref):
    hd, blk, grp = WA_HD, WINDOW, WA_GROUP
    rows = lax.broadcasted_iota(jnp.int32, (grp * blk, blk), 0) % blk
    cols = lax.broadcasted_iota(jnp.int32, (grp * blk, blk), 1)
    for kv in range(WA_KV_HEADS):
        ksl = slice(kv * hd, (kv + 1) * hd)
        kc = kc_ref[:, ksl]
        vc = vc_ref[:, ksl]

        def stack(ref, start, size):
            return jnp.concatenate([ref[pl.ds(start, size), (kv * grp + g) * hd:(kv * grp + g + 1) * hd]
                                    for g in range(grp)], axis=0)

        def unstack(o, size):
            return jnp.concatenate([o[g * size:(g + 1) * size] for g in range(grp)], axis=1)

        def sink_col(size):
            return jnp.concatenate([jnp.full((size, 1), sink_ref[kv * grp + g], F32) for g in range(grp)], axis=0)

        sink_q = sink_col(blk)

        def blk_body(n, carry):
            qs = pl.multiple_of(n * blk, blk)
            ps = pl.multiple_of(jnp.maximum(n - 1, 0) * blk, blk)
            ns = pl.multiple_of(jnp.minimum(n + 1, n_blk - 1) * blk, blk)
            q = stack(q_ref, qs, blk)
            s_prev = jnp.where((cols >= rows) & (n > 0), _dot_nt(q, k_ref[pl.ds(ps, blk), ksl]), NEG_INF)
            s_self = _dot_nt(q, k_ref[pl.ds(qs, blk), ksl])
            s_next = jnp.where((cols <= rows) & (n < n_blk - 1), _dot_nt(q, k_ref[pl.ds(ns, blk), ksl]), NEG_INF)
            s_ctx = _dot_nt(q, kc)
            o = _softmax_pv([s_prev, s_self, s_next, s_ctx],
                            [v_ref[pl.ds(ps, blk), ksl], v_ref[pl.ds(qs, blk), ksl], v_ref[pl.ds(ns, blk), ksl], vc],
                            sink=sink_q)
            o_ref[pl.ds(qs, blk), kv * grp * hd:(kv + 1) * grp * hd] = unstack(o, blk).astype(o_ref.dtype)
            return carry

        lax.fori_loop(0, n_blk, blk_body, 0)
        n_ctx = qc_ref.shape[0]
        if with_ctx:
            qc = stack(qc_ref, 0, n_ctx)
            oc = _softmax_pv([_dot_nt(qc, kc)], [vc], sink=sink_col(n_ctx))
            oc_ref[:, kv * grp * hd:(kv + 1) * grp * hd] = unstack(oc, n_ctx).astype(oc_ref.dtype)
        else:
            oc_ref[:, kv * grp * hd:(kv + 1) * grp * hd] = jnp.zeros((n_ctx, grp * hd), oc_ref.dtype)


def window_attention(qkv, sink, n_batch, seq, ctx_len, with_ctx):
    width = WA_HEADS * WA_HD
    ctx_blk0 = n_batch * seq // ctx_len
    kcol, vcol = width // LANES, width // LANES + 1
    return pl.pallas_call(
        functools.partial(_wa_body, seq // WINDOW, with_ctx),
        out_shape=(jax.ShapeDtypeStruct((n_batch * seq, width), BF16), jax.ShapeDtypeStruct((n_batch * ctx_len, width), BF16)),
        grid=(n_batch,),
        in_specs=[pl.BlockSpec(memory_space=pltpu.SMEM),
                  pl.BlockSpec((seq, width), lambda b: (b, 0)),
                  pl.BlockSpec((seq, LANES), lambda b: (b, kcol)),
                  pl.BlockSpec((seq, LANES), lambda b: (b, vcol)),
                  pl.BlockSpec((ctx_len, width), lambda b: (ctx_blk0 + b, 0)),
                  pl.BlockSpec((ctx_len, LANES), lambda b: (ctx_blk0 + b, kcol)),
                  pl.BlockSpec((ctx_len, LANES), lambda b: (ctx_blk0 + b, vcol))],
        out_specs=(pl.BlockSpec((seq, width), lambda b: (b, 0)), pl.BlockSpec((ctx_len, width), lambda b: (b, 0))),
        compiler_params=_params("parallel"),
        name="window_attention",
    )(sink, qkv, qkv, qkv, qkv, qkv, qkv)


CONV_ROWS = 256
CONV_HALO = 8


def _conv_silu_cols(src_ref, src_cols, w_ref, w_cols, pad_ref, dst_ref, dst_row0, dst_cols):
    n_rows = src_ref.shape[0]
    n_blk = n_rows // CONV_ROWS
    zeros = jnp.zeros((CONV_HALO, LANES), F32)
    pad_ref[0:CONV_HALO, :] = zeros
    pad_ref[CONV_HALO + n_rows:2 * CONV_HALO + n_rows, :] = zeros

    def copy(i, carry):
        r0 = pl.multiple_of(i * CONV_ROWS, CONV_ROWS)
        pad_ref[pl.ds(CONV_HALO + r0, CONV_ROWS), :] = src_ref[pl.ds(r0, CONV_ROWS), src_cols]
        return carry

    lax.fori_loop(0, n_blk, copy, 0)
    first = CONV_HALO - CONV_K // 2

    def conv(i, carry):
        r0 = pl.multiple_of(i * CONV_ROWS, CONV_ROWS)
        win = pad_ref[pl.ds(r0, CONV_ROWS + 2 * CONV_HALO), :]
        acc = win[first:first + CONV_ROWS] * w_ref[0:1, w_cols]
        for j in range(1, CONV_K):
            acc = acc + win[first + j:first + j + CONV_ROWS] * w_ref[j:j + 1, w_cols]
        dst_ref[pl.ds(dst_row0 + r0, CONV_ROWS), dst_cols] = _silu(acc)
        return carry

    lax.fori_loop(0, n_blk, conv, 0)


def _tri(upper):
    r = lax.broadcasted_iota(jnp.int32, (CHUNK, CHUNK), 0)
    c = lax.broadcasted_iota(jnp.int32, (CHUNK, CHUNK), 1)
    return (r <= c) if upper else (r >= c)


def _rows_as_lanes(a):
    return jnp.concatenate([a, jnp.zeros((LANES - CHUNK, LANES), F32)], axis=0).T[:, 0:CHUNK]


def _softplus(x):
    return jnp.maximum(x, 0.0) + jnp.log1p(jnp.exp(-jnp.abs(x)))


NEUMANN_STEPS = 5
DN_GROUP = 4

def _dn_body(q_ref, k_ref, v_ref, g_ref, qc_ref, kc_ref, vc_ref, gc_ref, wq_ref, wk_ref, wv_ref, alv_ref, dtv_ref,
             o_ref, oc_ref, pad_ref, qs_ref, ks_ref, vs_ref, gb_ref, osum_ref, qe_ref, kb_ref, dec_ref):
    n_ctx, n_lat = qc_ref.shape[0], q_ref.shape[0]
    n_rows = n_ctx + n_lat
    head = pl.program_id(1)
    full = slice(0, LANES)
    for src_c, src, w_ref, dst in ((qc_ref, q_ref, wq_ref, qs_ref), (kc_ref, k_ref, wk_ref, ks_ref), (vc_ref, v_ref, wv_ref, vs_ref)):
        _conv_silu_cols(src_c, full, w_ref, full, pad_ref, dst, 0, full)
        _conv_silu_cols(src, full, w_ref, full, pad_ref, dst, n_ctx, full)

    def l2(i, carry):
        r = pl.multiple_of(i * CONV_ROWS, CONV_ROWS)
        q = qs_ref[pl.ds(r, CONV_ROWS), :]
        qs_ref[pl.ds(r, CONV_ROWS), :] = q * lax.rsqrt(jnp.sum(q * q, axis=-1, keepdims=True) + EPS) * (DN_DK ** -0.5)
        k = ks_ref[pl.ds(r, CONV_ROWS), :]
        ks_ref[pl.ds(r, CONV_ROWS), :] = k * lax.rsqrt(jnp.sum(k * k, axis=-1, keepdims=True) + EPS)
        return carry

    lax.fori_loop(0, n_rows // CONV_ROWS, l2, 0)

    def gates(raw):
        lane = lax.broadcasted_iota(jnp.int32, raw.shape, 1)
        return jnp.where(lane < 2 * DN_HEADS, jax.nn.sigmoid(raw), alv_ref[...] * _softplus(raw + dtv_ref[...]))

    gb_ref[0:n_ctx, :] = gates(gc_ref[...])
    gb_ref[n_ctx:, :] = gates(g_ref[...])
    osum_ref[...] = jnp.zeros(osum_ref.shape, F32)
    grp = DN_GROUP * CHUNK
    lane = lax.broadcasted_iota(jnp.int32, (grp, LANES), 1)
    row_i = lax.broadcasted_iota(jnp.int32, (grp, grp), 0)
    col_i = lax.broadcasted_iota(jnp.int32, (grp, grp), 1)
    same_chunk = (row_i // CHUNK) == (col_i // CHUNK)

    def group(gi, carry):
        r = pl.multiple_of(gi * grp, grp)
        q, k, v, gb = qs_ref[pl.ds(r, grp), :], ks_ref[pl.ds(r, grp), :], vs_ref[pl.ds(r, grp), :], gb_ref[pl.ds(r, grp), :]
        k16 = k.astype(BF16)
        q16 = q.astype(BF16)
        for d in range(2):
            keep = same_chunk & ((row_i <= col_i) if d == 1 else (row_i >= col_i))
            strict = keep & (row_i != col_i)
            beta = jnp.sum(jnp.where(lane == d * DN_HEADS + head, gb, 0.0), axis=-1, keepdims=True)
            g = jnp.broadcast_to(jnp.sum(jnp.where(lane == (2 + d) * DN_HEADS + head, gb, 0.0), axis=-1, keepdims=True), (grp, LANES))
            g_hi = g.astype(BF16)
            g_r = g - g_hi.astype(F32)
            g_mid = g_r.astype(BF16)
            g_lo = (g_r - g_mid.astype(F32)).astype(BF16)
            ones = keep.astype(BF16)
            cum = _dot(ones, g_hi) + _dot(ones, g_mid) + _dot(ones, g_lo)
            cum_row = cum.T[0:1, :]
            decay = jnp.where(keep, jnp.exp(jnp.concatenate([cum] * (grp // LANES), axis=1) - cum_row), 0.0)
            kb = k * beta
            a = jnp.where(strict, _dot_nt(kb.astype(BF16), k16) * decay, 0.0)
            attn16 = (_dot_nt(q16, k16) * decay).astype(BF16)
            e_cum = jnp.exp(cum)
            rhs = jnp.concatenate([v * beta, kb * e_cum], axis=1)
            n = -a
            a16 = a.astype(BF16)
            p = _dot(a16, a16)
            for it in range(NEUMANN_STEPS):
                p16 = p.astype(BF16)
                n = n + p + _dot(n.astype(BF16), p16)
                if it + 1 < NEUMANN_STEPS:
                    p = _dot(p16, p16)
            sol16 = (rhs + _dot(n.astype(BF16), rhs.astype(BF16))).astype(BF16)
            aw = _dot(attn16, sol16)
            osum_ref[pl.ds(r, grp), :] += aw[:, 0:DN_DK]
            qe_ref[d, pl.ds(r, grp), :] = q * e_cum - aw[:, DN_DK:]
            for c in range(DN_GROUP):
                rows = slice(c * CHUNK, (c + 1) * CHUNK)
                edge = cum[c * CHUNK + CHUNK - 1:(c + 1) * CHUNK, :] if d == 0 else cum[c * CHUNK:c * CHUNK + 1, :]
                kdec16 = (k[rows] * jnp.exp(edge - cum[rows])).astype(BF16)
                kb_ref[d, gi * DN_GROUP + c] = _dot_tn(kdec16, sol16[rows])
                dec_ref[d, gi * DN_GROUP + c] = jnp.broadcast_to(jnp.exp(edge), (8, LANES))
        return carry

    n_chunks = n_rows // CHUNK
    n_ctx_chunks = n_ctx // CHUNK
    lax.fori_loop(0, n_chunks // DN_GROUP, group, 0)

    def step(i, states):
        new = []
        for d in range(2):
            c = i if d == 0 else jnp.where(i < n_ctx_chunks, n_ctx_chunks - 1 - i, n_chunks + n_ctx_chunks - 1 - i)
            r = pl.multiple_of(c * CHUNK, CHUNK)
            s16 = states[d].astype(BF16)
            osum_ref[pl.ds(r, CHUNK), :] += _dot(qe_ref[d, pl.ds(r, CHUNK), :].astype(BF16), s16)
            bk = kb_ref[d, c]
            new.append(states[d] * dec_ref[d, c][0:1, :] - _dot(bk[:, DN_DK:].astype(BF16), s16) + bk[:, 0:DN_DK])
        return tuple(new)

    zero = jnp.zeros((DN_DK, DN_DK), F32)
    lax.fori_loop(0, n_chunks, step, (zero, zero))
    oc_ref[...] = osum_ref[0:n_ctx, :]
    o_ref[...] = osum_ref[n_ctx:, :]


def dn_gate_vectors(a_log, dt_bias):
    def vec(v):
        return jnp.pad(v.astype(F32).reshape(1, -1), ((0, 0), (2 * DN_HEADS, LANES - 4 * DN_HEADS)))
    return vec(-jnp.exp(a_log.astype(F32))), vec(dt_bias)


def deltanet(dqkv, dg, conv_w, alv, dtv, n_batch, seq, ctx_len):
    nh = DN_HEADS
    ctx_blk0 = n_batch * seq // ctx_len
    n_rows = seq + ctx_len
    lat = lambda off: pl.BlockSpec((seq, LANES), lambda b, h: (b, off + h))
    ctx = lambda off: pl.BlockSpec((ctx_len, LANES), lambda b, h: (ctx_blk0 + b, off + h))
    cw = lambda off: pl.BlockSpec((CONV_K, LANES), lambda b, h: (0, off + h))
    vec = pl.BlockSpec((1, LANES), lambda b, h: (0, 0))
    seq_buf = pltpu.VMEM((n_rows, LANES), F32)
    return pl.pallas_call(
        _dn_body,
        out_shape=(jax.ShapeDtypeStruct((n_batch * seq, nh * DN_DK), F32), jax.ShapeDtypeStruct((n_batch * ctx_len, nh * DN_DK), F32)),
        grid=(n_batch, nh),
        in_specs=[lat(0), lat(nh), lat(2 * nh), pl.BlockSpec((seq, LANES), lambda b, h: (b, 0)),
                  ctx(0), ctx(nh), ctx(2 * nh), pl.BlockSpec((ctx_len, LANES), lambda b, h: (ctx_blk0 + b, 0)),
                  cw(0), cw(nh), cw(2 * nh), vec, vec],
        out_specs=(pl.BlockSpec((seq, LANES), lambda b, h: (b, h)), pl.BlockSpec((ctx_len, LANES), lambda b, h: (b, h))),
        scratch_shapes=[pltpu.VMEM((seq + 2 * CONV_HALO, LANES), F32), seq_buf, seq_buf, seq_buf, seq_buf, seq_buf,
                        pltpu.VMEM((2, n_rows, LANES), F32),
                        pltpu.VMEM((2, n_rows // CHUNK, DN_DK, 2 * DN_DK), F32),
                        pltpu.VMEM((2, n_rows // CHUNK, 8, LANES), F32)],
        compiler_params=_params("parallel", "arbitrary"),
        name="deltanet",
    )(dqkv, dqkv, dqkv, dg, dqkv, dqkv, dqkv, dg, conv_w, conv_w, conv_w, alv, dtv)


def _ssd_body(xbc_ref, dt_ref, xbc_c_ref, dt_c_ref, cw_ref, av_ref, dtb_ref, dsk_ref, y_ref, yc_ref,
              pad_ref, s_ref, dts_ref, da_ref, hs_ref):
    n_ctx, n_lat = xbc_c_ref.shape[0], xbc_ref.shape[0]
    inner = SSD_HEADS * SSD_HD
    hpg = SSD_HEADS // SSD_GROUPS
    gw = hpg * SSD_HD
    for cb in range(xbc_ref.shape[1] // LANES):
        cols = slice(cb * LANES, (cb + 1) * LANES)
        _conv_silu_cols(xbc_c_ref, cols, cw_ref, cols, pad_ref, s_ref, 0, cols)
        _conv_silu_cols(xbc_ref, cols, cw_ref, cols, pad_ref, s_ref, n_ctx, cols)
    dts_ref[0:n_ctx, :] = _softplus(dt_c_ref[...] + dtb_ref[...])
    dts_ref[n_ctx:, :] = _softplus(dt_ref[...] + dtb_ref[...])
    da_ref[...] = dts_ref[...] * av_ref[...]
    yc_ref[...] = s_ref[0:n_ctx, 0:inner] * dsk_ref[...]
    y_ref[...] = s_ref[n_ctx:, 0:inner] * dsk_ref[...]
    hs_ref[...] = jnp.zeros(hs_ref.shape, F32)

    def chunk(d, c, row0, out_ref):
        r = pl.multiple_of(row0 + c * CHUNK, CHUNK)
        ro = pl.multiple_of(c * CHUNK, CHUNK)
        keep = _tri(upper=(d == 1))
        acum = jnp.dot(keep.astype(F32), da_ref[pl.ds(r, CHUNK), :], precision=HIGHEST, preferred_element_type=F32)
        acum_t = _rows_as_lanes(acum)
        edge = acum[CHUNK - 1:CHUNK, :] if d == 0 else acum[0:1, :]
        dt = dts_ref[pl.ds(r, CHUNK), :]
        for g in range(SSD_GROUPS):
            lanes = [d * SSD_HEADS + g * hpg + h for h in range(hpg)]

            def expand(v, rows):
                return jnp.concatenate([jnp.broadcast_to(v[:, l:l + 1], (rows, SSD_HD)) for l in lanes], axis=1)

            x4 = s_ref[pl.ds(r, CHUNK), g * gw:(g + 1) * gw]
            bm = s_ref[pl.ds(r, CHUNK), inner + g * SSD_STATE:inner + (g + 1) * SSD_STATE].astype(BF16)
            cm = s_ref[pl.ds(r, CHUNK), inner + (SSD_GROUPS + g) * SSD_STATE:inner + (SSD_GROUPS + g + 1) * SSD_STATE].astype(BF16)
            cb_mat = _dot_nt(cm, bm)
            a4 = expand(acum, CHUNK)
            e4 = expand(edge, 1)
            xd = x4 * expand(dt, CHUNK)
            prev = hs_ref[d, g]
            y = _dot_nt(cm, prev.astype(BF16)) * jnp.exp(a4)
            diag = []
            for h, l in enumerate(lanes):
                lmat = jnp.where(keep, jnp.exp(acum[:, l:l + 1] - acum_t[l:l + 1, :]), 0.0)
                diag.append(_dot((cb_mat * lmat).astype(BF16), xd[:, h * SSD_HD:(h + 1) * SSD_HD].astype(BF16)))
            y = y + jnp.concatenate(diag, axis=1)
            out_ref[pl.ds(ro, CHUNK), g * gw:(g + 1) * gw] += y
            states = _dot_tn((xd * jnp.exp(e4 - a4)).astype(BF16), bm)
            decay = jnp.concatenate([jnp.broadcast_to(jnp.exp(edge[:, l:l + 1]), (SSD_HD, SSD_STATE)) for l in lanes], axis=0)
            hs_ref[d, g] = prev * decay + states

    def segment(n_chunks, row0, out_ref):
        def body(i, carry):
            chunk(0, i, row0, out_ref)
            chunk(1, n_chunks - 1 - i, row0, out_ref)
            return carry
        lax.fori_loop(0, n_chunks, body, 0)

    segment(n_ctx // CHUNK, 0, yc_ref)
    segment(n_lat // CHUNK, n_ctx, y_ref)


def ssd(xbc, dt, conv_w, a_vec, dtb_vec, skip_vec, n_batch, seq, ctx_len):
    width = xbc.shape[1]
    inner = SSD_HEADS * SSD_HD
    ctx_blk0 = n_batch * seq // ctx_len
    n_rows = seq + ctx_len
    return pl.pallas_call(
        _ssd_body,
        out_shape=(jax.ShapeDtypeStruct((n_batch * seq, inner), F32), jax.ShapeDtypeStruct((n_batch * ctx_len, inner), F32)),
        grid=(n_batch,),
        in_specs=[pl.BlockSpec((seq, width), lambda b: (b, 0)), pl.BlockSpec((seq, LANES), lambda b: (b, 0)),
                  pl.BlockSpec((ctx_len, width), lambda b: (ctx_blk0 + b, 0)), pl.BlockSpec((ctx_len, LANES), lambda b: (ctx_blk0 + b, 0)),
                  _resident(conv_w.shape), _resident(a_vec.shape), _resident(dtb_vec.shape), _resident(skip_vec.shape)],
        out_specs=(pl.BlockSpec((seq, inner), lambda b: (b, 0)), pl.BlockSpec((ctx_len, inner), lambda b: (b, 0))),
        scratch_shapes=[pltpu.VMEM((seq + 2 * CONV_HALO, LANES), F32), pltpu.VMEM((n_rows, width), F32),
                        pltpu.VMEM((n_rows, LANES), F32), pltpu.VMEM((n_rows, LANES), F32),
                        pltpu.VMEM((2, SSD_GROUPS, (SSD_HEADS // SSD_GROUPS) * SSD_HD, SSD_STATE), F32)],
        compiler_params=_params("parallel"),
        name="ssd",
    )(xbc, dt, xbc, dt, conv_w, a_vec, dtb_vec, skip_vec)


def ssd_vectors(a_log, dt_bias, d_skip):
    def vec(v):
        return jnp.pad(v.astype(F32).reshape(1, -1), ((0, 0), (0, LANES - 2 * SSD_HEADS)))
    return vec(-jnp.exp(a_log.astype(F32))), vec(dt_bias), jnp.repeat(d_skip.astype(F32), SSD_HD)[None, :]


def kernel(x, c, ctx, c_ctx, ada_w, ada_b, norm_w, ev_w_in, ev_w_out, na_rpb, dn_conv, dn_a_log, dn_dt_bias, dn_norm, ffn_w1, ffn_w3, ffn_w2, od_w_in, od_w_out, wa_sink, ssd_conv, ssd_a_log, ssd_dt_bias, ssd_d, ssd_norm, moe_router, moe_w1, moe_w3, moe_w2):
    n_batch, seq, d = x.shape
    ctx_len = ctx.shape[1]
    depth = ada_w.shape[0]
    n_lat, n_ctx = n_batch * seq, n_batch * ctx_len
    n_cond = -(-(n_batch + 1) // 8) * 8
    cond = jnp.concatenate([c, c_ctx[None, :], jnp.zeros((n_cond - n_batch - 1, d), F32)], axis=0)
    mod_all = ada_modulation(cond, ada_w, ada_b)
    x_all = jnp.concatenate([x.reshape(n_lat, d), ctx.reshape(n_ctx, d)], axis=0)
    cos_t, sin_t = rope_tables(seq)
    for layer in range(depth):
        i = layer // 2
        n_tok = n_lat + n_ctx if layer < depth - 1 else n_lat
        mod = mod_all[layer, :n_batch + 1].reshape(n_batch + 1, 6, d)
        nw = norm_w[layer]
        if layer % 2 == 0:
            att, dqkv, dz, dg = proj_in_even(x_all, mod, nw, prep_w_in_even(ev_w_in[i]), n_batch, seq)
            o_a, oc_a = na_attention(att, na_bias_tables(na_rpb[i], seq // GRID_W), n_batch, seq, ctx_len)
            alv, dtv = dn_gate_vectors(dn_a_log[i], dn_dt_bias[i])
            o_b, oc_b = deltanet(dqkv, dg, dn_conv[i], alv, dtv, n_batch, seq, ctx_len)
            x_all = proj_out(jnp.concatenate([o_a, oc_a], axis=0), jnp.concatenate([o_b, oc_b], axis=0), dz,
                             jnp.tile(dn_norm[i], DN_HEADS)[None, :], ev_w_out[i].astype(BF16), x_all, mod, nw,
                             n_batch, seq, n_tok, DN_DK, False)
            x_all = ffn_dense(x_all, mod, nw, ffn_w1[i].astype(BF16), ffn_w3[i].astype(BF16), ffn_w2[i].astype(BF16),
                              n_batch, seq, n_tok)
        else:
            qkv, z, xbc, dt = proj_in_odd(x_all, mod, nw, prep_w_in_odd(od_w_in[i]), cos_t, sin_t, n_batch, seq)
            o_c, oc_c = window_attention(qkv, wa_sink[i], n_batch, seq, ctx_len, n_tok > n_lat)
            a_vec, dtb_vec, skip_vec = ssd_vectors(ssd_a_log[i], ssd_dt_bias[i], ssd_d[i])
            y_d, yc_d = ssd(xbc, dt, ssd_conv[i], a_vec, dtb_vec, skip_vec, n_batch, seq, ctx_len)
            x_all = proj_out(jnp.concatenate([o_c, oc_c], axis=0), jnp.concatenate([y_d, yc_d], axis=0), z,
                             ssd_norm[i][None, :], od_w_out[i].astype(BF16), x_all, mod, nw,
                             n_batch, seq, n_tok, SSD_HEADS * SSD_HD // SSD_GROUPS, True)
            router = jnp.pad(moe_router[i], ((0, 0), (0, LANES - N_EXPERTS)))
            x_all = moe_layer(x_all, mod, nw, router, moe_w1[i].astype(BF16), moe_w3[i].astype(BF16),
                              moe_w2[i].astype(BF16), n_batch, seq, n_tok)
    return x_all[:n_lat].reshape(n_batch, seq, d)
```

```python
import functools
import math

import jax
import jax.numpy as jnp
import numpy as np
from jax import lax
from jax.experimental import pallas as pl
from jax.experimental.pallas import tpu as pltpu

F32 = jnp.float32
BF16 = jnp.bfloat16
HIGHEST = lax.Precision.HIGHEST

EPS = 1e-6
NEG_INF = -1e30
GRID_W = 64
CHUNK = 64
CONV_K = 5
ROPE_BASE = 10000.0
NA_HEADS, NA_HD, NA_WIN_ROWS, NA_WIN_COLS = 8, 64, 8, 16
DN_HEADS, DN_DK = 4, 128
WA_HEADS, WA_KV_HEADS, WA_HD, WINDOW = 8, 2, 64, 128
WA_GROUP = WA_HEADS // WA_KV_HEADS
SSD_HEADS, SSD_HD, SSD_GROUPS, SSD_STATE = 8, 64, 2, 128
N_EXPERTS, TOP_K = 8, 2

LANES = 128
VMEM_LIMIT = 56 * 1024 * 1024
TOK_TILE = 512


def _params(*sem):
    return pltpu.CompilerParams(dimension_semantics=sem, vmem_limit_bytes=VMEM_LIMIT)


def _resident(shape):
    nd = len(shape)
    return pl.BlockSpec(shape, lambda *_: (0,) * nd, pipeline_mode=pl.Buffered(1))


def _silu(x):
    return x * jax.nn.sigmoid(x)


def _rms(x, w):
    return x * lax.rsqrt(jnp.mean(x * x, axis=-1, keepdims=True) + EPS) * w


def _dot(a, b):
    return jnp.dot(a, b, preferred_element_type=F32)


def _dot_nt(a, b):
    return lax.dot_general(a, b, (((1,), (1,)), ((), ())), preferred_element_type=F32)


def _dot_tn(a, b):
    return lax.dot_general(a, b, (((0,), (0,)), ((), ())), preferred_element_type=F32)


def _ada_body(s_ref, w_ref, b_ref, o_ref):
    s = _silu(s_ref[...])
    o_ref[0] = jnp.dot(s, w_ref[0], precision=HIGHEST, preferred_element_type=F32) + b_ref[0]


def ada_modulation(cond, ada_w, ada_b):
    depth, d, n = ada_w.shape
    r = cond.shape[0]
    tn = 1536
    return pl.pallas_call(
        _ada_body,
        out_shape=jax.ShapeDtypeStruct((depth, r, n), F32),
        grid=(depth, n // tn),
        in_specs=[
            pl.BlockSpec((r, d), lambda l, j: (0, 0)),
            pl.BlockSpec((1, d, tn), lambda l, j: (l, 0, j)),
            pl.BlockSpec((1, 1, tn), lambda l, j: (l, 0, j)),
        ],
        out_specs=pl.BlockSpec((1, r, tn), lambda l, j: (l, 0, j)),
        compiler_params=_params("arbitrary", "arbitrary"),
        name="ada_modulation",
    )(cond, ada_w, ada_b.reshape(depth, 1, n))


def _mod_rows(n_lat_tiles, tiles_per_batch, n_batch):
    def index(i):
        return (jnp.minimum(i // tiles_per_batch, n_batch), 0, 0)
    del n_lat_tiles
    return index


def _norm_mod(x, mod_ref, nw_ref, row):
    y = _rms(x, nw_ref[row:row + 1, :])
    return y * (1.0 + mod_ref[0, row + 1:row + 2, :]) + mod_ref[0, row:row + 1, :]


def _proj_even_body(x_ref, mod_ref, nw_ref, w_ref, att_ref, dqkv_ref, dz_ref, dg_ref):
    hb = _norm_mod(x_ref[...], mod_ref, nw_ref, 0).astype(BF16)
    att_ref[...] = _dot(hb, w_ref[:, 0:1536]).astype(BF16)
    dqkv_ref[...] = _dot(hb, w_ref[:, 1536:3072])
    dz_ref[...] = _dot(hb, w_ref[:, 3072:3584])
    dg_ref[...] = _dot(hb, w_ref[:, 3584:3712])


def _swap16(x):
    lane = lax.broadcasted_iota(jnp.int32, x.shape, 1)
    up = pltpu.roll(x, LANES - 16, 1)
    down = pltpu.roll(x, 16, 1)
    return jnp.where((lane % 32) < 16, up, down)


def _proj_odd_body(x_ref, mod_ref, nw_ref, w_ref, cos_ref, sin_ref, qkv_ref, z_ref, xbc_ref, dt_ref):
    hb = _norm_mod(x_ref[...], mod_ref, nw_ref, 0).astype(BF16)
    cos, sin = cos_ref[...], sin_ref[...]
    for c in range(5):
        t = _dot(hb, w_ref[:, c * LANES:(c + 1) * LANES])
        qkv_ref[:, c * LANES:(c + 1) * LANES] = (t * cos + _swap16(t) * sin).astype(BF16)
    qkv_ref[:, 640:768] = _dot(hb, w_ref[:, 640:768]).astype(BF16)
    z_ref[...] = _dot(hb, w_ref[:, 768:1280])
    xbc_ref[...] = _dot(hb, w_ref[:, 1280:2304])
    dt_ref[...] = _dot(hb, w_ref[:, 2304:2432])


def _pad_cols(w, n):
    return jnp.pad(w, ((0, 0), (0, n - w.shape[1])))


def prep_w_in_even(w):
    q_scale = jnp.concatenate([jnp.full((NA_HEADS * NA_HD,), NA_HD ** -0.5, F32), jnp.ones((w.shape[1] - NA_HEADS * NA_HD,), F32)])
    return _pad_cols(w * q_scale[None, :], 3712).astype(BF16)


def prep_w_in_odd(w):
    q_scale = jnp.concatenate([jnp.full((WA_HEADS * WA_HD,), WA_HD ** -0.5, F32), jnp.ones((w.shape[1] - WA_HEADS * WA_HD,), F32)])
    return _pad_cols(w * q_scale[None, :], 2432).astype(BF16)


def _tok_spec(width, tm=TOK_TILE):
    return pl.BlockSpec((tm, width), lambda i: (i, 0))


def proj_in_even(x_all, mod, nw, w, n_batch, seq):
    t_all, d = x_all.shape
    tm = TOK_TILE
    tpb = seq // tm
    return pl.pallas_call(
        _proj_even_body,
        out_shape=(jax.ShapeDtypeStruct((t_all, 1536), BF16), jax.ShapeDtypeStruct((t_all, 1536), F32),
                   jax.ShapeDtypeStruct((t_all, 512), F32), jax.ShapeDtypeStruct((t_all, LANES), F32)),
        grid=(t_all // tm,),
        in_specs=[_tok_spec(d), pl.BlockSpec((1, 6, d), _mod_rows(None, tpb, n_batch)),
                  _resident(nw.shape), _resident(w.shape)],
        out_specs=(_tok_spec(1536), _tok_spec(1536), _tok_spec(512), _tok_spec(LANES)),
        compiler_params=_params("parallel"),
        name="proj_in_even",
    )(x_all, mod, nw, w)


def proj_in_odd(x_all, mod, nw, w, cos_t, sin_t, n_batch, seq):
    t_all, d = x_all.shape
    tm = TOK_TILE
    tpb = seq // tm

    def rope_rows(i):
        return (jnp.where(i < n_batch * tpb, i % tpb, tpb), 0)

    return pl.pallas_call(
        _proj_odd_body,
        out_shape=(jax.ShapeDtypeStruct((t_all, 768), BF16), jax.ShapeDtypeStruct((t_all, 512), F32),
                   jax.ShapeDtypeStruct((t_all, 1024), F32), jax.ShapeDtypeStruct((t_all, LANES), F32)),
        grid=(t_all // tm,),
        in_specs=[_tok_spec(d), pl.BlockSpec((1, 6, d), _mod_rows(None, tpb, n_batch)),
                  _resident(nw.shape), _resident(w.shape),
                  pl.BlockSpec((tm, LANES), rope_rows), pl.BlockSpec((tm, LANES), rope_rows)],
        out_specs=(_tok_spec(768), _tok_spec(512), _tok_spec(1024), _tok_spec(LANES)),
        compiler_params=_params("parallel"),
        name="proj_in_odd",
    )(x_all, mod, nw, w, cos_t, sin_t)


def rope_tables(seq, tm=TOK_TILE):
    half = WA_HD // 4
    freqs = ROPE_BASE ** (-jnp.arange(half, dtype=F32) / half)
    t = jnp.arange(seq)
    ang_r = (t // GRID_W).astype(F32)[:, None] * freqs[None, :]
    ang_c = (t % GRID_W).astype(F32)[:, None] * freqs[None, :]
    cos = jnp.concatenate([jnp.cos(ang_r)] * 2 + [jnp.cos(ang_c)] * 2, axis=-1)
    sin = jnp.concatenate([-jnp.sin(ang_r), jnp.sin(ang_r), -jnp.sin(ang_c), jnp.sin(ang_c)], axis=-1)
    cos = jnp.concatenate([jnp.tile(cos, (1, 2)), jnp.ones((tm, LANES), F32)], axis=0)
    sin = jnp.concatenate([jnp.tile(sin, (1, 2)), jnp.zeros((tm, LANES), F32)], axis=0)
    return cos, sin


def _proj_out_body(group, gate_first, a_ref, r_ref, z_ref, gw_ref, w_ref, x_ref, mod_ref, nw_ref, o_ref):
    r = r_ref[...]
    gate = _silu(z_ref[...])
    if gate_first:
        r = r * gate
    parts = []
    for c in range(r.shape[1] // group):
        rc = r[:, c * group:(c + 1) * group]
        parts.append(_rms(rc, gw_ref[:, c * group:(c + 1) * group]))
    r = jnp.concatenate(parts, axis=1)
    if not gate_first:
        r = r * gate
    half = a_ref.shape[1]
    y = _dot(a_ref[...], w_ref[0:half, :]) + _dot(r.astype(BF16), w_ref[half:, :])
    o_ref[...] = x_ref[...] + mod_ref[0, 2:3, :] * _rms(y, nw_ref[1:2, :])


def proj_out(att, rec, z, gate_w, w_out, x_all, mod, nw, n_batch, seq, n_tok, group, gate_first):
    d = x_all.shape[1]
    tm = TOK_TILE
    tpb = seq // tm
    half = att.shape[1]
    return pl.pallas_call(
        functools.partial(_proj_out_body, group, gate_first),
        out_shape=jax.ShapeDtypeStruct((n_tok, d), F32),
        grid=(n_tok // tm,),
        in_specs=[_tok_spec(half), _tok_spec(half), _tok_spec(half), _resident(gate_w.shape),
                  _resident(w_out.shape), _tok_spec(d), pl.BlockSpec((1, 6, d), _mod_rows(None, tpb, n_batch)),
                  _resident(nw.shape)],
        out_specs=_tok_spec(d),
        compiler_params=_params("parallel"),
        name="proj_out",
    )(att, rec, z, gate_w, w_out, x_all, mod, nw)


def _ffn_dense_body(n_split, x_ref, mod_ref, nw_ref, w1_ref, w3_ref, w2_ref, o_ref):
    x = x_ref[...]
    y = _rms(x, nw_ref[2:3, :])
    hb = (y * (1.0 + mod_ref[0, 4:5, :]) + mod_ref[0, 3:4, :]).astype(BF16)
    ff = w1_ref.shape[1]
    step = ff // n_split
    acc = None
    for c in range(n_split):
        sl = slice(c * step, (c + 1) * step)
        act = (_silu(_dot(hb, w1_ref[:, sl])) * _dot(hb, w3_ref[:, sl])).astype(BF16)
        part = _dot(act, w2_ref[sl, :])
        acc = part if acc is None else acc + part
    o_ref[...] = x + mod_ref[0, 5:6, :] * _rms(acc, nw_ref[3:4, :])


def ffn_dense(x_all, mod, nw, w1, w3, w2, n_batch, seq, n_tok):
    d = x_all.shape[1]
    tm = TOK_TILE
    tpb = seq // tm
    return pl.pallas_call(
        functools.partial(_ffn_dense_body, 2),
        out_shape=jax.ShapeDtypeStruct((n_tok, d), F32),
        grid=(n_tok // tm,),
        in_specs=[_tok_spec(d), pl.BlockSpec((1, 6, d), _mod_rows(None, tpb, n_batch)), _resident(nw.shape),
                  _resident(w1.shape), _resident(w3.shape), _resident(w2.shape)],
        out_specs=_tok_spec(d),
        compiler_params=_params("parallel"),
        name="ffn_dense",
    )(x_all, mod, nw, w1, w3, w2)


MOE_TILE = 1024
MOE_FF_TILE = 896


def _moe_route_body(x_ref, mod_ref, nw_ref, r_ref, h_ref, idx_ref, gate_ref):
    y = _rms(x_ref[...], nw_ref[2:3, :])
    h = y * (1.0 + mod_ref[0, 4:5, :]) + mod_ref[0, 3:4, :]
    h_ref[...] = h.astype(BF16)
    logits = jnp.dot(h, r_ref[...], precision=HIGHEST, preferred_element_type=F32)
    lane = lax.broadcasted_iota(jnp.int32, logits.shape, 1)
    logits = jnp.where(lane < N_EXPERTS, logits, -jnp.inf)
    v1 = jnp.max(logits, axis=-1, keepdims=True)
    i1 = jnp.min(jnp.where(logits == v1, lane, LANES), axis=-1, keepdims=True)
    rest = jnp.where(lane == i1, -jnp.inf, logits)
    v2 = jnp.max(rest, axis=-1, keepdims=True)
    i2 = jnp.min(jnp.where(rest == v2, lane, LANES), axis=-1, keepdims=True)
    e2 = jnp.exp(v2 - v1)
    g1 = 1.0 / (1.0 + e2)
    idx_ref[...] = jnp.where(lane == 0, i1, jnp.where(lane == 1, i2, 0))
    gate_ref[...] = jnp.where(lane == 0, g1, jnp.where(lane == 1, e2 * g1, 0.0))


def moe_route(x_all, mod, nw, router, n_batch, seq, n_tok):
    d = x_all.shape[1]
    tm = TOK_TILE
    tpb = seq // tm
    return pl.pallas_call(
        _moe_route_body,
        out_shape=(jax.ShapeDtypeStruct((n_tok, d), BF16), jax.ShapeDtypeStruct((n_tok, LANES), jnp.int32),
                   jax.ShapeDtypeStruct((n_tok, LANES), F32)),
        grid=(n_tok // tm,),
        in_specs=[_tok_spec(d), pl.BlockSpec((1, 6, d), _mod_rows(None, tpb, n_batch)), _resident(nw.shape),
                  _resident(router.shape)],
        out_specs=(_tok_spec(d), _tok_spec(LANES), _tok_spec(LANES)),
        compiler_params=_params("parallel"),
        name="moe_route",
    )(x_all, mod, nw, router)


def _moe_gmm_body(te_ref, nv_ref, x_ref, w1_ref, w3_ref, w2_ref, o_ref, acc_ref):
    i, f = pl.program_id(0), pl.program_id(1)

    @pl.when(i < nv_ref[0])
    def _():
        x = x_ref[...]
        act = (_silu(_dot(x, w1_ref[0])) * _dot(x, w3_ref[0])).astype(BF16)
        part = _dot(act, w2_ref[0])

        @pl.when(f == 0)
        def _():
            acc_ref[...] = part

        @pl.when(f > 0)
        def _():
            acc_ref[...] += part

        @pl.when(f == pl.num_programs(1) - 1)
        def _():
            o_ref[...] = acc_ref[...].astype(o_ref.dtype)

    @pl.when((i >= nv_ref[0]) & (f == pl.num_programs(1) - 1))
    def _():
        o_ref[...] = jnp.zeros(o_ref.shape, o_ref.dtype)


def moe_gmm(tile_expert, n_valid, xs, w1, w3, w2):
    p, d = xs.shape
    ff = w1.shape[2]
    tm, tf = MOE_TILE, MOE_FF_TILE
    nf = ff // tf

    def wcol(i, f, te, nv):
        live = i < nv[0]
        return (te[i], 0, jnp.where(live, f, nf - 1))

    def wrow(i, f, te, nv):
        live = i < nv[0]
        return (te[i], jnp.where(live, f, nf - 1), 0)

    return pl.pallas_call(
        _moe_gmm_body,
        out_shape=jax.ShapeDtypeStruct((p, d), F32),
        grid_spec=pltpu.PrefetchScalarGridSpec(
            num_scalar_prefetch=2,
            grid=(p // tm, nf),
            in_specs=[pl.BlockSpec((tm, d), lambda i, f, te, nv: (i, 0)),
                      pl.BlockSpec((1, d, tf), wcol), pl.BlockSpec((1, d, tf), wcol), pl.BlockSpec((1, tf, d), wrow)],
            out_specs=pl.BlockSpec((tm, d), lambda i, f, te, nv: (i, 0)),
            scratch_shapes=[pltpu.VMEM((tm, d), F32)]),
        compiler_params=_params("arbitrary", "arbitrary"),
        name="moe_gmm",
    )(tile_expert, n_valid, xs, w1, w3, w2)


def moe_dispatch(idx, n_tok):
    tm = MOE_TILE
    n_assign = n_tok * TOP_K
    p = n_assign + N_EXPERTS * tm
    e_flat = idx[:, :TOP_K].reshape(n_assign)
    onehot = (e_flat[:, None] == jnp.arange(N_EXPERTS)[None, :]).astype(jnp.int32)
    rank = jnp.take_along_axis(jnp.cumsum(onehot, axis=0) - onehot, e_flat[:, None], axis=1)[:, 0]
    counts = jnp.sum(onehot, axis=0)
    padded = ((counts + tm - 1) // tm) * tm
    ends = jnp.cumsum(padded)
    starts = ends - padded
    pos = starts[e_flat] + rank
    src_tok = jnp.zeros((p,), jnp.int32).at[pos].set(jnp.arange(n_assign, dtype=jnp.int32) // TOP_K)
    tile_start = jnp.arange(p // tm, dtype=jnp.int32) * tm
    tile_expert = jnp.minimum(jnp.sum((tile_start[:, None] >= ends[None, :]).astype(jnp.int32), axis=1), N_EXPERTS - 1)
    n_valid = (ends[-1] // tm).astype(jnp.int32).reshape(1)
    last_live = jnp.take(tile_expert, jnp.maximum(n_valid - 1, 0))
    tile_expert = jnp.where(tile_start < ends[-1], tile_expert, last_live).astype(jnp.int32)
    return src_tok, pos.reshape(n_tok, TOP_K), tile_expert, n_valid


def _moe_combine_body(pos_ref, pos_next_ref, ys_ref, gate_ref, x_ref, mod_ref, nw_ref, o_ref, ybuf, sem):
    i = pl.program_id(0)
    n_tiles = pl.num_programs(0)
    tm = x_ref.shape[0]
    slot = i % 2

    def row_copy(p_ref, s, k, r):
        return pltpu.make_async_copy(ys_ref.at[pl.ds(p_ref[0, 0, k * tm + r], 1), :], ybuf.at[s, k, pl.ds(r, 1), :], sem.at[s])

    def for_rows(fn):
        def body(r, carry):
            fn(r)
            return carry
        lax.fori_loop(0, tm, body, 0, unroll=8)

    def start_tile(p_ref, s):
        def fn(r):
            row_copy(p_ref, s, 0, r).start()
            row_copy(p_ref, s, 1, r).start()
        for_rows(fn)

    @pl.when(i == 0)
    def _():
        start_tile(pos_ref, 0)

    @pl.when(i + 1 < n_tiles)
    def _():
        start_tile(pos_next_ref, 1 - slot)

    def wait_row(r):
        row_copy(pos_ref, slot, 0, r).wait()
        row_copy(pos_ref, slot, 1, r).wait()
    for_rows(wait_row)

    g = gate_ref[...]
    ffn = ybuf[slot, 0] * g[:, 0:1] + ybuf[slot, 1] * g[:, 1:2]
    o_ref[...] = x_ref[...] + mod_ref[0, 5:6, :] * _rms(ffn, nw_ref[3:4, :])


def moe_combine(ys, pos, gates, x_all, mod, nw, n_batch, seq, n_tok):
    d = x_all.shape[1]
    tm = TOK_TILE
    tpb = seq // tm
    n_tiles = n_tok // tm
    pos_tiles = pos.reshape(n_tiles, tm, TOP_K).transpose(0, 2, 1).reshape(n_tiles, 1, TOP_K * tm)
    smem = lambda index: pl.BlockSpec((1, 1, TOP_K * tm), index, memory_space=pltpu.SMEM)
    return pl.pallas_call(
        _moe_combine_body,
        out_shape=jax.ShapeDtypeStruct((n_tok, d), F32),
        grid=(n_tiles,),
        in_specs=[smem(lambda i: (i, 0, 0)), smem(lambda i: (jnp.minimum(i + 1, n_tiles - 1), 0, 0)),
                  pl.BlockSpec(memory_space=pl.ANY), _tok_spec(LANES), _tok_spec(d),
                  pl.BlockSpec((1, 6, d), _mod_rows(None, tpb, n_batch)), _resident(nw.shape)],
        out_specs=_tok_spec(d),
        scratch_shapes=[pltpu.VMEM((2, TOP_K, tm, d), F32), pltpu.SemaphoreType.DMA((2,))],
        compiler_params=_params("arbitrary"),
        name="moe_combine",
    )(pos_tiles, pos_tiles, ys, gates, x_all, mod, nw)


def moe_layer(x_all, mod, nw, router, w1, w3, w2, n_batch, seq, n_tok):
    h, idx, gates = moe_route(x_all, mod, nw, router, n_batch, seq, n_tok)
    src_tok, pos, tile_expert, n_valid = moe_dispatch(idx, n_tok)
    ys = moe_gmm(tile_expert, n_valid, jnp.take(h, src_tok, axis=0), w1, w3, w2)
    return moe_combine(ys, pos, gates, x_all, mod, nw, n_batch, seq, n_tok)


NA_QROWS = 4
NA_KROWS = NA_QROWS + NA_WIN_ROWS - 1


def _na_key_start(blk, rows):
    return jnp.clip(blk * NA_QROWS - NA_WIN_ROWS // 2, 0, rows - NA_KROWS)


def na_bias_tables(rpb, rows):
    n_blk = rows // NA_QROWS
    blk = jnp.array([0, 1, n_blk - 1])
    q_row = blk[:, None] * NA_QROWS + jnp.arange(NA_QROWS)[None, :]
    k_row = _na_key_start(blk, rows)[:, None] + jnp.arange(NA_KROWS)[None, :]
    q_start = jnp.clip(q_row - NA_WIN_ROWS // 2, 0, rows - NA_WIN_ROWS)
    row_ok = (k_row[:, None, :] >= q_start[:, :, None]) & (k_row[:, None, :] < q_start[:, :, None] + NA_WIN_ROWS)
    dr = jnp.clip(k_row[:, None, :] - q_row[:, :, None] + NA_WIN_ROWS - 1, 0, 2 * NA_WIN_ROWS - 2)
    col = jnp.arange(GRID_W)
    col_start = jnp.clip(col - NA_WIN_COLS // 2, 0, GRID_W - NA_WIN_COLS)
    col_ok = (col[None, :] >= col_start[:, None]) & (col[None, :] < col_start[:, None] + NA_WIN_COLS)
    dc = jnp.clip(col[None, :] - col[:, None] + NA_WIN_COLS - 1, 0, 2 * NA_WIN_COLS - 2)
    bias = rpb.astype(F32)[:, dr][..., dc]
    ok = row_ok[None, :, :, :, None, None] & col_ok[None, None, None, None, :, :]
    bias = jnp.where(ok, bias, NEG_INF).transpose(0, 1, 2, 4, 3, 5)
    return bias.reshape(rpb.shape[0], 3, NA_QROWS * GRID_W, NA_KROWS * GRID_W)


def _softmax_pv(scores, values, sink=None):
    m = functools.reduce(jnp.maximum, [jnp.max(s, axis=-1, keepdims=True) for s in scores])
    if sink is not None:
        m = jnp.maximum(m, sink)
    ps = [jnp.exp(s - m) for s in scores]
    denom = functools.reduce(lambda a, b: a + b, [jnp.sum(p, axis=-1, keepdims=True) for p in ps])
    if sink is not None:
        denom = denom + jnp.exp(sink - m)
    o = functools.reduce(lambda a, b: a + b, [_dot(p.astype(BF16), v) for p, v in zip(ps, values)])
    return o / denom


def _na_body(rows, q_ref, k_ref, v_ref, qc_ref, kc_ref, vc_ref, bias_ref, o_ref, oc_ref):
    hd = NA_HD
    n_q, n_k = NA_QROWS * GRID_W, NA_KROWS * GRID_W
    n_blk = rows // NA_QROWS
    kc = kc_ref[...]
    vc = vc_ref[...]

    def blk_body(blk, carry):
        pattern = jnp.where(blk == 0, 0, jnp.where(blk == n_blk - 1, 2, 1))
        qs = pl.multiple_of(blk * n_q, n_q)
        ks = pl.multiple_of(_na_key_start(blk, rows) * GRID_W, GRID_W)
        q2 = q_ref[pl.ds(qs, n_q), :]
        k2 = k_ref[pl.ds(ks, n_k), :]
        v2 = v_ref[pl.ds(ks, n_k), :]
        outs = []
        for h in range(2):
            sl = slice(h * hd, (h + 1) * hd)
            q = q2[:, sl]
            s_loc = _dot_nt(q, k2[:, sl]) + bias_ref[h, pattern]
            s_ctx = _dot_nt(q, kc[:, sl])
            outs.append(_softmax_pv([s_loc, s_ctx], [v2[:, sl], vc[:, sl]]))
        o_ref[pl.ds(qs, n_q), :] = jnp.concatenate(outs, axis=1).astype(o_ref.dtype)
        return carry

    lax.fori_loop(0, n_blk, blk_body, 0)
    outs = []
    for h in range(2):
        sl = slice(h * hd, (h + 1) * hd)
        outs.append(_softmax_pv([_dot_nt(qc_ref[:, sl], kc[:, sl])], [vc[:, sl]]))
    oc_ref[...] = jnp.concatenate(outs, axis=1).astype(oc_ref.dtype)


def na_attention(att, bias, n_batch, seq, ctx_len):
    n_pair = NA_HEADS // 2
    ctx_blk0 = n_batch * seq // ctx_len
    lat = lambda off: pl.BlockSpec((seq, LANES), lambda b, j: (b, off + j))
    ctx = lambda off: pl.BlockSpec((ctx_len, LANES), lambda b, j: (ctx_blk0 + b, off + j))
    return pl.pallas_call(
        functools.partial(_na_body, seq // GRID_W),
        out_shape=(jax.ShapeDtypeStruct((n_batch * seq, NA_HEADS * NA_HD), BF16),
                   jax.ShapeDtypeStruct((n_batch * ctx_len, NA_HEADS * NA_HD), BF16)),
        grid=(n_batch, n_pair),
        in_specs=[lat(0), lat(n_pair), lat(2 * n_pair), ctx(0), ctx(n_pair), ctx(2 * n_pair),
                  pl.BlockSpec((2,) + bias.shape[1:], lambda b, j: (j, 0, 0, 0))],
        out_specs=(pl.BlockSpec((seq, LANES), lambda b, j: (b, j)), pl.BlockSpec((ctx_len, LANES), lambda b, j: (b, j))),
        compiler_params=_params("parallel", "arbitrary"),
        name="na_attention",
    )(att, att, att, att, att, att, bias)


def _wa_body(n_blk, with_ctx, sink_ref, q_ref, k_ref, v_ref, qc_ref, kc_ref, vc_ref, o_ref, oc_ref):
    hd, blk, grp = WA_HD, WINDOW, WA_GROUP
    rows = lax.broadcasted_iota(jnp.int32, (grp * blk, blk), 0) % blk
    cols = lax.broadcasted_iota(jnp.int32, (grp * blk, blk), 1)
    for kv in range(WA_KV_HEADS):
        ksl = slice(kv * hd, (kv + 1) * hd)
        kc = kc_ref[:, ksl]
        vc = vc_ref[:, ksl]

        def stack(ref, start, size):
            return jnp.concatenate([ref[pl.ds(start, size), (kv * grp + g) * hd:(kv * grp + g + 1) * hd]
                                    for g in range(grp)], axis=0)

        def unstack(o, size):
            return jnp.concatenate([o[g * size:(g + 1) * size] for g in range(grp)], axis=1)

        def sink_col(size):
            return jnp.concatenate([jnp.full((size, 1), sink_ref[kv * grp + g], F32) for g in range(grp)], axis=0)

        sink_q = sink_col(blk)

        def blk_body(n, carry):
            qs = pl.multiple_of(n * blk, blk)
            ps = pl.multiple_of(jnp.maximum(n - 1, 0) * blk, blk)
            ns = pl.multiple_of(jnp.minimum(n + 1, n_blk - 1) * blk, blk)
            q = stack(q_ref, qs, blk)
            s_prev = jnp.where((cols >= rows) & (n > 0), _dot_nt(q, k_ref[pl.ds(ps, blk), ksl]), NEG_INF)
            s_self = _dot_nt(q, k_ref[pl.ds(qs, blk), ksl])
            s_next = jnp.where((cols <= rows) & (n < n_blk - 1), _dot_nt(q, k_ref[pl.ds(ns, blk), ksl]), NEG_INF)
            s_ctx = _dot_nt(q, kc)
            o = _softmax_pv([s_prev, s_self, s_next, s_ctx],
                            [v_ref[pl.ds(ps, blk), ksl], v_ref[pl.ds(qs, blk), ksl], v_ref[pl.ds(ns, blk), ksl], vc],
                            sink=sink_q)
            o_ref[pl.ds(qs, blk), kv * grp * hd:(kv + 1) * grp * hd] = unstack(o, blk).astype(o_ref.dtype)
            return carry

        lax.fori_loop(0, n_blk, blk_body, 0)
        n_ctx = qc_ref.shape[0]
        if with_ctx:
            qc = stack(qc_ref, 0, n_ctx)
            oc = _softmax_pv([_dot_nt(qc, kc)], [vc], sink=sink_col(n_ctx))
            oc_ref[:, kv * grp * hd:(kv + 1) * grp * hd] = unstack(oc, n_ctx).astype(oc_ref.dtype)
        else:
            oc_ref[:, kv * grp * hd:(kv + 1) * grp * hd] = jnp.zeros((n_ctx, grp * hd), oc_ref.dtype)


def window_attention(qkv, sink, n_batch, seq, ctx_len, with_ctx):
    width = WA_HEADS * WA_HD
    ctx_blk0 = n_batch * seq // ctx_len
    kcol, vcol = width // LANES, width // LANES + 1
    return pl.pallas_call(
        functools.partial(_wa_body, seq // WINDOW, with_ctx),
        out_shape=(jax.ShapeDtypeStruct((n_batch * seq, width), BF16), jax.ShapeDtypeStruct((n_batch * ctx_len, width), BF16)),
        grid=(n_batch,),
        in_specs=[pl.BlockSpec(memory_space=pltpu.SMEM),
                  pl.BlockSpec((seq, width), lambda b: (b, 0)),
                  pl.BlockSpec((seq, LANES), lambda b: (b, kcol)),
                  pl.BlockSpec((seq, LANES), lambda b: (b, vcol)),
                  pl.BlockSpec((ctx_len, width), lambda b: (ctx_blk0 + b, 0)),
                  pl.BlockSpec((ctx_len, LANES), lambda b: (ctx_blk0 + b, kcol)),
                  pl.BlockSpec((ctx_len, LANES), lambda b: (ctx_blk0 + b, vcol))],
        out_specs=(pl.BlockSpec((seq, width), lambda b: (b, 0)), pl.BlockSpec((ctx_len, width), lambda b: (b, 0))),
        compiler_params=_params("parallel"),
        name="window_attention",
    )(sink, qkv, qkv, qkv, qkv, qkv, qkv)


CONV_ROWS = 256
CONV_HALO = 8


def _conv_silu_cols(src_ref, src_cols, w_ref, w_cols, pad_ref, dst_ref, dst_row0, dst_cols):
    n_rows = src_ref.shape[0]
    n_blk = n_rows // CONV_ROWS
    zeros = jnp.zeros((CONV_HALO, LANES), F32)
    pad_ref[0:CONV_HALO, :] = zeros
    pad_ref[CONV_HALO + n_rows:2 * CONV_HALO + n_rows, :] = zeros

    def copy(i, carry):
        r0 = pl.multiple_of(i * CONV_ROWS, CONV_ROWS)
        pad_ref[pl.ds(CONV_HALO + r0, CONV_ROWS), :] = src_ref[pl.ds(r0, CONV_ROWS), src_cols]
        return carry

    lax.fori_loop(0, n_blk, copy, 0)
    first = CONV_HALO - CONV_K // 2

    def conv(i, carry):
        r0 = pl.multiple_of(i * CONV_ROWS, CONV_ROWS)
        win = pad_ref[pl.ds(r0, CONV_ROWS + 2 * CONV_HALO), :]
        acc = win[first:first + CONV_ROWS] * w_ref[0:1, w_cols]
        for j in range(1, CONV_K):
            acc = acc + win[first + j:first + j + CONV_ROWS] * w_ref[j:j + 1, w_cols]
        dst_ref[pl.ds(dst_row0 + r0, CONV_ROWS), dst_cols] = _silu(acc)
        return carry

    lax.fori_loop(0, n_blk, conv, 0)


def _tri(upper):
    r = lax.broadcasted_iota(jnp.int32, (CHUNK, CHUNK), 0)
    c = lax.broadcasted_iota(jnp.int32, (CHUNK, CHUNK), 1)
    return (r <= c) if upper else (r >= c)


def _rows_as_lanes(a):
    return jnp.concatenate([a, jnp.zeros((LANES - CHUNK, LANES), F32)], axis=0).T[:, 0:CHUNK]


def _softplus(x):
    return jnp.maximum(x, 0.0) + jnp.log1p(jnp.exp(-jnp.abs(x)))


NEUMANN_STEPS = 5
DN_GROUP = 4
DN_BATCH = 3

def _dn_body(q_ref, k_ref, v_ref, g_ref, qc_ref, kc_ref, vc_ref, gc_ref, wq_ref, wk_ref, wv_ref, alv_ref, dtv_ref,
             o_ref, oc_ref, pad_ref, qs_ref, ks_ref, vs_ref, gb_ref, osum_ref, qe_ref, kb_ref, dec_ref):
    n_ctx, n_lat = qc_ref.shape[0], q_ref.shape[0]
    n_rows = n_ctx + n_lat
    head = pl.program_id(1)
    full = slice(0, LANES)
    for src_c, src, w_ref, dst in ((qc_ref, q_ref, wq_ref, qs_ref), (kc_ref, k_ref, wk_ref, ks_ref), (vc_ref, v_ref, wv_ref, vs_ref)):
        _conv_silu_cols(src_c, full, w_ref, full, pad_ref, dst, 0, full)
        _conv_silu_cols(src, full, w_ref, full, pad_ref, dst, n_ctx, full)

    def l2(i, carry):
        r = pl.multiple_of(i * CONV_ROWS, CONV_ROWS)
        q = qs_ref[pl.ds(r, CONV_ROWS), :]
        qs_ref[pl.ds(r, CONV_ROWS), :] = q * lax.rsqrt(jnp.sum(q * q, axis=-1, keepdims=True) + EPS) * (DN_DK ** -0.5)
        k = ks_ref[pl.ds(r, CONV_ROWS), :]
        ks_ref[pl.ds(r, CONV_ROWS), :] = k * lax.rsqrt(jnp.sum(k * k, axis=-1, keepdims=True) + EPS)
        return carry

    lax.fori_loop(0, n_rows // CONV_ROWS, l2, 0)

    def gates(raw):
        lane = lax.broadcasted_iota(jnp.int32, raw.shape, 1)
        return jnp.where(lane < 2 * DN_HEADS, jax.nn.sigmoid(raw), alv_ref[...] * _softplus(raw + dtv_ref[...]))

    gb_ref[0:n_ctx, :] = gates(gc_ref[...])
    gb_ref[n_ctx:, :] = gates(g_ref[...])
    osum_ref[...] = jnp.zeros(osum_ref.shape, F32)
    grp = DN_GROUP * CHUNK
    nb = DN_BATCH
    n_chunks = n_rows // CHUNK
    n_ctx_chunks = n_ctx // CHUNK
    lane = lax.broadcasted_iota(jnp.int32, (nb, grp, LANES), 2)
    row_i = lax.broadcasted_iota(jnp.int32, (2 * nb, grp, grp), 1)
    col_i = lax.broadcasted_iota(jnp.int32, (2 * nb, grp, grp), 2)
    rev = lax.broadcasted_iota(jnp.int32, (2 * nb, grp, grp), 0) >= nb
    tri = (rev & (row_i <= col_i)) | (jnp.logical_not(rev) & (row_i >= col_i))
    keep = ((row_i // CHUNK) == (col_i // CHUNK)) & tri
    strict = keep & (row_i != col_i)
    ones = jnp.where(keep, 1.0, 0.0).astype(BF16)

    def bdot(a, b):
        return lax.dot_general(a, b, (((2,), (1,)), ((0,), (0,))), preferred_element_type=F32)

    def bdot_nt(a, b):
        return lax.dot_general(a, b, (((2,), (2,)), ((0,), (0,))), preferred_element_type=F32)

    def bdot_tn(a, b):
        return lax.dot_general(a, b, (((1,), (1,)), ((0,), (0,))), preferred_element_type=F32)

    def both(x):
        return jnp.concatenate([x, x], axis=0)

    def trip(ti, carry):
        r = pl.multiple_of(ti * (nb * grp), nb * grp)
        load = lambda ref: ref[pl.ds(r, nb * grp), :].reshape(nb, grp, LANES)
        q, k, v, gb = load(qs_ref), load(ks_ref), load(vs_ref), load(gb_ref)
        pick = lambda l: jnp.sum(jnp.where(lane == l, gb, 0.0), axis=-1, keepdims=True)
        beta = jnp.concatenate([pick(head), pick(DN_HEADS + head)], axis=0)
        g = jnp.broadcast_to(jnp.concatenate([pick(2 * DN_HEADS + head), pick(3 * DN_HEADS + head)], axis=0), (2 * nb, grp, LANES))
        g_hi = g.astype(BF16)
        g_r = g - g_hi.astype(F32)
        g_mid = g_r.astype(BF16)
        g_lo = (g_r - g_mid.astype(F32)).astype(BF16)
        cum = bdot(ones, g_hi) + bdot(ones, g_mid) + bdot(ones, g_lo)
        cum_row = jnp.stack([cum[b].T[0:1, :] for b in range(2 * nb)], axis=0)
        decay = jnp.where(keep, jnp.exp(jnp.concatenate([cum] * (grp // LANES), axis=2) - cum_row), 0.0)
        q2, k2, v2 = both(q), both(k), both(v)
        k16 = k2.astype(BF16)
        kb = k2 * beta
        a = jnp.where(strict, bdot_nt(kb.astype(BF16), k16) * decay, 0.0)
        attn16 = (bdot_nt(q2.astype(BF16), k16) * decay).astype(BF16)
        e_cum = jnp.exp(cum)
        rhs = jnp.concatenate([v2 * beta, kb * e_cum], axis=2)
        n = -a
        a16 = a.astype(BF16)
        p = bdot(a16, a16)
        for it in range(NEUMANN_STEPS):
            p16 = p.astype(BF16)
            n = n + p + bdot(n.astype(BF16), p16)
            if it + 1 < NEUMANN_STEPS:
                p = bdot(p16, p16)
        sol16 = (rhs + bdot(n.astype(BF16), rhs.astype(BF16))).astype(BF16)
        aw = bdot(attn16, sol16)
        o0 = aw[0:nb, :, 0:DN_DK] + aw[nb:, :, 0:DN_DK]
        osum_ref[pl.ds(r, nb * grp), :] += o0.reshape(nb * grp, DN_DK)
        qe = q2 * e_cum - aw[:, :, DN_DK:]
        qe_ref[0, pl.ds(r, nb * grp), :] = qe[0:nb].reshape(nb * grp, DN_DK)
        qe_ref[1, pl.ds(r, nb * grp), :] = qe[nb:].reshape(nb * grp, DN_DK)
        cpt = nb * DN_GROUP
        cum_c = cum.reshape(2 * cpt, CHUNK, LANES)
        edge = jnp.concatenate([cum_c[0:cpt, CHUNK - 1:CHUNK, :], cum_c[cpt:, 0:1, :]], axis=0)
        kdec16 = (k2.reshape(2 * cpt, CHUNK, LANES) * jnp.exp(edge - cum_c)).astype(BF16)
        bk = bdot_tn(kdec16, sol16.reshape(2 * cpt, CHUNK, 2 * DN_DK))
        dec = jnp.broadcast_to(jnp.exp(edge), (2 * cpt, 8, LANES))
        c0 = ti * cpt
        for d in range(2):
            kb_ref[d, pl.ds(c0, cpt)] = bk[d * cpt:(d + 1) * cpt]
            dec_ref[d, pl.ds(c0, cpt)] = dec[d * cpt:(d + 1) * cpt]
        return carry

    lax.fori_loop(0, n_chunks // (DN_GROUP * nb), trip, 0)

    def step(i, states):
        new = []
        for d in range(2):
            c = i if d == 0 else jnp.where(i < n_ctx_chunks, n_ctx_chunks - 1 - i, n_chunks + n_ctx_chunks - 1 - i)
            r = pl.multiple_of(c * CHUNK, CHUNK)
            s16 = states[d].astype(BF16)
            osum_ref[pl.ds(r, CHUNK), :] += _dot(qe_ref[d, pl.ds(r, CHUNK), :].astype(BF16), s16)
            bk = kb_ref[d, c]
            new.append(states[d] * dec_ref[d, c][0:1, :] - _dot(bk[:, DN_DK:].astype(BF16), s16) + bk[:, 0:DN_DK])
        return tuple(new)

    zero = jnp.zeros((DN_DK, DN_DK), F32)
    lax.fori_loop(0, n_chunks, step, (zero, zero))
    oc_ref[...] = osum_ref[0:n_ctx, :]
    o_ref[...] = osum_ref[n_ctx:, :]


def dn_gate_vectors(a_log, dt_bias):
    def vec(v):
        return jnp.pad(v.astype(F32).reshape(1, -1), ((0, 0), (2 * DN_HEADS, LANES - 4 * DN_HEADS)))
    return vec(-jnp.exp(a_log.astype(F32))), vec(dt_bias)


def deltanet(dqkv, dg, conv_w, alv, dtv, n_batch, seq, ctx_len):
    nh = DN_HEADS
    ctx_blk0 = n_batch * seq // ctx_len
    n_rows = seq + ctx_len
    lat = lambda off: pl.BlockSpec((seq, LANES), lambda b, h: (b, off + h))
    ctx = lambda off: pl.BlockSpec((ctx_len, LANES), lambda b, h: (ctx_blk0 + b, off + h))
    cw = lambda off: pl.BlockSpec((CONV_K, LANES), lambda b, h: (0, off + h))
    vec = pl.BlockSpec((1, LANES), lambda b, h: (0, 0))
    seq_buf = pltpu.VMEM((n_rows, LANES), F32)
    return pl.pallas_call(
        _dn_body,
        out_shape=(jax.ShapeDtypeStruct((n_batch * seq, nh * DN_DK), F32), jax.ShapeDtypeStruct((n_batch * ctx_len, nh * DN_DK), F32)),
        grid=(n_batch, nh),
        in_specs=[lat(0), lat(nh), lat(2 * nh), pl.BlockSpec((seq, LANES), lambda b, h: (b, 0)),
                  ctx(0), ctx(nh), ctx(2 * nh), pl.BlockSpec((ctx_len, LANES), lambda b, h: (ctx_blk0 + b, 0)),
                  cw(0), cw(nh), cw(2 * nh), vec, vec],
        out_specs=(pl.BlockSpec((seq, LANES), lambda b, h: (b, h)), pl.BlockSpec((ctx_len, LANES), lambda b, h: (b, h))),
        scratch_shapes=[pltpu.VMEM((seq + 2 * CONV_HALO, LANES), F32), seq_buf, seq_buf, seq_buf, seq_buf, seq_buf,
                        pltpu.VMEM((2, n_rows, LANES), F32),
                        pltpu.VMEM((2, n_rows // CHUNK, DN_DK, 2 * DN_DK), F32),
                        pltpu.VMEM((2, n_rows // CHUNK, 8, LANES), F32)],
        compiler_params=_params("parallel", "arbitrary"),
        name="deltanet",
    )(dqkv, dqkv, dqkv, dg, dqkv, dqkv, dqkv, dg, conv_w, conv_w, conv_w, alv, dtv)


def _ssd_body(xbc_ref, dt_ref, xbc_c_ref, dt_c_ref, cw_ref, av_ref, dtb_ref, dsk_ref, y_ref, yc_ref,
              pad_ref, s_ref, dts_ref, da_ref, hs_ref):
    n_ctx, n_lat = xbc_c_ref.shape[0], xbc_ref.shape[0]
    inner = SSD_HEADS * SSD_HD
    hpg = SSD_HEADS // SSD_GROUPS
    gw = hpg * SSD_HD
    for cb in range(xbc_ref.shape[1] // LANES):
        cols = slice(cb * LANES, (cb + 1) * LANES)
        _conv_silu_cols(xbc_c_ref, cols, cw_ref, cols, pad_ref, s_ref, 0, cols)
        _conv_silu_cols(xbc_ref, cols, cw_ref, cols, pad_ref, s_ref, n_ctx, cols)
    dts_ref[0:n_ctx, :] = _softplus(dt_c_ref[...] + dtb_ref[...])
    dts_ref[n_ctx:, :] = _softplus(dt_ref[...] + dtb_ref[...])
    da_ref[...] = dts_ref[...] * av_ref[...]
    yc_ref[...] = s_ref[0:n_ctx, 0:inner] * dsk_ref[...]
    y_ref[...] = s_ref[n_ctx:, 0:inner] * dsk_ref[...]
    hs_ref[...] = jnp.zeros(hs_ref.shape, F32)

    def chunk(d, c, row0, out_ref):
        r = pl.multiple_of(row0 + c * CHUNK, CHUNK)
        ro = pl.multiple_of(c * CHUNK, CHUNK)
        keep = _tri(upper=(d == 1))
        acum = jnp.dot(keep.astype(F32), da_ref[pl.ds(r, CHUNK), :], precision=HIGHEST, preferred_element_type=F32)
        acum_t = _rows_as_lanes(acum)
        edge = acum[CHUNK - 1:CHUNK, :] if d == 0 else acum[0:1, :]
        dt = dts_ref[pl.ds(r, CHUNK), :]
        for g in range(SSD_GROUPS):
            lanes = [d * SSD_HEADS + g * hpg + h for h in range(hpg)]

            def expand(v, rows):
                return jnp.concatenate([jnp.broadcast_to(v[:, l:l + 1], (rows, SSD_HD)) for l in lanes], axis=1)

            x4 = s_ref[pl.ds(r, CHUNK), g * gw:(g + 1) * gw]
            bm = s_ref[pl.ds(r, CHUNK), inner + g * SSD_STATE:inner + (g + 1) * SSD_STATE].astype(BF16)
            cm = s_ref[pl.ds(r, CHUNK), inner + (SSD_GROUPS + g) * SSD_STATE:inner + (SSD_GROUPS + g + 1) * SSD_STATE].astype(BF16)
            cb_mat = _dot_nt(cm, bm)
            a4 = expand(acum, CHUNK)
            e4 = expand(edge, 1)
            xd = x4 * expand(dt, CHUNK)
            prev = hs_ref[d, g]
            y = _dot_nt(cm, prev.astype(BF16)) * jnp.exp(a4)
            diag = []
            for h, l in enumerate(lanes):
                lmat = jnp.where(keep, jnp.exp(acum[:, l:l + 1] - acum_t[l:l + 1, :]), 0.0)
                diag.append(_dot((cb_mat * lmat).astype(BF16), xd[:, h * SSD_HD:(h + 1) * SSD_HD].astype(BF16)))
            y = y + jnp.concatenate(diag, axis=1)
            out_ref[pl.ds(ro, CHUNK), g * gw:(g + 1) * gw] += y
            states = _dot_tn((xd * jnp.exp(e4 - a4)).astype(BF16), bm)
            decay = jnp.concatenate([jnp.broadcast_to(jnp.exp(edge[:, l:l + 1]), (SSD_HD, SSD_STATE)) for l in lanes], axis=0)
            hs_ref[d, g] = prev * decay + states

    def segment(n_chunks, row0, out_ref):
        def body(i, carry):
            chunk(0, i, row0, out_ref)
            chunk(1, n_chunks - 1 - i, row0, out_ref)
            return carry
        lax.fori_loop(0, n_chunks, body, 0)

    segment(n_ctx // CHUNK, 0, yc_ref)
    segment(n_lat // CHUNK, n_ctx, y_ref)


def ssd(xbc, dt, conv_w, a_vec, dtb_vec, skip_vec, n_batch, seq, ctx_len):
    width = xbc.shape[1]
    inner = SSD_HEADS * SSD_HD
    ctx_blk0 = n_batch * seq // ctx_len
    n_rows = seq + ctx_len
    return pl.pallas_call(
        _ssd_body,
        out_shape=(jax.ShapeDtypeStruct((n_batch * seq, inner), F32), jax.ShapeDtypeStruct((n_batch * ctx_len, inner), F32)),
        grid=(n_batch,),
        in_specs=[pl.BlockSpec((seq, width), lambda b: (b, 0)), pl.BlockSpec((seq, LANES), lambda b: (b, 0)),
                  pl.BlockSpec((ctx_len, width), lambda b: (ctx_blk0 + b, 0)), pl.BlockSpec((ctx_len, LANES), lambda b: (ctx_blk0 + b, 0)),
                  _resident(conv_w.shape), _resident(a_vec.shape), _resident(dtb_vec.shape), _resident(skip_vec.shape)],
        out_specs=(pl.BlockSpec((seq, inner), lambda b: (b, 0)), pl.BlockSpec((ctx_len, inner), lambda b: (b, 0))),
        scratch_shapes=[pltpu.VMEM((seq + 2 * CONV_HALO, LANES), F32), pltpu.VMEM((n_rows, width), F32),
                        pltpu.VMEM((n_rows, LANES), F32), pltpu.VMEM((n_rows, LANES), F32),
                        pltpu.VMEM((2, SSD_GROUPS, (SSD_HEADS // SSD_GROUPS) * SSD_HD, SSD_STATE), F32)],
        compiler_params=_params("parallel"),
        name="ssd",
    )(xbc, dt, xbc, dt, conv_w, a_vec, dtb_vec, skip_vec)


def ssd_vectors(a_log, dt_bias, d_skip):
    def vec(v):
        return jnp.pad(v.astype(F32).reshape(1, -1), ((0, 0), (0, LANES - 2 * SSD_HEADS)))
    return vec(-jnp.exp(a_log.astype(F32))), vec(dt_bias), jnp.repeat(d_skip.astype(F32), SSD_HD)[None, :]


def kernel(x, c, ctx, c_ctx, ada_w, ada_b, norm_w, ev_w_in, ev_w_out, na_rpb, dn_conv, dn_a_log, dn_dt_bias, dn_norm, ffn_w1, ffn_w3, ffn_w2, od_w_in, od_w_out, wa_sink, ssd_conv, ssd_a_log, ssd_dt_bias, ssd_d, ssd_norm, moe_router, moe_w1, moe_w3, moe_w2):
    n_batch, seq, d = x.shape
    ctx_len = ctx.shape[1]
    depth = ada_w.shape[0]
    n_lat, n_ctx = n_batch * seq, n_batch * ctx_len
    n_cond = -(-(n_batch + 1) // 8) * 8
    cond = jnp.concatenate([c, c_ctx[None, :], jnp.zeros((n_cond - n_batch - 1, d), F32)], axis=0)
    mod_all = ada_modulation(cond, ada_w, ada_b)
    x_all = jnp.concatenate([x.reshape(n_lat, d), ctx.reshape(n_ctx, d)], axis=0)
    cos_t, sin_t = rope_tables(seq)
    for layer in range(depth):
        i = layer // 2
        n_tok = n_lat + n_ctx if layer < depth - 1 else n_lat
        mod = mod_all[layer, :n_batch + 1].reshape(n_batch + 1, 6, d)
        nw = norm_w[layer]
        if layer % 2 == 0:
            att, dqkv, dz, dg = proj_in_even(x_all, mod, nw, prep_w_in_even(ev_w_in[i]), n_batch, seq)
            o_a, oc_a = na_attention(att, na_bias_tables(na_rpb[i], seq // GRID_W), n_batch, seq, ctx_len)
            alv, dtv = dn_gate_vectors(dn_a_log[i], dn_dt_bias[i])
            o_b, oc_b = deltanet(dqkv, dg, dn_conv[i], alv, dtv, n_batch, seq, ctx_len)
            x_all = proj_out(jnp.concatenate([o_a, oc_a], axis=0), jnp.concatenate([o_b, oc_b], axis=0), dz,
                             jnp.tile(dn_norm[i], DN_HEADS)[None, :], ev_w_out[i].astype(BF16), x_all, mod, nw,
                             n_batch, seq, n_tok, DN_DK, False)
            x_all = ffn_dense(x_all, mod, nw, ffn_w1[i].astype(BF16), ffn_w3[i].astype(BF16), ffn_w2[i].astype(BF16),
                              n_batch, seq, n_tok)
        else:
            qkv, z, xbc, dt = proj_in_odd(x_all, mod, nw, prep_w_in_odd(od_w_in[i]), cos_t, sin_t, n_batch, seq)
            o_c, oc_c = window_attention(qkv, wa_sink[i], n_batch, seq, ctx_len, n_tok > n_lat)
            a_vec, dtb_vec, skip_vec = ssd_vectors(ssd_a_log[i], ssd_dt_bias[i], ssd_d[i])
            y_d, yc_d = ssd(xbc, dt, ssd_conv[i], a_vec, dtb_vec, skip_vec, n_batch, seq, ctx_len)
            x_all = proj_out(jnp.concatenate([o_c, oc_c], axis=0), jnp.concatenate([y_d, yc_d], axis=0), z,
                             ssd_norm[i][None, :], od_w_out[i].astype(BF16), x_all, mod, nw,
                             n_batch, seq, n_tok, SSD_HEADS * SSD_HD // SSD_GROUPS, True)
            router = jnp.pad(moe_router[i], ((0, 0), (0, LANES - N_EXPERTS)))
            x_all = moe_layer(x_all, mod, nw, router, moe_w1[i].astype(BF16), moe_w3[i].astype(BF16),
                              moe_w2[i].astype(BF16), n_batch, seq, n_tok)
    return x_all[:n_lat].reshape(n_batch, seq, d)
```

```python
import functools
import math

import jax
import jax.numpy as jnp
import numpy as np
from jax import lax
from jax.experimental import pallas as pl
from jax.experimental.pallas import tpu as pltpu

F32 = jnp.float32
BF16 = jnp.bfloat16
HIGHEST = lax.Precision.HIGHEST

EPS = 1e-6
NEG_INF = -1e30
GRID_W = 64
CHUNK = 64
CONV_K = 5
ROPE_BASE = 10000.0
NA_HEADS, NA_HD, NA_WIN_ROWS, NA_WIN_COLS = 8, 64, 8, 16
DN_HEADS, DN_DK = 4, 128
WA_HEADS, WA_KV_HEADS, WA_HD, WINDOW = 8, 2, 64, 128
WA_GROUP = WA_HEADS // WA_KV_HEADS
SSD_HEADS, SSD_HD, SSD_GROUPS, SSD_STATE = 8, 64, 2, 128
N_EXPERTS, TOP_K = 8, 2

LANES = 128
VMEM_LIMIT = 56 * 1024 * 1024
TOK_TILE = 512


def _params(*sem):
    return pltpu.CompilerParams(dimension_semantics=sem, vmem_limit_bytes=VMEM_LIMIT)


def _resident(shape):
    nd = len(shape)
    return pl.BlockSpec(shape, lambda *_: (0,) * nd, pipeline_mode=pl.Buffered(1))


def _silu(x):
    return x * jax.nn.sigmoid(x)


def _rms(x, w):
    return x * lax.rsqrt(jnp.mean(x * x, axis=-1, keepdims=True) + EPS) * w


def _dot(a, b):
    return jnp.dot(a, b, preferred_element_type=F32)


def _dot_nt(a, b):
    return lax.dot_general(a, b, (((1,), (1,)), ((), ())), preferred_element_type=F32)


def _dot_tn(a, b):
    return lax.dot_general(a, b, (((0,), (0,)), ((), ())), preferred_element_type=F32)


def _ada_body(s_ref, w_ref, b_ref, o_ref):
    s = _silu(s_ref[...])
    o_ref[0] = jnp.dot(s, w_ref[0], precision=HIGHEST, preferred_element_type=F32) + b_ref[0]


def ada_modulation(cond, ada_w, ada_b):
    depth, d, n = ada_w.shape
    r = cond.shape[0]
    tn = 1536
    return pl.pallas_call(
        _ada_body,
        out_shape=jax.ShapeDtypeStruct((depth, r, n), F32),
        grid=(depth, n // tn),
        in_specs=[
            pl.BlockSpec((r, d), lambda l, j: (0, 0)),
            pl.BlockSpec((1, d, tn), lambda l, j: (l, 0, j)),
            pl.BlockSpec((1, 1, tn), lambda l, j: (l, 0, j)),
        ],
        out_specs=pl.BlockSpec((1, r, tn), lambda l, j: (l, 0, j)),
        compiler_params=_params("arbitrary", "arbitrary"),
        name="ada_modulation",
    )(cond, ada_w, ada_b.reshape(depth, 1, n))


def _mod_rows(n_lat_tiles, tiles_per_batch, n_batch):
    def index(i):
        return (jnp.minimum(i // tiles_per_batch, n_batch), 0, 0)
    del n_lat_tiles
    return index


def _norm_mod(x, mod_ref, nw_ref, row):
    y = _rms(x, nw_ref[row:row + 1, :])
    return y * (1.0 + mod_ref[0, row + 1:row + 2, :]) + mod_ref[0, row:row + 1, :]


def _proj_even_body(x_ref, mod_ref, nw_ref, w_ref, att_ref, dqkv_ref, dz_ref, dg_ref):
    hb = _norm_mod(x_ref[...], mod_ref, nw_ref, 0).astype(BF16)
    att_ref[...] = _dot(hb, w_ref[:, 0:1536]).astype(BF16)
    dqkv_ref[...] = _dot(hb, w_ref[:, 1536:3072])
    dz_ref[...] = _dot(hb, w_ref[:, 3072:3584])
    dg_ref[...] = _dot(hb, w_ref[:, 3584:3712])


def _swap16(x):
    lane = lax.broadcasted_iota(jnp.int32, x.shape, 1)
    up = pltpu.roll(x, LANES - 16, 1)
    down = pltpu.roll(x, 16, 1)
    return jnp.where((lane % 32) < 16, up, down)


def _proj_odd_body(x_ref, mod_ref, nw_ref, w_ref, cos_ref, sin_ref, qkv_ref, z_ref, xbc_ref, dt_ref):
    hb = _norm_mod(x_ref[...], mod_ref, nw_ref, 0).astype(BF16)
    cos, sin = cos_ref[...], sin_ref[...]
    for c in range(5):
        t = _dot(hb, w_ref[:, c * LANES:(c + 1) * LANES])
        qkv_ref[:, c * LANES:(c + 1) * LANES] = (t * cos + _swap16(t) * sin).astype(BF16)
    qkv_ref[:, 640:768] = _dot(hb, w_ref[:, 640:768]).astype(BF16)
    z_ref[...] = _dot(hb, w_ref[:, 768:1280])
    xbc_ref[...] = _dot(hb, w_ref[:, 1280:2304])
    dt_ref[...] = _dot(hb, w_ref[:, 2304:2432])


def _pad_cols(w, n):
    return jnp.pad(w, ((0, 0), (0, n - w.shape[1])))


def prep_w_in_even(w):
    q_scale = jnp.concatenate([jnp.full((NA_HEADS * NA_HD,), NA_HD ** -0.5, F32), jnp.ones((w.shape[1] - NA_HEADS * NA_HD,), F32)])
    return _pad_cols(w * q_scale[None, :], 3712).astype(BF16)


def prep_w_in_odd(w):
    q_scale = jnp.concatenate([jnp.full((WA_HEADS * WA_HD,), WA_HD ** -0.5, F32), jnp.ones((w.shape[1] - WA_HEADS * WA_HD,), F32)])
    return _pad_cols(w * q_scale[None, :], 2432).astype(BF16)


def _tok_spec(width, tm=TOK_TILE):
    return pl.BlockSpec((tm, width), lambda i: (i, 0))


def proj_in_even(x_all, mod, nw, w, n_batch, seq):
    t_all, d = x_all.shape
    tm = TOK_TILE
    tpb = seq // tm
    return pl.pallas_call(
        _proj_even_body,
        out_shape=(jax.ShapeDtypeStruct((t_all, 1536), BF16), jax.ShapeDtypeStruct((t_all, 1536), F32),
                   jax.ShapeDtypeStruct((t_all, 512), F32), jax.ShapeDtypeStruct((t_all, LANES), F32)),
        grid=(t_all // tm,),
        in_specs=[_tok_spec(d), pl.BlockSpec((1, 6, d), _mod_rows(None, tpb, n_batch)),
                  _resident(nw.shape), _resident(w.shape)],
        out_specs=(_tok_spec(1536), _tok_spec(1536), _tok_spec(512), _tok_spec(LANES)),
        compiler_params=_params("parallel"),
        name="proj_in_even",
    )(x_all, mod, nw, w)


def proj_in_odd(x_all, mod, nw, w, cos_t, sin_t, n_batch, seq):
    t_all, d = x_all.shape
    tm = TOK_TILE
    tpb = seq // tm

    def rope_rows(i):
        return (jnp.where(i < n_batch * tpb, i % tpb, tpb), 0)

    return pl.pallas_call(
        _proj_odd_body,
        out_shape=(jax.ShapeDtypeStruct((t_all, 768), BF16), jax.ShapeDtypeStruct((t_all, 512), F32),
                   jax.ShapeDtypeStruct((t_all, 1024), F32), jax.ShapeDtypeStruct((t_all, LANES), F32)),
        grid=(t_all // tm,),
        in_specs=[_tok_spec(d), pl.BlockSpec((1, 6, d), _mod_rows(None, tpb, n_batch)),
                  _resident(nw.shape), _resident(w.shape),
                  pl.BlockSpec((tm, LANES), rope_rows), pl.BlockSpec((tm, LANES), rope_rows)],
        out_specs=(_tok_spec(768), _tok_spec(512), _tok_spec(1024), _tok_spec(LANES)),
        compiler_params=_params("parallel"),
        name="proj_in_odd",
    )(x_all, mod, nw, w, cos_t, sin_t)


def rope_tables(seq, tm=TOK_TILE):
    half = WA_HD // 4
    freqs = ROPE_BASE ** (-jnp.arange(half, dtype=F32) / half)
    t = jnp.arange(seq)
    ang_r = (t // GRID_W).astype(F32)[:, None] * freqs[None, :]
    ang_c = (t % GRID_W).astype(F32)[:, None] * freqs[None, :]
    cos = jnp.concatenate([jnp.cos(ang_r)] * 2 + [jnp.cos(ang_c)] * 2, axis=-1)
    sin = jnp.concatenate([-jnp.sin(ang_r), jnp.sin(ang_r), -jnp.sin(ang_c), jnp.sin(ang_c)], axis=-1)
    cos = jnp.concatenate([jnp.tile(cos, (1, 2)), jnp.ones((tm, LANES), F32)], axis=0)
    sin = jnp.concatenate([jnp.tile(sin, (1, 2)), jnp.zeros((tm, LANES), F32)], axis=0)
    return cos, sin


def _proj_out_body(group, gate_first, a_ref, r_ref, z_ref, gw_ref, w_ref, x_ref, mod_ref, nw_ref, o_ref):
    r = r_ref[...]
    gate = _silu(z_ref[...])
    if gate_first:
        r = r * gate
    parts = []
    for c in range(r.shape[1] // group):
        rc = r[:, c * group:(c + 1) * group]
        parts.append(_rms(rc, gw_ref[:, c * group:(c + 1) * group]))
    r = jnp.concatenate(parts, axis=1)
    if not gate_first:
        r = r * gate
    half = a_ref.shape[1]
    y = _dot(a_ref[...], w_ref[0:half, :]) + _dot(r.astype(BF16), w_ref[half:, :])
    o_ref[...] = x_ref[...] + mod_ref[0, 2:3, :] * _rms(y, nw_ref[1:2, :])


def proj_out(att, rec, z, gate_w, w_out, x_all, mod, nw, n_batch, seq, n_tok, group, gate_first):
    d = x_all.shape[1]
    tm = TOK_TILE
    tpb = seq // tm
    half = att.shape[1]
    return pl.pallas_call(
        functools.partial(_proj_out_body, group, gate_first),
        out_shape=jax.ShapeDtypeStruct((n_tok, d), F32),
        grid=(n_tok // tm,),
        in_specs=[_tok_spec(half), _tok_spec(half), _tok_spec(half), _resident(gate_w.shape),
                  _resident(w_out.shape), _tok_spec(d), pl.BlockSpec((1, 6, d), _mod_rows(None, tpb, n_batch)),
                  _resident(nw.shape)],
        out_specs=_tok_spec(d),
        compiler_params=_params("parallel"),
        name="proj_out",
    )(att, rec, z, gate_w, w_out, x_all, mod, nw)


def _ffn_dense_body(n_split, x_ref, mod_ref, nw_ref, w1_ref, w3_ref, w2_ref, o_ref):
    x = x_ref[...]
    y = _rms(x, nw_ref[2:3, :])
    hb = (y * (1.0 + mod_ref[0, 4:5, :]) + mod_ref[0, 3:4, :]).astype(BF16)
    ff = w1_ref.shape[1]
    step = ff // n_split
    acc = None
    for c in range(n_split):
        sl = slice(c * step, (c + 1) * step)
        act = (_silu(_dot(hb, w1_ref[:, sl])) * _dot(hb, w3_ref[:, sl])).astype(BF16)
        part = _dot(act, w2_ref[sl, :])
        acc = part if acc is None else acc + part
    o_ref[...] = x + mod_ref[0, 5:6, :] * _rms(acc, nw_ref[3:4, :])


def ffn_dense(x_all, mod, nw, w1, w3, w2, n_batch, seq, n_tok):
    d = x_all.shape[1]
    tm = TOK_TILE
    tpb = seq // tm
    return pl.pallas_call(
        functools.partial(_ffn_dense_body, 2),
        out_shape=jax.ShapeDtypeStruct((n_tok, d), F32),
        grid=(n_tok // tm,),
        in_specs=[_tok_spec(d), pl.BlockSpec((1, 6, d), _mod_rows(None, tpb, n_batch)), _resident(nw.shape),
                  _resident(w1.shape), _resident(w3.shape), _resident(w2.shape)],
        out_specs=_tok_spec(d),
        compiler_params=_params("parallel"),
        name="ffn_dense",
    )(x_all, mod, nw, w1, w3, w2)


MOE_TILE = 1024
MOE_FF_TILE = 896


def _moe_route_body(x_ref, mod_ref, nw_ref, r_ref, h_ref, idx_ref, gate_ref):
    y = _rms(x_ref[...], nw_ref[2:3, :])
    h = y * (1.0 + mod_ref[0, 4:5, :]) + mod_ref[0, 3:4, :]
    h_ref[...] = h
    logits = jnp.dot(h, r_ref[...], precision=HIGHEST, preferred_element_type=F32)
    lane = lax.broadcasted_iota(jnp.int32, logits.shape, 1)
    logits = jnp.where(lane < N_EXPERTS, logits, -jnp.inf)
    v1 = jnp.max(logits, axis=-1, keepdims=True)
    i1 = jnp.min(jnp.where(logits == v1, lane, LANES), axis=-1, keepdims=True)
    rest = jnp.where(lane == i1, -jnp.inf, logits)
    v2 = jnp.max(rest, axis=-1, keepdims=True)
    i2 = jnp.min(jnp.where(rest == v2, lane, LANES), axis=-1, keepdims=True)
    e2 = jnp.exp(v2 - v1)
    g1 = 1.0 / (1.0 + e2)
    idx_ref[...] = jnp.where(lane == 0, i1, jnp.where(lane == 1, i2, 0))
    gate_ref[...] = jnp.where(lane == 0, g1, jnp.where(lane == 1, e2 * g1, 0.0))


def moe_route(x_all, mod, nw, router, n_batch, seq, n_tok):
    d = x_all.shape[1]
    tm = TOK_TILE
    tpb = seq // tm
    return pl.pallas_call(
        _moe_route_body,
        out_shape=(jax.ShapeDtypeStruct((n_tok, d), F32), jax.ShapeDtypeStruct((n_tok, LANES), jnp.int32),
                   jax.ShapeDtypeStruct((n_tok, LANES), F32)),
        grid=(n_tok // tm,),
        in_specs=[_tok_spec(d), pl.BlockSpec((1, 6, d), _mod_rows(None, tpb, n_batch)), _resident(nw.shape),
                  _resident(router.shape)],
        out_specs=(_tok_spec(d), _tok_spec(LANES), _tok_spec(LANES)),
        compiler_params=_params("parallel"),
        name="moe_route",
    )(x_all, mod, nw, router)


def _moe_gmm_body(te_ref, nv_ref, src_ref, src_next_ref, h_ref, w1_ref, w3_ref, w2_ref, o_ref, xbuf, x16, sem):
    i, f = pl.program_id(0), pl.program_id(1)
    n_live = nv_ref[0]
    tm = x16.shape[0]
    slot = i % 2

    def row_copy(s_ref, s, r):
        return pltpu.make_async_copy(h_ref.at[pl.ds(s_ref[0, 0, r], 1), :], xbuf.at[s, pl.ds(r, 1), :], sem.at[s])

    def for_rows(fn):
        def body(r, carry):
            fn(r)
            return carry
        lax.fori_loop(0, tm, body, 0, unroll=8)

    @pl.when((f == 0) & (i == 0) & (n_live > 0))
    def _():
        for_rows(lambda r: row_copy(src_ref, 0, r).start())

    @pl.when((f == 0) & (i < n_live))
    def _():
        for_rows(lambda r: row_copy(src_ref, slot, r).wait())
        x16[...] = xbuf[slot].astype(BF16)

    @pl.when((f == 1) & (i + 1 < n_live))
    def _():
        for_rows(lambda r: row_copy(src_next_ref, 1 - slot, r).start())

    @pl.when(i < n_live)
    def _():
        x = x16[...]
        act = (_silu(_dot(x, w1_ref[0])) * _dot(x, w3_ref[0])).astype(BF16)
        part = _dot(act, w2_ref[0])

        @pl.when(f == 0)
        def _():
            o_ref[...] = part

        @pl.when(f > 0)
        def _():
            o_ref[...] += part

    @pl.when((i >= n_live) & (f == 0))
    def _():
        o_ref[...] = jnp.zeros(o_ref.shape, o_ref.dtype)


def moe_gmm(tile_expert, n_valid, src_tok, h, w1, w3, w2):
    p = src_tok.shape[0]
    d = h.shape[1]
    ff = w1.shape[2]
    tm, tf = MOE_TILE, MOE_FF_TILE
    nt, nf = p // tm, ff // tf
    assert nf >= 2

    def wcol(i, f, te, nv):
        live = i < nv[0]
        return (te[i], 0, jnp.where(live, f, nf - 1))

    def wrow(i, f, te, nv):
        live = i < nv[0]
        return (te[i], jnp.where(live, f, nf - 1), 0)

    src_tiles = src_tok.reshape(nt, 1, tm)
    smem = lambda index: pl.BlockSpec((1, 1, tm), index, memory_space=pltpu.SMEM)
    return pl.pallas_call(
        _moe_gmm_body,
        out_shape=jax.ShapeDtypeStruct((p, d), F32),
        grid_spec=pltpu.PrefetchScalarGridSpec(
            num_scalar_prefetch=2,
            grid=(nt, nf),
            in_specs=[smem(lambda i, f, te, nv: (i, 0, 0)), smem(lambda i, f, te, nv: (jnp.minimum(i + 1, nt - 1), 0, 0)),
                      pl.BlockSpec(memory_space=pl.ANY),
                      pl.BlockSpec((1, d, tf), wcol), pl.BlockSpec((1, d, tf), wcol), pl.BlockSpec((1, tf, d), wrow)],
            out_specs=pl.BlockSpec((tm, d), lambda i, f, te, nv: (i, 0)),
            scratch_shapes=[pltpu.VMEM((2, tm, d), F32), pltpu.VMEM((tm, d), BF16), pltpu.SemaphoreType.DMA((2,))]),
        compiler_params=_params("arbitrary", "arbitrary"),
        name="moe_gmm",
    )(tile_expert, n_valid, src_tiles, src_tiles, h, w1, w3, w2)


def moe_dispatch(idx, n_tok):
    tm = MOE_TILE
    n_assign = n_tok * TOP_K
    p = n_assign + N_EXPERTS * tm
    e_flat = idx[:, :TOP_K].reshape(n_assign)
    onehot = (e_flat[:, None] == jnp.arange(N_EXPERTS)[None, :]).astype(jnp.int32)
    rank = jnp.take_along_axis(jnp.cumsum(onehot, axis=0) - onehot, e_flat[:, None], axis=1)[:, 0]
    counts = jnp.sum(onehot, axis=0)
    padded = ((counts + tm - 1) // tm) * tm
    ends = jnp.cumsum(padded)
    starts = ends - padded
    pos = starts[e_flat] + rank
    src_tok = jnp.zeros((p,), jnp.int32).at[pos].set(jnp.arange(n_assign, dtype=jnp.int32) // TOP_K)
    tile_start = jnp.arange(p // tm, dtype=jnp.int32) * tm
    tile_expert = jnp.minimum(jnp.sum((tile_start[:, None] >= ends[None, :]).astype(jnp.int32), axis=1), N_EXPERTS - 1)
    n_valid = (ends[-1] // tm).astype(jnp.int32).reshape(1)
    last_live = jnp.take(tile_expert, jnp.maximum(n_valid - 1, 0))
    tile_expert = jnp.where(tile_start < ends[-1], tile_expert, last_live).astype(jnp.int32)
    return src_tok, pos.reshape(n_tok, TOP_K), tile_expert, n_valid


def _moe_combine_body(pos_ref, pos_next_ref, ys_ref, gate_ref, x_ref, mod_ref, nw_ref, o_ref, ybuf, sem):
    i = pl.program_id(0)
    n_tiles = pl.num_programs(0)
    tm = x_ref.shape[0]
    slot = i % 2

    def row_copy(p_ref, s, k, r):
        return pltpu.make_async_copy(ys_ref.at[pl.ds(p_ref[0, 0, k * tm + r], 1), :], ybuf.at[s, k, pl.ds(r, 1), :], sem.at[s])

    def for_rows(fn):
        def body(r, carry):
            fn(r)
            return carry
        lax.fori_loop(0, tm, body, 0, unroll=8)

    def start_tile(p_ref, s):
        def fn(r):
            row_copy(p_ref, s, 0, r).start()
            row_copy(p_ref, s, 1, r).start()
        for_rows(fn)

    @pl.when(i == 0)
    def _():
        start_tile(pos_ref, 0)

    @pl.when(i + 1 < n_tiles)
    def _():
        start_tile(pos_next_ref, 1 - slot)

    def wait_row(r):
        row_copy(pos_ref, slot, 0, r).wait()
        row_copy(pos_ref, slot, 1, r).wait()
    for_rows(wait_row)

    g = gate_ref[...]
    ffn = ybuf[slot, 0] * g[:, 0:1] + ybuf[slot, 1] * g[:, 1:2]
    o_ref[...] = x_ref[...] + mod_ref[0, 5:6, :] * _rms(ffn, nw_ref[3:4, :])


def moe_combine(ys, pos, gates, x_all, mod, nw, n_batch, seq, n_tok):
    d = x_all.shape[1]
    tm = TOK_TILE
    tpb = seq // tm
    n_tiles = n_tok // tm
    pos_tiles = pos.reshape(n_tiles, tm, TOP_K).transpose(0, 2, 1).reshape(n_tiles, 1, TOP_K * tm)
    smem = lambda index: pl.BlockSpec((1, 1, TOP_K * tm), index, memory_space=pltpu.SMEM)
    return pl.pallas_call(
        _moe_combine_body,
        out_shape=jax.ShapeDtypeStruct((n_tok, d), F32),
        grid=(n_tiles,),
        in_specs=[smem(lambda i: (i, 0, 0)), smem(lambda i: (jnp.minimum(i + 1, n_tiles - 1), 0, 0)),
                  pl.BlockSpec(memory_space=pl.ANY), _tok_spec(LANES), _tok_spec(d),
                  pl.BlockSpec((1, 6, d), _mod_rows(None, tpb, n_batch)), _resident(nw.shape)],
        out_specs=_tok_spec(d),
        scratch_shapes=[pltpu.VMEM((2, TOP_K, tm, d), F32), pltpu.SemaphoreType.DMA((2,))],
        compiler_params=_params("arbitrary"),
        name="moe_combine",
    )(pos_tiles, pos_tiles, ys, gates, x_all, mod, nw)


def moe_layer(x_all, mod, nw, router, w1, w3, w2, n_batch, seq, n_tok):
    h, idx, gates = moe_route(x_all, mod, nw, router, n_batch, seq, n_tok)
    src_tok, pos, tile_expert, n_valid = moe_dispatch(idx, n_tok)
    ys = moe_gmm(tile_expert, n_valid, src_tok, h, w1, w3, w2)
    return moe_combine(ys, pos, gates, x_all, mod, nw, n_batch, seq, n_tok)


NA_QROWS = 4
NA_KROWS = NA_QROWS + NA_WIN_ROWS - 1


def _na_key_start(blk, rows):
    return jnp.clip(blk * NA_QROWS - NA_WIN_ROWS // 2, 0, rows - NA_KROWS)


def na_bias_tables(rpb, rows):
    n_blk = rows // NA_QROWS
    blk = jnp.array([0, 1, n_blk - 1])
    q_row = blk[:, None] * NA_QROWS + jnp.arange(NA_QROWS)[None, :]
    k_row = _na_key_start(blk, rows)[:, None] + jnp.arange(NA_KROWS)[None, :]
    q_start = jnp.clip(q_row - NA_WIN_ROWS // 2, 0, rows - NA_WIN_ROWS)
    row_ok = (k_row[:, None, :] >= q_start[:, :, None]) & (k_row[:, None, :] < q_start[:, :, None] + NA_WIN_ROWS)
    dr = jnp.clip(k_row[:, None, :] - q_row[:, :, None] + NA_WIN_ROWS - 1, 0, 2 * NA_WIN_ROWS - 2)
    col = jnp.arange(GRID_W)
    col_start = jnp.clip(col - NA_WIN_COLS // 2, 0, GRID_W - NA_WIN_COLS)
    col_ok = (col[None, :] >= col_start[:, None]) & (col[None, :] < col_start[:, None] + NA_WIN_COLS)
    dc = jnp.clip(col[None, :] - col[:, None] + NA_WIN_COLS - 1, 0, 2 * NA_WIN_COLS - 2)
    bias = rpb.astype(F32)[:, dr][..., dc]
    ok = row_ok[None, :, :, :, None, None] & col_ok[None, None, None, None, :, :]
    bias = jnp.where(ok, bias, NEG_INF).transpose(0, 1, 2, 4, 3, 5)
    return bias.reshape(rpb.shape[0], 3, NA_QROWS * GRID_W, NA_KROWS * GRID_W)


def _softmax_pv(scores, values, sink=None):
    m = functools.reduce(jnp.maximum, [jnp.max(s, axis=-1, keepdims=True) for s in scores])
    if sink is not None:
        m = jnp.maximum(m, sink)
    ps = [jnp.exp(s - m) for s in scores]
    denom = functools.reduce(lambda a, b: a + b, [jnp.sum(p, axis=-1, keepdims=True) for p in ps])
    if sink is not None:
        denom = denom + jnp.exp(sink - m)
    o = functools.reduce(lambda a, b: a + b, [_dot(p.astype(BF16), v) for p, v in zip(ps, values)])
    return o / denom


def _na_body(rows, q_ref, k_ref, v_ref, qc_ref, kc_ref, vc_ref, bias_ref, o_ref, oc_ref):
    hd = NA_HD
    n_q, n_k = NA_QROWS * GRID_W, NA_KROWS * GRID_W
    n_blk = rows // NA_QROWS
    kc = kc_ref[...]
    vc = vc_ref[...]

    def blk_body(blk, carry):
        pattern = jnp.where(blk == 0, 0, jnp.where(blk == n_blk - 1, 2, 1))
        qs = pl.multiple_of(blk * n_q, n_q)
        ks = pl.multiple_of(_na_key_start(blk, rows) * GRID_W, GRID_W)
        q2 = q_ref[pl.ds(qs, n_q), :]
        k2 = k_ref[pl.ds(ks, n_k), :]
        v2 = v_ref[pl.ds(ks, n_k), :]
        outs = []
        for h in range(2):
            sl = slice(h * hd, (h + 1) * hd)
            q = q2[:, sl]
            s_loc = _dot_nt(q, k2[:, sl]) + bias_ref[h, pattern]
            s_ctx = _dot_nt(q, kc[:, sl])
            outs.append(_softmax_pv([s_loc, s_ctx], [v2[:, sl], vc[:, sl]]))
        o_ref[pl.ds(qs, n_q), :] = jnp.concatenate(outs, axis=1).astype(o_ref.dtype)
        return carry

    lax.fori_loop(0, n_blk, blk_body, 0)
    outs = []
    for h in range(2):
        sl = slice(h * hd, (h + 1) * hd)
        outs.append(_softmax_pv([_dot_nt(qc_ref[:, sl], kc[:, sl])], [vc[:, sl]]))
    oc_ref[...] = jnp.concatenate(outs, axis=1).astype(oc_ref.dtype)


def na_attention(att, bias, n_batch, seq, ctx_len):
    n_pair = NA_HEADS // 2
    ctx_blk0 = n_batch * seq // ctx_len
    lat = lambda off: pl.BlockSpec((seq, LANES), lambda b, j: (b, off + j))
    ctx = lambda off: pl.BlockSpec((ctx_len, LANES), lambda b, j: (ctx_blk0 + b, off + j))
    return pl.pallas_call(
        functools.partial(_na_body, seq // GRID_W),
        out_shape=(jax.ShapeDtypeStruct((n_batch * seq, NA_HEADS * NA_HD), BF16),
                   jax.ShapeDtypeStruct((n_batch * ctx_len, NA_HEADS * NA_HD), BF16)),
        grid=(n_batch, n_pair),
        in_specs=[lat(0), lat(n_pair), lat(2 * n_pair), ctx(0), ctx(n_pair), ctx(2 * n_pair),
                  pl.BlockSpec((2,) + bias.shape[1:], lambda b, j: (j, 0, 0, 0))],
        out_specs=(pl.BlockSpec((seq, LANES), lambda b, j: (b, j)), pl.BlockSpec((ctx_len, LANES), lambda b, j: (b, j))),
        compiler_params=_params("parallel", "arbitrary"),
        name="na_attention",
    )(att, att, att, att, att, att, bias)


def _wa_body(n_blk, with_ctx, sink_ref, q_ref, k_ref, v_ref, qc_ref, kc_ref, vc_ref, o_ref, oc_ref, kt_ref, kct_ref):
    hd, blk, grp = WA_HD, WINDOW, WA_GROUP
    rows = lax.broadcasted_iota(jnp.int32, (grp * blk, blk), 0) % blk
    cols = lax.broadcasted_iota(jnp.int32, (grp * blk, blk), 1)
    kt_ref[...] = k_ref[...].astype(F32).T.astype(BF16)
    kct_ref[...] = kc_ref[...].astype(F32).T.astype(BF16)
    for kv in range(WA_KV_HEADS):
        ksl = slice(kv * hd, (kv + 1) * hd)
        kc_t = kct_ref[ksl, :]
        vc = vc_ref[:, ksl]

        def stack(ref, start, size):
            return jnp.concatenate([ref[pl.ds(start, size), (kv * grp + g) * hd:(kv * grp + g + 1) * hd]
                                    for g in range(grp)], axis=0)

        def unstack(o, size):
            return jnp.concatenate([o[g * size:(g + 1) * size] for g in range(grp)], axis=1)

        def sink_col(size):
            return jnp.concatenate([jnp.full((size, 1), sink_ref[kv * grp + g], F32) for g in range(grp)], axis=0)

        sink_q = sink_col(blk)

        def blk_body(n, carry):
            qs = pl.multiple_of(n * blk, blk)
            ps = pl.multiple_of(jnp.maximum(n - 1, 0) * blk, blk)
            ns = pl.multiple_of(jnp.minimum(n + 1, n_blk - 1) * blk, blk)
            q = stack(q_ref, qs, blk)
            s_prev = jnp.where((cols >= rows) & (n > 0), _dot(q, kt_ref[ksl, pl.ds(ps, blk)]), NEG_INF)
            s_self = _dot(q, kt_ref[ksl, pl.ds(qs, blk)])
            s_next = jnp.where((cols <= rows) & (n < n_blk - 1), _dot(q, kt_ref[ksl, pl.ds(ns, blk)]), NEG_INF)
            s_ctx = _dot(q, kc_t)
            o = _softmax_pv([s_prev, s_self, s_next, s_ctx],
                            [v_ref[pl.ds(ps, blk), ksl], v_ref[pl.ds(qs, blk), ksl], v_ref[pl.ds(ns, blk), ksl], vc],
                            sink=sink_q)
            o_ref[pl.ds(qs, blk), kv * grp * hd:(kv + 1) * grp * hd] = unstack(o, blk).astype(o_ref.dtype)
            return carry

        lax.fori_loop(0, n_blk, blk_body, 0)
        n_ctx = qc_ref.shape[0]
        if with_ctx:
            qc = stack(qc_ref, 0, n_ctx)
            oc = _softmax_pv([_dot(qc, kc_t)], [vc], sink=sink_col(n_ctx))
            oc_ref[:, kv * grp * hd:(kv + 1) * grp * hd] = unstack(oc, n_ctx).astype(oc_ref.dtype)
        else:
            oc_ref[:, kv * grp * hd:(kv + 1) * grp * hd] = jnp.zeros((n_ctx, grp * hd), oc_ref.dtype)


def window_attention(qkv, sink, n_batch, seq, ctx_len, with_ctx):
    width = WA_HEADS * WA_HD
    ctx_blk0 = n_batch * seq // ctx_len
    kcol, vcol = width // LANES, width // LANES + 1
    return pl.pallas_call(
        functools.partial(_wa_body, seq // WINDOW, with_ctx),
        out_shape=(jax.ShapeDtypeStruct((n_batch * seq, width), BF16), jax.ShapeDtypeStruct((n_batch * ctx_len, width), BF16)),
        grid=(n_batch,),
        in_specs=[pl.BlockSpec(memory_space=pltpu.SMEM),
                  pl.BlockSpec((seq, width), lambda b: (b, 0)),
                  pl.BlockSpec((seq, LANES), lambda b: (b, kcol)),
                  pl.BlockSpec((seq, LANES), lambda b: (b, vcol)),
                  pl.BlockSpec((ctx_len, width), lambda b: (ctx_blk0 + b, 0)),
                  pl.BlockSpec((ctx_len, LANES), lambda b: (ctx_blk0 + b, kcol)),
                  pl.BlockSpec((ctx_len, LANES), lambda b: (ctx_blk0 + b, vcol))],
        out_specs=(pl.BlockSpec((seq, width), lambda b: (b, 0)), pl.BlockSpec((ctx_len, width), lambda b: (b, 0))),
        scratch_shapes=[pltpu.VMEM((LANES, seq), BF16), pltpu.VMEM((LANES, ctx_len), BF16)],
        compiler_params=_params("parallel"),
        name="window_attention",
    )(sink, qkv, qkv, qkv, qkv, qkv, qkv)


CONV_ROWS = 256
CONV_HALO = 8


def _conv_silu_cols(src_ref, src_cols, w_ref, w_cols, pad_ref, dst_ref, dst_row0, dst_cols):
    n_rows = src_ref.shape[0]
    n_blk = n_rows // CONV_ROWS
    zeros = jnp.zeros((CONV_HALO, LANES), F32)
    pad_ref[0:CONV_HALO, :] = zeros
    pad_ref[CONV_HALO + n_rows:2 * CONV_HALO + n_rows, :] = zeros

    def copy(i, carry):
        r0 = pl.multiple_of(i * CONV_ROWS, CONV_ROWS)
        pad_ref[pl.ds(CONV_HALO + r0, CONV_ROWS), :] = src_ref[pl.ds(r0, CONV_ROWS), src_cols]
        return carry

    lax.fori_loop(0, n_blk, copy, 0)
    first = CONV_HALO - CONV_K // 2

    def conv(i, carry):
        r0 = pl.multiple_of(i * CONV_ROWS, CONV_ROWS)
        win = pad_ref[pl.ds(r0, CONV_ROWS + 2 * CONV_HALO), :]
        acc = win[first:first + CONV_ROWS] * w_ref[0:1, w_cols]
        for j in range(1, CONV_K):
            acc = acc + win[first + j:first + j + CONV_ROWS] * w_ref[j:j + 1, w_cols]
        dst_ref[pl.ds(dst_row0 + r0, CONV_ROWS), dst_cols] = _silu(acc)
        return carry

    lax.fori_loop(0, n_blk, conv, 0)


def _tri(upper):
    r = lax.broadcasted_iota(jnp.int32, (CHUNK, CHUNK), 0)
    c = lax.broadcasted_iota(jnp.int32, (CHUNK, CHUNK), 1)
    return (r <= c) if upper else (r >= c)


def _rows_as_lanes(a):
    return jnp.concatenate([a, jnp.zeros((LANES - CHUNK, LANES), F32)], axis=0).T[:, 0:CHUNK]


def _softplus(x):
    return jnp.maximum(x, 0.0) + jnp.log1p(jnp.exp(-jnp.abs(x)))


NEUMANN_STEPS = 5
DN_GROUP = 4
DN_BATCH = 3

def _dn_body(q_ref, k_ref, v_ref, g_ref, qc_ref, kc_ref, vc_ref, gc_ref, wq_ref, wk_ref, wv_ref, alv_ref, dtv_ref,
             o_ref, oc_ref, pad_ref, qs_ref, ks_ref, vs_ref, gb_ref, osum_ref, qe_ref, kb_ref, dec_ref):
    n_ctx, n_lat = qc_ref.shape[0], q_ref.shape[0]
    n_rows = n_ctx + n_lat
    head = pl.program_id(1)
    full = slice(0, LANES)
    for src_c, src, w_ref, dst in ((qc_ref, q_ref, wq_ref, qs_ref), (kc_ref, k_ref, wk_ref, ks_ref), (vc_ref, v_ref, wv_ref, vs_ref)):
        _conv_silu_cols(src_c, full, w_ref, full, pad_ref, dst, 0, full)
        _conv_silu_cols(src, full, w_ref, full, pad_ref, dst, n_ctx, full)

    def l2(i, carry):
        r = pl.multiple_of(i * CONV_ROWS, CONV_ROWS)
        q = qs_ref[pl.ds(r, CONV_ROWS), :]
        qs_ref[pl.ds(r, CONV_ROWS), :] = q * lax.rsqrt(jnp.sum(q * q, axis=-1, keepdims=True) + EPS) * (DN_DK ** -0.5)
        k = ks_ref[pl.ds(r, CONV_ROWS), :]
        ks_ref[pl.ds(r, CONV_ROWS), :] = k * lax.rsqrt(jnp.sum(k * k, axis=-1, keepdims=True) + EPS)
        return carry

    lax.fori_loop(0, n_rows // CONV_ROWS, l2, 0)

    def gates(raw):
        lane = lax.broadcasted_iota(jnp.int32, raw.shape, 1)
        return jnp.where(lane < 2 * DN_HEADS, jax.nn.sigmoid(raw), alv_ref[...] * _softplus(raw + dtv_ref[...]))

    gb_ref[0:n_ctx, :] = gates(gc_ref[...])
    gb_ref[n_ctx:, :] = gates(g_ref[...])
    osum_ref[...] = jnp.zeros(osum_ref.shape, F32)
    grp = DN_GROUP * CHUNK
    nb = DN_BATCH
    n_chunks = n_rows // CHUNK
    n_ctx_chunks = n_ctx // CHUNK
    lane = lax.broadcasted_iota(jnp.int32, (nb, grp, LANES), 2)
    row_i = lax.broadcasted_iota(jnp.int32, (2 * nb, grp, grp), 1)
    col_i = lax.broadcasted_iota(jnp.int32, (2 * nb, grp, grp), 2)
    rev = lax.broadcasted_iota(jnp.int32, (2 * nb, grp, grp), 0) >= nb
    tri = (rev & (row_i <= col_i)) | (jnp.logical_not(rev) & (row_i >= col_i))
    keep = ((row_i // CHUNK) == (col_i // CHUNK)) & tri
    strict = keep & (row_i != col_i)
    ones = jnp.where(keep, 1.0, 0.0).astype(BF16)

    def bdot(a, b):
        return lax.dot_general(a, b, (((2,), (1,)), ((0,), (0,))), preferred_element_type=F32)

    def bdot_nt(a, b):
        return lax.dot_general(a, b, (((2,), (2,)), ((0,), (0,))), preferred_element_type=F32)

    def bdot_tn(a, b):
        return lax.dot_general(a, b, (((1,), (1,)), ((0,), (0,))), preferred_element_type=F32)

    def both(x):
        return jnp.concatenate([x, x], axis=0)

    def trip(ti, carry):
        r = pl.multiple_of(ti * (nb * grp), nb * grp)
        load = lambda ref: ref[pl.ds(r, nb * grp), :].reshape(nb, grp, LANES)
        q, k, v, gb = load(qs_ref), load(ks_ref), load(vs_ref), load(gb_ref)
        pick = lambda l: jnp.sum(jnp.where(lane == l, gb, 0.0), axis=-1, keepdims=True)
        beta = jnp.concatenate([pick(head), pick(DN_HEADS + head)], axis=0)
        g = jnp.broadcast_to(jnp.concatenate([pick(2 * DN_HEADS + head), pick(3 * DN_HEADS + head)], axis=0), (2 * nb, grp, LANES))
        g_hi = g.astype(BF16)
        g_r = g - g_hi.astype(F32)
        g_mid = g_r.astype(BF16)
        g_lo = (g_r - g_mid.astype(F32)).astype(BF16)
        cum = bdot(ones, g_hi) + bdot(ones, g_mid) + bdot(ones, g_lo)
        cum_row = jnp.stack([cum[b].T[0:1, :] for b in range(2 * nb)], axis=0)
        decay = jnp.where(keep, jnp.exp(jnp.concatenate([cum] * (grp // LANES), axis=2) - cum_row), 0.0)
        q2, k2, v2 = both(q), both(k), both(v)
        k16 = k2.astype(BF16)
        kb = k2 * beta
        a = jnp.where(strict, bdot_nt(kb.astype(BF16), k16) * decay, 0.0)
        attn16 = (bdot_nt(q2.astype(BF16), k16) * decay).astype(BF16)
        e_cum = jnp.exp(cum)
        rhs = jnp.concatenate([v2 * beta, kb * e_cum], axis=2)
        n = -a
        a16 = a.astype(BF16)
        p = bdot(a16, a16)
        for it in range(NEUMANN_STEPS):
            p16 = p.astype(BF16)
            n = n + p + bdot(n.astype(BF16), p16)
            if it + 1 < NEUMANN_STEPS:
                p = bdot(p16, p16)
        sol16 = (rhs + bdot(n.astype(BF16), rhs.astype(BF16))).astype(BF16)
        aw = bdot(attn16, sol16)
        o0 = aw[0:nb, :, 0:DN_DK] + aw[nb:, :, 0:DN_DK]
        osum_ref[pl.ds(r, nb * grp), :] += o0.reshape(nb * grp, DN_DK)
        qe = q2 * e_cum - aw[:, :, DN_DK:]
        qe_ref[0, pl.ds(r, nb * grp), :] = qe[0:nb].reshape(nb * grp, DN_DK)
        qe_ref[1, pl.ds(r, nb * grp), :] = qe[nb:].reshape(nb * grp, DN_DK)
        cpt = nb * DN_GROUP
        cum_c = cum.reshape(2 * cpt, CHUNK, LANES)
        edge = jnp.concatenate([cum_c[0:cpt, CHUNK - 1:CHUNK, :], cum_c[cpt:, 0:1, :]], axis=0)
        kdec16 = (k2.reshape(2 * cpt, CHUNK, LANES) * jnp.exp(edge - cum_c)).astype(BF16)
        bk = bdot_tn(kdec16, sol16.reshape(2 * cpt, CHUNK, 2 * DN_DK))
        dec = jnp.broadcast_to(jnp.exp(edge), (2 * cpt, 8, LANES))
        c0 = ti * cpt
        for d in range(2):
            kb_ref[d, pl.ds(c0, cpt)] = bk[d * cpt:(d + 1) * cpt]
            dec_ref[d, pl.ds(c0, cpt)] = dec[d * cpt:(d + 1) * cpt]
        return carry

    lax.fori_loop(0, n_chunks // (DN_GROUP * nb), trip, 0)

    def step(i, states):
        new = []
        for d in range(2):
            c = i if d == 0 else jnp.where(i < n_ctx_chunks, n_ctx_chunks - 1 - i, n_chunks + n_ctx_chunks - 1 - i)
            r = pl.multiple_of(c * CHUNK, CHUNK)
            s16 = states[d].astype(BF16)
            osum_ref[pl.ds(r, CHUNK), :] += _dot(qe_ref[d, pl.ds(r, CHUNK), :].astype(BF16), s16)
            bk = kb_ref[d, c]
            new.append(states[d] * dec_ref[d, c][0:1, :] - _dot(bk[:, DN_DK:].astype(BF16), s16) + bk[:, 0:DN_DK])
        return tuple(new)

    zero = jnp.zeros((DN_DK, DN_DK), F32)
    lax.fori_loop(0, n_chunks, step, (zero, zero))
    oc_ref[...] = osum_ref[0:n_ctx, :]
    o_ref[...] = osum_ref[n_ctx:, :]


def dn_gate_vectors(a_log, dt_bias):
    def vec(v):
        return jnp.pad(v.astype(F32).reshape(1, -1), ((0, 0), (2 * DN_HEADS, LANES - 4 * DN_HEADS)))
    return vec(-jnp.exp(a_log.astype(F32))), vec(dt_bias)


def deltanet(dqkv, dg, conv_w, alv, dtv, n_batch, seq, ctx_len):
    nh = DN_HEADS
    ctx_blk0 = n_batch * seq // ctx_len
    n_rows = seq + ctx_len
    lat = lambda off: pl.BlockSpec((seq, LANES), lambda b, h: (b, off + h))
    ctx = lambda off: pl.BlockSpec((ctx_len, LANES), lambda b, h: (ctx_blk0 + b, off + h))
    cw = lambda off: pl.BlockSpec((CONV_K, LANES), lambda b, h: (0, off + h))
    vec = pl.BlockSpec((1, LANES), lambda b, h: (0, 0))
    seq_buf = pltpu.VMEM((n_rows, LANES), F32)
    return pl.pallas_call(
        _dn_body,
        out_shape=(jax.ShapeDtypeStruct((n_batch * seq, nh * DN_DK), F32), jax.ShapeDtypeStruct((n_batch * ctx_len, nh * DN_DK), F32)),
        grid=(n_batch, nh),
        in_specs=[lat(0), lat(nh), lat(2 * nh), pl.BlockSpec((seq, LANES), lambda b, h: (b, 0)),
                  ctx(0), ctx(nh), ctx(2 * nh), pl.BlockSpec((ctx_len, LANES), lambda b, h: (ctx_blk0 + b, 0)),
                  cw(0), cw(nh), cw(2 * nh), vec, vec],
        out_specs=(pl.BlockSpec((seq, LANES), lambda b, h: (b, h)), pl.BlockSpec((ctx_len, LANES), lambda b, h: (b, h))),
        scratch_shapes=[pltpu.VMEM((seq + 2 * CONV_HALO, LANES), F32), seq_buf, seq_buf, seq_buf, seq_buf, seq_buf,
                        pltpu.VMEM((2, n_rows, LANES), F32),
                        pltpu.VMEM((2, n_rows // CHUNK, DN_DK, 2 * DN_DK), F32),
                        pltpu.VMEM((2, n_rows // CHUNK, 8, LANES), F32)],
        compiler_params=_params("parallel", "arbitrary"),
        name="deltanet",
    )(dqkv, dqkv, dqkv, dg, dqkv, dqkv, dqkv, dg, conv_w, conv_w, conv_w, alv, dtv)


def _ssd_body(xbc_ref, dt_ref, xbc_c_ref, dt_c_ref, cw_ref, av_ref, dtb_ref, dsk_ref, y_ref, yc_ref,
              pad_ref, s_ref, dts_ref, da_ref, hs_ref):
    n_ctx, n_lat = xbc_c_ref.shape[0], xbc_ref.shape[0]
    inner = SSD_HEADS * SSD_HD
    hpg = SSD_HEADS // SSD_GROUPS
    gw = hpg * SSD_HD
    for cb in range(xbc_ref.shape[1] // LANES):
        cols = slice(cb * LANES, (cb + 1) * LANES)
        _conv_silu_cols(xbc_c_ref, cols, cw_ref, cols, pad_ref, s_ref, 0, cols)
        _conv_silu_cols(xbc_ref, cols, cw_ref, cols, pad_ref, s_ref, n_ctx, cols)
    dts_ref[0:n_ctx, :] = _softplus(dt_c_ref[...] + dtb_ref[...])
    dts_ref[n_ctx:, :] = _softplus(dt_ref[...] + dtb_ref[...])
    da_ref[...] = dts_ref[...] * av_ref[...]
    yc_ref[...] = s_ref[0:n_ctx, 0:inner] * dsk_ref[...]
    y_ref[...] = s_ref[n_ctx:, 0:inner] * dsk_ref[...]
    hs_ref[...] = jnp.zeros(hs_ref.shape, F32)

    def chunk(d, c, row0, out_ref):
        r = pl.multiple_of(row0 + c * CHUNK, CHUNK)
        ro = pl.multiple_of(c * CHUNK, CHUNK)
        keep = _tri(upper=(d == 1))
        acum = jnp.dot(keep.astype(F32), da_ref[pl.ds(r, CHUNK), :], precision=HIGHEST, preferred_element_type=F32)
        acum_t = _rows_as_lanes(acum)
        edge = acum[CHUNK - 1:CHUNK, :] if d == 0 else acum[0:1, :]
        dt = dts_ref[pl.ds(r, CHUNK), :]
        for g in range(SSD_GROUPS):
            lanes = [d * SSD_HEADS + g * hpg + h for h in range(hpg)]

            def expand(v, rows):
                return jnp.concatenate([jnp.broadcast_to(v[:, l:l + 1], (rows, SSD_HD)) for l in lanes], axis=1)

            x4 = s_ref[pl.ds(r, CHUNK), g * gw:(g + 1) * gw]
            bm = s_ref[pl.ds(r, CHUNK), inner + g * SSD_STATE:inner + (g + 1) * SSD_STATE].astype(BF16)
            cm = s_ref[pl.ds(r, CHUNK), inner + (SSD_GROUPS + g) * SSD_STATE:inner + (SSD_GROUPS + g + 1) * SSD_STATE].astype(BF16)
            cb_mat = _dot_nt(cm, bm)
            a4 = expand(acum, CHUNK)
            e4 = expand(edge, 1)
            xd = x4 * expand(dt, CHUNK)
            prev = hs_ref[d, g]
            y = _dot_nt(cm, prev.astype(BF16)) * jnp.exp(a4)
            diag = []
            for h, l in enumerate(lanes):
                lmat = jnp.where(keep, jnp.exp(acum[:, l:l + 1] - acum_t[l:l + 1, :]), 0.0)
                diag.append(_dot((cb_mat * lmat).astype(BF16), xd[:, h * SSD_HD:(h + 1) * SSD_HD].astype(BF16)))
            y = y + jnp.concatenate(diag, axis=1)
            out_ref[pl.ds(ro, CHUNK), g * gw:(g + 1) * gw] += y
            states = _dot_tn((xd * jnp.exp(e4 - a4)).astype(BF16), bm)
            decay = jnp.concatenate([jnp.broadcast_to(jnp.exp(edge[:, l:l + 1]), (SSD_HD, SSD_STATE)) for l in lanes], axis=0)
            hs_ref[d, g] = prev * decay + states

    def segment(n_chunks, row0, out_ref):
        def body(i, carry):
            chunk(0, i, row0, out_ref)
            chunk(1, n_chunks - 1 - i, row0, out_ref)
            return carry
        lax.fori_loop(0, n_chunks, body, 0)

    segment(n_ctx // CHUNK, 0, yc_ref)
    segment(n_lat // CHUNK, n_ctx, y_ref)


def ssd(xbc, dt, conv_w, a_vec, dtb_vec, skip_vec, n_batch, seq, ctx_len):
    width = xbc.shape[1]
    inner = SSD_HEADS * SSD_HD
    ctx_blk0 = n_batch * seq // ctx_len
    n_rows = seq + ctx_len
    return pl.pallas_call(
        _ssd_body,
        out_shape=(jax.ShapeDtypeStruct((n_batch * seq, inner), F32), jax.ShapeDtypeStruct((n_batch * ctx_len, inner), F32)),
        grid=(n_batch,),
        in_specs=[pl.BlockSpec((seq, width), lambda b: (b, 0)), pl.BlockSpec((seq, LANES), lambda b: (b, 0)),
                  pl.BlockSpec((ctx_len, width), lambda b: (ctx_blk0 + b, 0)), pl.BlockSpec((ctx_len, LANES), lambda b: (ctx_blk0 + b, 0)),
                  _resident(conv_w.shape), _resident(a_vec.shape), _resident(dtb_vec.shape), _resident(skip_vec.shape)],
        out_specs=(pl.BlockSpec((seq, inner), lambda b: (b, 0)), pl.BlockSpec((ctx_len, inner), lambda b: (b, 0))),
        scratch_shapes=[pltpu.VMEM((seq + 2 * CONV_HALO, LANES), F32), pltpu.VMEM((n_rows, width), F32),
                        pltpu.VMEM((n_rows, LANES), F32), pltpu.VMEM((n_rows, LANES), F32),
                        pltpu.VMEM((2, SSD_GROUPS, (SSD_HEADS // SSD_GROUPS) * SSD_HD, SSD_STATE), F32)],
        compiler_params=_params("parallel"),
        name="ssd",
    )(xbc, dt, xbc, dt, conv_w, a_vec, dtb_vec, skip_vec)


def ssd_vectors(a_log, dt_bias, d_skip):
    def vec(v):
        return jnp.pad(v.astype(F32).reshape(1, -1), ((0, 0), (0, LANES - 2 * SSD_HEADS)))
    return vec(-jnp.exp(a_log.astype(F32))), vec(dt_bias), jnp.repeat(d_skip.astype(F32), SSD_HD)[None, :]


def kernel(x, c, ctx, c_ctx, ada_w, ada_b, norm_w, ev_w_in, ev_w_out, na_rpb, dn_conv, dn_a_log, dn_dt_bias, dn_norm, ffn_w1, ffn_w3, ffn_w2, od_w_in, od_w_out, wa_sink, ssd_conv, ssd_a_log, ssd_dt_bias, ssd_d, ssd_norm, moe_router, moe_w1, moe_w3, moe_w2):
    n_batch, seq, d = x.shape
    ctx_len = ctx.shape[1]
    depth = ada_w.shape[0]
    n_lat, n_ctx = n_batch * seq, n_batch * ctx_len
    n_cond = -(-(n_batch + 1) // 8) * 8
    cond = jnp.concatenate([c, c_ctx[None, :], jnp.zeros((n_cond - n_batch - 1, d), F32)], axis=0)
    mod_all = ada_modulation(cond, ada_w, ada_b)
    x_all = jnp.concatenate([x.reshape(n_lat, d), ctx.reshape(n_ctx, d)], axis=0)
    cos_t, sin_t = rope_tables(seq)
    for layer in range(depth):
        i = layer // 2
        n_tok = n_lat + n_ctx if layer < depth - 1 else n_lat
        mod = mod_all[layer, :n_batch + 1].reshape(n_batch + 1, 6, d)
        nw = norm_w[layer]
        if layer % 2 == 0:
            att, dqkv, dz, dg = proj_in_even(x_all, mod, nw, prep_w_in_even(ev_w_in[i]), n_batch, seq)
            o_a, oc_a = na_attention(att, na_bias_tables(na_rpb[i], seq // GRID_W), n_batch, seq, ctx_len)
            alv, dtv = dn_gate_vectors(dn_a_log[i], dn_dt_bias[i])
            o_b, oc_b = deltanet(dqkv, dg, dn_conv[i], alv, dtv, n_batch, seq, ctx_len)
            x_all = proj_out(jnp.concatenate([o_a, oc_a], axis=0), jnp.concatenate([o_b, oc_b], axis=0), dz,
                             jnp.tile(dn_norm[i], DN_HEADS)[None, :], ev_w_out[i].astype(BF16), x_all, mod, nw,
                             n_batch, seq, n_tok, DN_DK, False)
            x_all = ffn_dense(x_all, mod, nw, ffn_w1[i].astype(BF16), ffn_w3[i].astype(BF16), ffn_w2[i].astype(BF16),
                              n_batch, seq, n_tok)
        else:
            qkv, z, xbc, dt = proj_in_odd(x_all, mod, nw, prep_w_in_odd(od_w_in[i]), cos_t, sin_t, n_batch, seq)
            o_c, oc_c = window_attention(qkv, wa_sink[i], n_batch, seq, ctx_len, n_tok > n_lat)
            a_vec, dtb_vec, skip_vec = ssd_vectors(ssd_a_log[i], ssd_dt_bias[i], ssd_d[i])
            y_d, yc_d = ssd(xbc, dt, ssd_conv[i], a_vec, dtb_vec, skip_vec, n_batch, seq, ctx_len)
            x_all = proj_out(jnp.concatenate([o_c, oc_c], axis=0), jnp.concatenate([y_d, yc_d], axis=0), z,
                             ssd_norm[i][None, :], od_w_out[i].astype(BF16), x_all, mod, nw,
                             n_batch, seq, n_tok, SSD_HEADS * SSD_HD // SSD_GROUPS, True)
            router = jnp.pad(moe_router[i], ((0, 0), (0, LANES - N_EXPERTS)))
            x_all = moe_layer(x_all, mod, nw, router, moe_w1[i].astype(BF16), moe_w3[i].astype(BF16),
                              moe_w2[i].astype(BF16), n_batch, seq, n_tok)
    return x_all[:n_lat].reshape(n_batch, seq, d)
```

```python
import functools
import math

import jax
import jax.numpy as jnp
import numpy as np
from jax import lax
from jax.experimental import pallas as pl
from jax.experimental.pallas import tpu as pltpu

F32 = jnp.float32
BF16 = jnp.bfloat16
HIGHEST = lax.Precision.HIGHEST

EPS = 1e-6
NEG_INF = -1e30
GRID_W = 64
CHUNK = 64
CONV_K = 5
ROPE_BASE = 10000.0
NA_HEADS, NA_HD, NA_WIN_ROWS, NA_WIN_COLS = 8, 64, 8, 16
DN_HEADS, DN_DK = 4, 128
WA_HEADS, WA_KV_HEADS, WA_HD, WINDOW = 8, 2, 64, 128
WA_GROUP = WA_HEADS // WA_KV_HEADS
SSD_HEADS, SSD_HD, SSD_GROUPS, SSD_STATE = 8, 64, 2, 128
N_EXPERTS, TOP_K = 8, 2

LANES = 128
VMEM_LIMIT = 56 * 1024 * 1024
TOK_TILE = 512


def _params(*sem):
    return pltpu.CompilerParams(dimension_semantics=sem, vmem_limit_bytes=VMEM_LIMIT)


def _resident(shape):
    nd = len(shape)
    return pl.BlockSpec(shape, lambda *_: (0,) * nd, pipeline_mode=pl.Buffered(1))


def _silu(x):
    return x * jax.nn.sigmoid(x)


def _rms(x, w):
    return x * lax.rsqrt(jnp.mean(x * x, axis=-1, keepdims=True) + EPS) * w


def _dot(a, b):
    return jnp.dot(a, b, preferred_element_type=F32)


def _dot_nt(a, b):
    return lax.dot_general(a, b, (((1,), (1,)), ((), ())), preferred_element_type=F32)


def _dot_tn(a, b):
    return lax.dot_general(a, b, (((0,), (0,)), ((), ())), preferred_element_type=F32)


def _ada_body(s_ref, w_ref, b_ref, o_ref):
    s = _silu(s_ref[...])
    o_ref[0] = jnp.dot(s, w_ref[0], precision=HIGHEST, preferred_element_type=F32) + b_ref[0]


def ada_modulation(cond, ada_w, ada_b):
    depth, d, n = ada_w.shape
    r = cond.shape[0]
    tn = 1536
    return pl.pallas_call(
        _ada_body,
        out_shape=jax.ShapeDtypeStruct((depth, r, n), F32),
        grid=(depth, n // tn),
        in_specs=[
            pl.BlockSpec((r, d), lambda l, j: (0, 0)),
            pl.BlockSpec((1, d, tn), lambda l, j: (l, 0, j)),
            pl.BlockSpec((1, 1, tn), lambda l, j: (l, 0, j)),
        ],
        out_specs=pl.BlockSpec((1, r, tn), lambda l, j: (l, 0, j)),
        compiler_params=_params("arbitrary", "arbitrary"),
        name="ada_modulation",
    )(cond, ada_w, ada_b.reshape(depth, 1, n))


def _mod_rows(n_lat_tiles, tiles_per_batch, n_batch):
    def index(i):
        return (jnp.minimum(i // tiles_per_batch, n_batch), 0, 0)
    del n_lat_tiles
    return index


def _norm_mod(x, mod_ref, nw_ref, row):
    y = _rms(x, nw_ref[row:row + 1, :])
    return y * (1.0 + mod_ref[0, row + 1:row + 2, :]) + mod_ref[0, row:row + 1, :]


def _proj_even_body(x_ref, mod_ref, nw_ref, w_ref, att_ref, dqkv_ref, dz_ref, dg_ref):
    hb = _norm_mod(x_ref[...], mod_ref, nw_ref, 0).astype(BF16)
    att_ref[...] = _dot(hb, w_ref[:, 0:1536]).astype(BF16)
    dqkv_ref[...] = _dot(hb, w_ref[:, 1536:3072])
    dz_ref[...] = _dot(hb, w_ref[:, 3072:3584])
    dg_ref[...] = _dot(hb, w_ref[:, 3584:3712])


def _swap16(x):
    lane = lax.broadcasted_iota(jnp.int32, x.shape, 1)
    up = pltpu.roll(x, LANES - 16, 1)
    down = pltpu.roll(x, 16, 1)
    return jnp.where((lane % 32) < 16, up, down)


def _proj_odd_body(x_ref, mod_ref, nw_ref, w_ref, cos_ref, sin_ref, qkv_ref, z_ref, xbc_ref, dt_ref):
    hb = _norm_mod(x_ref[...], mod_ref, nw_ref, 0).astype(BF16)
    cos, sin = cos_ref[...], sin_ref[...]
    for c in range(5):
        t = _dot(hb, w_ref[:, c * LANES:(c + 1) * LANES])
        qkv_ref[:, c * LANES:(c + 1) * LANES] = (t * cos + _swap16(t) * sin).astype(BF16)
    qkv_ref[:, 640:768] = _dot(hb, w_ref[:, 640:768]).astype(BF16)
    z_ref[...] = _dot(hb, w_ref[:, 768:1280])
    xbc_ref[...] = _dot(hb, w_ref[:, 1280:2304])
    dt_ref[...] = _dot(hb, w_ref[:, 2304:2432])


def _pad_cols(w, n):
    return jnp.pad(w, ((0, 0), (0, n - w.shape[1])))


def prep_w_in_even(w):
    q_scale = jnp.concatenate([jnp.full((NA_HEADS * NA_HD,), NA_HD ** -0.5, F32), jnp.ones((w.shape[1] - NA_HEADS * NA_HD,), F32)])
    return _pad_cols(w * q_scale[None, :], 3712).astype(BF16)


def prep_w_in_odd(w):
    q_scale = jnp.concatenate([jnp.full((WA_HEADS * WA_HD,), WA_HD ** -0.5, F32), jnp.ones((w.shape[1] - WA_HEADS * WA_HD,), F32)])
    return _pad_cols(w * q_scale[None, :], 2432).astype(BF16)


def _tok_spec(width, tm=TOK_TILE):
    return pl.BlockSpec((tm, width), lambda i: (i, 0))


def proj_in_even(x_all, mod, nw, w, n_batch, seq):
    t_all, d = x_all.shape
    tm = TOK_TILE
    tpb = seq // tm
    return pl.pallas_call(
        _proj_even_body,
        out_shape=(jax.ShapeDtypeStruct((t_all, 1536), BF16), jax.ShapeDtypeStruct((t_all, 1536), F32),
                   jax.ShapeDtypeStruct((t_all, 512), F32), jax.ShapeDtypeStruct((t_all, LANES), F32)),
        grid=(t_all // tm,),
        in_specs=[_tok_spec(d), pl.BlockSpec((1, 6, d), _mod_rows(None, tpb, n_batch)),
                  _resident(nw.shape), _resident(w.shape)],
        out_specs=(_tok_spec(1536), _tok_spec(1536), _tok_spec(512), _tok_spec(LANES)),
        compiler_params=_params("parallel"),
        name="proj_in_even",
    )(x_all, mod, nw, w)


def proj_in_odd(x_all, mod, nw, w, cos_t, sin_t, n_batch, seq):
    t_all, d = x_all.shape
    tm = TOK_TILE
    tpb = seq // tm

    def rope_rows(i):
        return (jnp.where(i < n_batch * tpb, i % tpb, tpb), 0)

    return pl.pallas_call(
        _proj_odd_body,
        out_shape=(jax.ShapeDtypeStruct((t_all, 768), BF16), jax.ShapeDtypeStruct((t_all, 512), F32),
                   jax.ShapeDtypeStruct((t_all, 1024), F32), jax.ShapeDtypeStruct((t_all, LANES), F32)),
        grid=(t_all // tm,),
        in_specs=[_tok_spec(d), pl.BlockSpec((1, 6, d), _mod_rows(None, tpb, n_batch)),
                  _resident(nw.shape), _resident(w.shape),
                  pl.BlockSpec((tm, LANES), rope_rows), pl.BlockSpec((tm, LANES), rope_rows)],
        out_specs=(_tok_spec(768), _tok_spec(512), _tok_spec(1024), _tok_spec(LANES)),
        compiler_params=_params("parallel"),
        name="proj_in_odd",
    )(x_all, mod, nw, w, cos_t, sin_t)


def rope_tables(seq, tm=TOK_TILE):
    half = WA_HD // 4
    freqs = ROPE_BASE ** (-jnp.arange(half, dtype=F32) / half)
    t = jnp.arange(seq)
    ang_r = (t // GRID_W).astype(F32)[:, None] * freqs[None, :]
    ang_c = (t % GRID_W).astype(F32)[:, None] * freqs[None, :]
    cos = jnp.concatenate([jnp.cos(ang_r)] * 2 + [jnp.cos(ang_c)] * 2, axis=-1)
    sin = jnp.concatenate([-jnp.sin(ang_r), jnp.sin(ang_r), -jnp.sin(ang_c), jnp.sin(ang_c)], axis=-1)
    cos = jnp.concatenate([jnp.tile(cos, (1, 2)), jnp.ones((tm, LANES), F32)], axis=0)
    sin = jnp.concatenate([jnp.tile(sin, (1, 2)), jnp.zeros((tm, LANES), F32)], axis=0)
    return cos, sin


def _proj_out_body(group, gate_first, n_lat_tiles, a_ref, ac_ref, r_ref, rc_ref, z_ref, gw_ref, w_ref, x_ref, mod_ref,
                   nw_ref, o_ref):
    is_ctx = pl.program_id(0) >= n_lat_tiles
    a = jnp.where(is_ctx, ac_ref[...], a_ref[...])
    r = jnp.where(is_ctx, rc_ref[...], r_ref[...])
    gate = _silu(z_ref[...])
    if gate_first:
        r = r * gate
    parts = []
    for c in range(r.shape[1] // group):
        rc = r[:, c * group:(c + 1) * group]
        parts.append(_rms(rc, gw_ref[:, c * group:(c + 1) * group]))
    r = jnp.concatenate(parts, axis=1)
    if not gate_first:
        r = r * gate
    half = a.shape[1]
    y = _dot(a, w_ref[0:half, :]) + _dot(r.astype(BF16), w_ref[half:, :])
    o_ref[...] = x_ref[...] + mod_ref[0, 2:3, :] * _rms(y, nw_ref[1:2, :])


def proj_out(att, att_c, rec, rec_c, z, gate_w, w_out, x_all, mod, nw, n_batch, seq, n_tok, group, gate_first):
    d = x_all.shape[1]
    tm = TOK_TILE
    tpb = seq // tm
    half = att.shape[1]
    n_lat_tiles = att.shape[0] // tm
    lat = pl.BlockSpec((tm, half), lambda i: (jnp.minimum(i, n_lat_tiles - 1), 0))
    ctx = pl.BlockSpec((tm, half), lambda i: (jnp.maximum(i - n_lat_tiles, 0), 0))
    return pl.pallas_call(
        functools.partial(_proj_out_body, group, gate_first, n_lat_tiles),
        out_shape=jax.ShapeDtypeStruct((n_tok, d), F32),
        grid=(n_tok // tm,),
        in_specs=[lat, ctx, lat, ctx, _tok_spec(half), _resident(gate_w.shape),
                  _resident(w_out.shape), _tok_spec(d), pl.BlockSpec((1, 6, d), _mod_rows(None, tpb, n_batch)),
                  _resident(nw.shape)],
        out_specs=_tok_spec(d),
        compiler_params=_params("parallel"),
        name="proj_out",
    )(att, att_c, rec, rec_c, z, gate_w, w_out, x_all, mod, nw)


def _ffn_dense_body(n_split, x_ref, mod_ref, nw_ref, w1_ref, w3_ref, w2_ref, o_ref):
    x = x_ref[...]
    y = _rms(x, nw_ref[2:3, :])
    hb = (y * (1.0 + mod_ref[0, 4:5, :]) + mod_ref[0, 3:4, :]).astype(BF16)
    ff = w1_ref.shape[1]
    step = ff // n_split
    acc = None
    for c in range(n_split):
        sl = slice(c * step, (c + 1) * step)
        act = (_silu(_dot(hb, w1_ref[:, sl])) * _dot(hb, w3_ref[:, sl])).astype(BF16)
        part = _dot(act, w2_ref[sl, :])
        acc = part if acc is None else acc + part
    o_ref[...] = x + mod_ref[0, 5:6, :] * _rms(acc, nw_ref[3:4, :])


def ffn_dense(x_all, mod, nw, w1, w3, w2, n_batch, seq, n_tok):
    d = x_all.shape[1]
    tm = TOK_TILE
    tpb = seq // tm
    return pl.pallas_call(
        functools.partial(_ffn_dense_body, 2),
        out_shape=jax.ShapeDtypeStruct((n_tok, d), F32),
        grid=(n_tok // tm,),
        in_specs=[_tok_spec(d), pl.BlockSpec((1, 6, d), _mod_rows(None, tpb, n_batch)), _resident(nw.shape),
                  _resident(w1.shape), _resident(w3.shape), _resident(w2.shape)],
        out_specs=_tok_spec(d),
        compiler_params=_params("parallel"),
        name="ffn_dense",
    )(x_all, mod, nw, w1, w3, w2)


MOE_TILE = 1024
MOE_FF_TILE = 896


def _moe_route_body(x_ref, mod_ref, nw_ref, r_ref, h_ref, idx_ref, gate_ref):
    y = _rms(x_ref[...], nw_ref[2:3, :])
    h = y * (1.0 + mod_ref[0, 4:5, :]) + mod_ref[0, 3:4, :]
    h_ref[...] = h
    logits = jnp.dot(h, r_ref[...], precision=HIGHEST, preferred_element_type=F32)
    lane = lax.broadcasted_iota(jnp.int32, logits.shape, 1)
    logits = jnp.where(lane < N_EXPERTS, logits, -jnp.inf)
    v1 = jnp.max(logits, axis=-1, keepdims=True)
    i1 = jnp.min(jnp.where(logits == v1, lane, LANES), axis=-1, keepdims=True)
    rest = jnp.where(lane == i1, -jnp.inf, logits)
    v2 = jnp.max(rest, axis=-1, keepdims=True)
    i2 = jnp.min(jnp.where(rest == v2, lane, LANES), axis=-1, keepdims=True)
    e2 = jnp.exp(v2 - v1)
    g1 = 1.0 / (1.0 + e2)
    idx_ref[...] = jnp.where(lane == 0, i1, jnp.where(lane == 1, i2, 0))
    gate_ref[...] = jnp.where(lane == 0, g1, jnp.where(lane == 1, e2 * g1, 0.0))


def moe_route(x_all, mod, nw, router, n_batch, seq, n_tok):
    d = x_all.shape[1]
    tm = TOK_TILE
    tpb = seq // tm
    return pl.pallas_call(
        _moe_route_body,
        out_shape=(jax.ShapeDtypeStruct((n_tok, d), F32), jax.ShapeDtypeStruct((n_tok, LANES), jnp.int32),
                   jax.ShapeDtypeStruct((n_tok, LANES), F32)),
        grid=(n_tok // tm,),
        in_specs=[_tok_spec(d), pl.BlockSpec((1, 6, d), _mod_rows(None, tpb, n_batch)), _resident(nw.shape),
                  _resident(router.shape)],
        out_specs=(_tok_spec(d), _tok_spec(LANES), _tok_spec(LANES)),
        compiler_params=_params("parallel"),
        name="moe_route",
    )(x_all, mod, nw, router)


def _moe_gmm_body(te_ref, nv_ref, src_ref, src_next_ref, h_ref, w1_ref, w3_ref, w2_ref, o_ref, xbuf, x16, sem):
    i, f = pl.program_id(0), pl.program_id(1)
    n_live = nv_ref[0]
    tm = x16.shape[0]
    slot = i % 2

    def row_copy(s_ref, s, r):
        return pltpu.make_async_copy(h_ref.at[pl.ds(s_ref[0, 0, r], 1), :], xbuf.at[s, pl.ds(r, 1), :], sem.at[s])

    def for_rows(fn):
        def body(r, carry):
            fn(r)
            return carry
        lax.fori_loop(0, tm, body, 0, unroll=8)

    @pl.when((f == 0) & (i == 0) & (n_live > 0))
    def _():
        for_rows(lambda r: row_copy(src_ref, 0, r).start())

    @pl.when((f == 0) & (i < n_live))
    def _():
        for_rows(lambda r: row_copy(src_ref, slot, r).wait())
        x16[...] = xbuf[slot].astype(BF16)

    @pl.when((f == 1) & (i + 1 < n_live))
    def _():
        for_rows(lambda r: row_copy(src_next_ref, 1 - slot, r).start())

    @pl.when(i < n_live)
    def _():
        x = x16[...]
        act = (_silu(_dot(x, w1_ref[0, 0])) * _dot(x, w3_ref[0, 0])).astype(BF16)
        part = _dot(act, w2_ref[0, 0])

        @pl.when(f == 0)
        def _():
            o_ref[...] = part

        @pl.when(f > 0)
        def _():
            o_ref[...] += part

    @pl.when((i >= n_live) & (f == 0))
    def _():
        o_ref[...] = jnp.zeros(o_ref.shape, o_ref.dtype)


def moe_gmm(tile_expert, n_valid, src_tok, h, w1, w3, w2, layer):
    p = src_tok.shape[0]
    d = h.shape[1]
    ff = w1.shape[3]
    tm, tf = MOE_TILE, MOE_FF_TILE
    nt, nf = p // tm, ff // tf
    assert nf >= 2

    def wcol(i, f, te, nv):
        live = i < nv[0]
        return (layer, te[i], 0, jnp.where(live, f, nf - 1))

    def wrow(i, f, te, nv):
        live = i < nv[0]
        return (layer, te[i], jnp.where(live, f, nf - 1), 0)

    src_tiles = src_tok.reshape(nt, 1, tm)
    smem = lambda index: pl.BlockSpec((1, 1, tm), index, memory_space=pltpu.SMEM)
    return pl.pallas_call(
        _moe_gmm_body,
        out_shape=jax.ShapeDtypeStruct((p, d), F32),
        grid_spec=pltpu.PrefetchScalarGridSpec(
            num_scalar_prefetch=2,
            grid=(nt, nf),
            in_specs=[smem(lambda i, f, te, nv: (i, 0, 0)), smem(lambda i, f, te, nv: (jnp.minimum(i + 1, nt - 1), 0, 0)),
                      pl.BlockSpec(memory_space=pl.ANY),
                      pl.BlockSpec((1, 1, d, tf), wcol), pl.BlockSpec((1, 1, d, tf), wcol), pl.BlockSpec((1, 1, tf, d), wrow)],
            out_specs=pl.BlockSpec((tm, d), lambda i, f, te, nv: (i, 0)),
            scratch_shapes=[pltpu.VMEM((2, tm, d), F32), pltpu.VMEM((tm, d), BF16), pltpu.SemaphoreType.DMA((2,))]),
        compiler_params=_params("arbitrary", "arbitrary"),
        name="moe_gmm",
    )(tile_expert, n_valid, src_tiles, src_tiles, h, w1, w3, w2)


def moe_dispatch(idx, n_tok):
    tm = MOE_TILE
    n_assign = n_tok * TOP_K
    p = n_assign + N_EXPERTS * tm
    e_flat = idx[:, :TOP_K].reshape(n_assign)
    onehot = (e_flat[:, None] == jnp.arange(N_EXPERTS)[None, :]).astype(jnp.int32)
    counts = jnp.sum(onehot, axis=0)
    padded = ((counts + tm - 1) // tm) * tm
    ends = jnp.cumsum(padded)
    starts = ends - padded
    pos = jnp.sum(onehot * (jnp.cumsum(onehot, axis=0) - onehot + starts[None, :]), axis=1)
    src_tok = jnp.zeros((p,), jnp.int32).at[pos].set(jnp.arange(n_assign, dtype=jnp.int32) // TOP_K)
    tile_start = jnp.arange(p // tm, dtype=jnp.int32) * tm
    tile_expert = jnp.minimum(jnp.sum((tile_start[:, None] >= ends[None, :]).astype(jnp.int32), axis=1), N_EXPERTS - 1)
    n_valid = (ends[-1] // tm).astype(jnp.int32).reshape(1)
    last_live = jnp.take(tile_expert, jnp.maximum(n_valid - 1, 0))
    tile_expert = jnp.where(tile_start < ends[-1], tile_expert, last_live).astype(jnp.int32)
    return src_tok, pos.reshape(n_tok, TOP_K), tile_expert, n_valid


def _moe_combine_body(pos_ref, pos_next_ref, ys_ref, gate_ref, x_ref, mod_ref, nw_ref, o_ref, ybuf, sem):
    i = pl.program_id(0)
    n_tiles = pl.num_programs(0)
    tm = x_ref.shape[0]
    slot = i % 2

    def row_copy(p_ref, s, k, r):
        return pltpu.make_async_copy(ys_ref.at[pl.ds(p_ref[0, 0, k * tm + r], 1), :], ybuf.at[s, k, pl.ds(r, 1), :], sem.at[s])

    def for_rows(fn):
        def body(r, carry):
            fn(r)
            return carry
        lax.fori_loop(0, tm, body, 0, unroll=8)

    def start_tile(p_ref, s):
        def fn(r):
            row_copy(p_ref, s, 0, r).start()
            row_copy(p_ref, s, 1, r).start()
        for_rows(fn)

    @pl.when(i == 0)
    def _():
        start_tile(pos_ref, 0)

    @pl.when(i + 1 < n_tiles)
    def _():
        start_tile(pos_next_ref, 1 - slot)

    def wait_row(r):
        row_copy(pos_ref, slot, 0, r).wait()
        row_copy(pos_ref, slot, 1, r).wait()
    for_rows(wait_row)

    g = gate_ref[...]
    ffn = ybuf[slot, 0] * g[:, 0:1] + ybuf[slot, 1] * g[:, 1:2]
    o_ref[...] = x_ref[...] + mod_ref[0, 5:6, :] * _rms(ffn, nw_ref[3:4, :])


def moe_combine(ys, pos, gates, x_all, mod, nw, n_batch, seq, n_tok):
    d = x_all.shape[1]
    tm = TOK_TILE
    tpb = seq // tm
    n_tiles = n_tok // tm
    pos_tiles = pos.reshape(n_tiles, tm, TOP_K).transpose(0, 2, 1).reshape(n_tiles, 1, TOP_K * tm)
    smem = lambda index: pl.BlockSpec((1, 1, TOP_K * tm), index, memory_space=pltpu.SMEM)
    return pl.pallas_call(
        _moe_combine_body,
        out_shape=jax.ShapeDtypeStruct((n_tok, d), F32),
        grid=(n_tiles,),
        in_specs=[smem(lambda i: (i, 0, 0)), smem(lambda i: (jnp.minimum(i + 1, n_tiles - 1), 0, 0)),
                  pl.BlockSpec(memory_space=pl.ANY), _tok_spec(LANES), _tok_spec(d),
                  pl.BlockSpec((1, 6, d), _mod_rows(None, tpb, n_batch)), _resident(nw.shape)],
        out_specs=_tok_spec(d),
        scratch_shapes=[pltpu.VMEM((2, TOP_K, tm, d), F32), pltpu.SemaphoreType.DMA((2,))],
        compiler_params=_params("arbitrary"),
        name="moe_combine",
    )(pos_tiles, pos_tiles, ys, gates, x_all, mod, nw)


def moe_layer(x_all, mod, nw, router, w1, w3, w2, layer, n_batch, seq, n_tok):
    h, idx, gates = moe_route(x_all, mod, nw, router, n_batch, seq, n_tok)
    src_tok, pos, tile_expert, n_valid = moe_dispatch(idx, n_tok)
    ys = moe_gmm(tile_expert, n_valid, src_tok, h, w1, w3, w2, layer)
    return moe_combine(ys, pos, gates, x_all, mod, nw, n_batch, seq, n_tok)


NA_QROWS = 4
NA_KROWS = NA_QROWS + NA_WIN_ROWS - 1


def _na_key_start(blk, rows):
    return jnp.clip(blk * NA_QROWS - NA_WIN_ROWS // 2, 0, rows - NA_KROWS)


def na_bias_tables(rpb, rows):
    n_blk = rows // NA_QROWS
    blk = jnp.array([0, 1, n_blk - 1])
    q_row = blk[:, None] * NA_QROWS + jnp.arange(NA_QROWS)[None, :]
    k_row = _na_key_start(blk, rows)[:, None] + jnp.arange(NA_KROWS)[None, :]
    q_start = jnp.clip(q_row - NA_WIN_ROWS // 2, 0, rows - NA_WIN_ROWS)
    row_ok = (k_row[:, None, :] >= q_start[:, :, None]) & (k_row[:, None, :] < q_start[:, :, None] + NA_WIN_ROWS)
    dr = jnp.clip(k_row[:, None, :] - q_row[:, :, None] + NA_WIN_ROWS - 1, 0, 2 * NA_WIN_ROWS - 2)
    col = jnp.arange(GRID_W)
    col_start = jnp.clip(col - NA_WIN_COLS // 2, 0, GRID_W - NA_WIN_COLS)
    col_ok = (col[None, :] >= col_start[:, None]) & (col[None, :] < col_start[:, None] + NA_WIN_COLS)
    dc = jnp.clip(col[None, :] - col[:, None] + NA_WIN_COLS - 1, 0, 2 * NA_WIN_COLS - 2)
    bias = rpb.astype(F32)[:, dr][..., dc]
    ok = row_ok[None, :, :, :, None, None] & col_ok[None, None, None, None, :, :]
    bias = jnp.where(ok, bias, NEG_INF).transpose(0, 1, 2, 4, 3, 5)
    return bias.reshape(rpb.shape[0], 3, NA_QROWS * GRID_W, NA_KROWS * GRID_W)


def _softmax_pv(scores, values, sink=None):
    m = functools.reduce(jnp.maximum, [jnp.max(s, axis=-1, keepdims=True) for s in scores])
    if sink is not None:
        m = jnp.maximum(m, sink)
    ps = [jnp.exp(s - m) for s in scores]
    denom = functools.reduce(lambda a, b: a + b, [jnp.sum(p, axis=-1, keepdims=True) for p in ps])
    if sink is not None:
        denom = denom + jnp.exp(sink - m)
    o = functools.reduce(lambda a, b: a + b, [_dot(p.astype(BF16), v) for p, v in zip(ps, values)])
    return o / denom


def _na_body(rows, q_ref, k_ref, v_ref, qc_ref, kc_ref, vc_ref, bias_ref, o_ref, oc_ref):
    hd = NA_HD
    n_q, n_k = NA_QROWS * GRID_W, NA_KROWS * GRID_W
    n_blk = rows // NA_QROWS
    kc = kc_ref[...]
    vc = vc_ref[...]

    def blk_body(blk, carry):
        pattern = jnp.where(blk == 0, 0, jnp.where(blk == n_blk - 1, 2, 1))
        qs = pl.multiple_of(blk * n_q, n_q)
        ks = pl.multiple_of(_na_key_start(blk, rows) * GRID_W, GRID_W)
        q2 = q_ref[pl.ds(qs, n_q), :]
        k2 = k_ref[pl.ds(ks, n_k), :]
        v2 = v_ref[pl.ds(ks, n_k), :]
        outs = []
        for h in range(2):
            sl = slice(h * hd, (h + 1) * hd)
            q = q2[:, sl]
            s_loc = _dot_nt(q, k2[:, sl]) + bias_ref[h, pattern]
            s_ctx = _dot_nt(q, kc[:, sl])
            outs.append(_softmax_pv([s_loc, s_ctx], [v2[:, sl], vc[:, sl]]))
        o_ref[pl.ds(qs, n_q), :] = jnp.concatenate(outs, axis=1).astype(o_ref.dtype)
        return carry

    lax.fori_loop(0, n_blk, blk_body, 0)
    outs = []
    for h in range(2):
        sl = slice(h * hd, (h + 1) * hd)
        outs.append(_softmax_pv([_dot_nt(qc_ref[:, sl], kc[:, sl])], [vc[:, sl]]))
    oc_ref[...] = jnp.concatenate(outs, axis=1).astype(oc_ref.dtype)


def na_attention(att, bias, n_batch, seq, ctx_len):
    n_pair = NA_HEADS // 2
    ctx_blk0 = n_batch * seq // ctx_len
    lat = lambda off: pl.BlockSpec((seq, LANES), lambda b, j: (b, off + j))
    ctx = lambda off: pl.BlockSpec((ctx_len, LANES), lambda b, j: (ctx_blk0 + b, off + j))
    return pl.pallas_call(
        functools.partial(_na_body, seq // GRID_W),
        out_shape=(jax.ShapeDtypeStruct((n_batch * seq, NA_HEADS * NA_HD), BF16),
                   jax.ShapeDtypeStruct((n_batch * ctx_len, NA_HEADS * NA_HD), BF16)),
        grid=(n_batch, n_pair),
        in_specs=[lat(0), lat(n_pair), lat(2 * n_pair), ctx(0), ctx(n_pair), ctx(2 * n_pair),
                  pl.BlockSpec((2,) + bias.shape[1:], lambda b, j: (j, 0, 0, 0))],
        out_specs=(pl.BlockSpec((seq, LANES), lambda b, j: (b, j)), pl.BlockSpec((ctx_len, LANES), lambda b, j: (b, j))),
        compiler_params=_params("parallel", "arbitrary"),
        name="na_attention",
    )(att, att, att, att, att, att, bias)


def _wa_body(n_blk, with_ctx, sink_ref, q_ref, k_ref, v_ref, qc_ref, kc_ref, vc_ref, o_ref, oc_ref, kt_ref, kct_ref):
    hd, blk, grp = WA_HD, WINDOW, WA_GROUP
    rows = lax.broadcasted_iota(jnp.int32, (grp * blk, blk), 0) % blk
    cols = lax.broadcasted_iota(jnp.int32, (grp * blk, blk), 1)
    kt_ref[...] = k_ref[...].astype(F32).T.astype(BF16)
    kct_ref[...] = kc_ref[...].astype(F32).T.astype(BF16)
    for kv in range(WA_KV_HEADS):
        ksl = slice(kv * hd, (kv + 1) * hd)
        kc_t = kct_ref[ksl, :]
        vc = vc_ref[:, ksl]

        def stack(ref, start, size):
            return jnp.concatenate([ref[pl.ds(start, size), (kv * grp + g) * hd:(kv * grp + g + 1) * hd]
                                    for g in range(grp)], axis=0)

        def unstack(o, size):
            return jnp.concatenate([o[g * size:(g + 1) * size] for g in range(grp)], axis=1)

        def sink_col(size):
            return jnp.concatenate([jnp.full((size, 1), sink_ref[kv * grp + g], F32) for g in range(grp)], axis=0)

        sink_q = sink_col(blk)

        def blk_body(n, carry):
            qs = pl.multiple_of(n * blk, blk)
            ps = pl.multiple_of(jnp.maximum(n - 1, 0) * blk, blk)
            ns = pl.multiple_of(jnp.minimum(n + 1, n_blk - 1) * blk, blk)
            q = stack(q_ref, qs, blk)
            s_prev = jnp.where((cols >= rows) & (n > 0), _dot(q, kt_ref[ksl, pl.ds(ps, blk)]), NEG_INF)
            s_self = _dot(q, kt_ref[ksl, pl.ds(qs, blk)])
            s_next = jnp.where((cols <= rows) & (n < n_blk - 1), _dot(q, kt_ref[ksl, pl.ds(ns, blk)]), NEG_INF)
            s_ctx = _dot(q, kc_t)
            o = _softmax_pv([s_prev, s_self, s_next, s_ctx],
                            [v_ref[pl.ds(ps, blk), ksl], v_ref[pl.ds(qs, blk), ksl], v_ref[pl.ds(ns, blk), ksl], vc],
                            sink=sink_q)
            o_ref[pl.ds(qs, blk), kv * grp * hd:(kv + 1) * grp * hd] = unstack(o, blk).astype(o_ref.dtype)
            return carry

        lax.fori_loop(0, n_blk, blk_body, 0)
        n_ctx = qc_ref.shape[0]
        if with_ctx:
            qc = stack(qc_ref, 0, n_ctx)
            oc = _softmax_pv([_dot(qc, kc_t)], [vc], sink=sink_col(n_ctx))
            oc_ref[:, kv * grp * hd:(kv + 1) * grp * hd] = unstack(oc, n_ctx).astype(oc_ref.dtype)
        else:
            oc_ref[:, kv * grp * hd:(kv + 1) * grp * hd] = jnp.zeros((n_ctx, grp * hd), oc_ref.dtype)


def window_attention(qkv, sink, n_batch, seq, ctx_len, with_ctx):
    width = WA_HEADS * WA_HD
    ctx_blk0 = n_batch * seq // ctx_len
    kcol, vcol = width // LANES, width // LANES + 1
    return pl.pallas_call(
        functools.partial(_wa_body, seq // WINDOW, with_ctx),
        out_shape=(jax.ShapeDtypeStruct((n_batch * seq, width), BF16), jax.ShapeDtypeStruct((n_batch * ctx_len, width), BF16)),
        grid=(n_batch,),
        in_specs=[pl.BlockSpec(memory_space=pltpu.SMEM),
                  pl.BlockSpec((seq, width), lambda b: (b, 0)),
                  pl.BlockSpec((seq, LANES), lambda b: (b, kcol)),
                  pl.BlockSpec((seq, LANES), lambda b: (b, vcol)),
                  pl.BlockSpec((ctx_len, width), lambda b: (ctx_blk0 + b, 0)),
                  pl.BlockSpec((ctx_len, LANES), lambda b: (ctx_blk0 + b, kcol)),
                  pl.BlockSpec((ctx_len, LANES), lambda b: (ctx_blk0 + b, vcol))],
        out_specs=(pl.BlockSpec((seq, width), lambda b: (b, 0)), pl.BlockSpec((ctx_len, width), lambda b: (b, 0))),
        scratch_shapes=[pltpu.VMEM((LANES, seq), BF16), pltpu.VMEM((LANES, ctx_len), BF16)],
        compiler_params=_params("parallel"),
        name="window_attention",
    )(sink, qkv, qkv, qkv, qkv, qkv, qkv)


CONV_ROWS = 256
CONV_HALO = 8


def _conv_silu_cols(src_ref, src_cols, w_ref, w_cols, pad_ref, dst_ref, dst_row0, dst_cols):
    n_rows = src_ref.shape[0]
    n_blk = n_rows // CONV_ROWS
    zeros = jnp.zeros((CONV_HALO, LANES), F32)
    pad_ref[0:CONV_HALO, :] = zeros
    pad_ref[CONV_HALO + n_rows:2 * CONV_HALO + n_rows, :] = zeros

    def copy(i, carry):
        r0 = pl.multiple_of(i * CONV_ROWS, CONV_ROWS)
        pad_ref[pl.ds(CONV_HALO + r0, CONV_ROWS), :] = src_ref[pl.ds(r0, CONV_ROWS), src_cols]
        return carry

    lax.fori_loop(0, n_blk, copy, 0)
    first = CONV_HALO - CONV_K // 2

    def conv(i, carry):
        r0 = pl.multiple_of(i * CONV_ROWS, CONV_ROWS)
        win = pad_ref[pl.ds(r0, CONV_ROWS + 2 * CONV_HALO), :]
        acc = win[first:first + CONV_ROWS] * w_ref[0:1, w_cols]
        for j in range(1, CONV_K):
            acc = acc + win[first + j:first + j + CONV_ROWS] * w_ref[j:j + 1, w_cols]
        dst_ref[pl.ds(dst_row0 + r0, CONV_ROWS), dst_cols] = _silu(acc)
        return carry

    lax.fori_loop(0, n_blk, conv, 0)


def _tri(upper):
    r = lax.broadcasted_iota(jnp.int32, (CHUNK, CHUNK), 0)
    c = lax.broadcasted_iota(jnp.int32, (CHUNK, CHUNK), 1)
    return (r <= c) if upper else (r >= c)


def _rows_as_lanes(a):
    return jnp.concatenate([a, jnp.zeros((LANES - CHUNK, LANES), F32)], axis=0).T[:, 0:CHUNK]


def _softplus(x):
    return jnp.maximum(x, 0.0) + jnp.log1p(jnp.exp(-jnp.abs(x)))


NEUMANN_STEPS = 5
DN_GROUP = 4
DN_BATCH = 3

def _dn_body(q_ref, k_ref, v_ref, g_ref, qc_ref, kc_ref, vc_ref, gc_ref, wq_ref, wk_ref, wv_ref, alv_ref, dtv_ref,
             o_ref, oc_ref, pad_ref, qs_ref, ks_ref, vs_ref, gb_ref, osum_ref, qe_ref, kb_ref, dec_ref):
    n_ctx, n_lat = qc_ref.shape[0], q_ref.shape[0]
    n_rows = n_ctx + n_lat
    head = pl.program_id(1)
    full = slice(0, LANES)
    for src_c, src, w_ref, dst in ((qc_ref, q_ref, wq_ref, qs_ref), (kc_ref, k_ref, wk_ref, ks_ref), (vc_ref, v_ref, wv_ref, vs_ref)):
        _conv_silu_cols(src_c, full, w_ref, full, pad_ref, dst, 0, full)
        _conv_silu_cols(src, full, w_ref, full, pad_ref, dst, n_ctx, full)

    def l2(i, carry):
        r = pl.multiple_of(i * CONV_ROWS, CONV_ROWS)
        q = qs_ref[pl.ds(r, CONV_ROWS), :]
        qs_ref[pl.ds(r, CONV_ROWS), :] = q * lax.rsqrt(jnp.sum(q * q, axis=-1, keepdims=True) + EPS) * (DN_DK ** -0.5)
        k = ks_ref[pl.ds(r, CONV_ROWS), :]
        ks_ref[pl.ds(r, CONV_ROWS), :] = k * lax.rsqrt(jnp.sum(k * k, axis=-1, keepdims=True) + EPS)
        return carry

    lax.fori_loop(0, n_rows // CONV_ROWS, l2, 0)

    def gates(raw):
        lane = lax.broadcasted_iota(jnp.int32, raw.shape, 1)
        return jnp.where(lane < 2 * DN_HEADS, jax.nn.sigmoid(raw), alv_ref[...] * _softplus(raw + dtv_ref[...]))

    gb_ref[0:n_ctx, :] = gates(gc_ref[...])
    gb_ref[n_ctx:, :] = gates(g_ref[...])
    osum_ref[...] = jnp.zeros(osum_ref.shape, F32)
    grp = DN_GROUP * CHUNK
    nb = DN_BATCH
    n_chunks = n_rows // CHUNK
    n_ctx_chunks = n_ctx // CHUNK
    lane = lax.broadcasted_iota(jnp.int32, (nb, grp, LANES), 2)
    row_i = lax.broadcasted_iota(jnp.int32, (2 * nb, grp, grp), 1)
    col_i = lax.broadcasted_iota(jnp.int32, (2 * nb, grp, grp), 2)
    rev = lax.broadcasted_iota(jnp.int32, (2 * nb, grp, grp), 0) >= nb
    tri = (rev & (row_i <= col_i)) | (jnp.logical_not(rev) & (row_i >= col_i))
    keep = ((row_i // CHUNK) == (col_i // CHUNK)) & tri
    strict = keep & (row_i != col_i)
    ones = jnp.where(keep, 1.0, 0.0).astype(BF16)

    def bdot(a, b):
        return lax.dot_general(a, b, (((2,), (1,)), ((0,), (0,))), preferred_element_type=F32)

    def bdot_nt(a, b):
        return lax.dot_general(a, b, (((2,), (2,)), ((0,), (0,))), preferred_element_type=F32)

    def bdot_tn(a, b):
        return lax.dot_general(a, b, (((1,), (1,)), ((0,), (0,))), preferred_element_type=F32)

    def both(x):
        return jnp.concatenate([x, x], axis=0)

    def trip(ti, carry):
        r = pl.multiple_of(ti * (nb * grp), nb * grp)
        load = lambda ref: ref[pl.ds(r, nb * grp), :].reshape(nb, grp, LANES)
        q, k, v, gb = load(qs_ref), load(ks_ref), load(vs_ref), load(gb_ref)
        pick = lambda l: jnp.sum(jnp.where(lane == l, gb, 0.0), axis=-1, keepdims=True)
        beta = jnp.concatenate([pick(head), pick(DN_HEADS + head)], axis=0)
        g = jnp.broadcast_to(jnp.concatenate([pick(2 * DN_HEADS + head), pick(3 * DN_HEADS + head)], axis=0), (2 * nb, grp, LANES))
        g_hi = g.astype(BF16)
        g_r = g - g_hi.astype(F32)
        g_mid = g_r.astype(BF16)
        g_lo = (g_r - g_mid.astype(F32)).astype(BF16)
        cum = bdot(ones, g_hi) + bdot(ones, g_mid) + bdot(ones, g_lo)
        cum_row = jnp.stack([cum[b].T[0:1, :] for b in range(2 * nb)], axis=0)
        decay = jnp.where(keep, jnp.exp(jnp.concatenate([cum] * (grp // LANES), axis=2) - cum_row), 0.0)
        q2, k2, v2 = both(q), both(k), both(v)
        k16 = k2.astype(BF16)
        kb = k2 * beta
        a = jnp.where(strict, bdot_nt(kb.astype(BF16), k16) * decay, 0.0)
        attn16 = (bdot_nt(q2.astype(BF16), k16) * decay).astype(BF16)
        e_cum = jnp.exp(cum)
        rhs = jnp.concatenate([v2 * beta, kb * e_cum], axis=2)
        n = -a
        a16 = a.astype(BF16)
        p = bdot(a16, a16)
        for it in range(NEUMANN_STEPS):
            p16 = p.astype(BF16)
            n = n + p + bdot(n.astype(BF16), p16)
            if it + 1 < NEUMANN_STEPS:
                p = bdot(p16, p16)
        sol16 = (rhs + bdot(n.astype(BF16), rhs.astype(BF16))).astype(BF16)
        aw = bdot(attn16, sol16)
        o0 = aw[0:nb, :, 0:DN_DK] + aw[nb:, :, 0:DN_DK]
        osum_ref[pl.ds(r, nb * grp), :] += o0.reshape(nb * grp, DN_DK)
        qe = q2 * e_cum - aw[:, :, DN_DK:]
        qe_ref[0, pl.ds(r, nb * grp), :] = qe[0:nb].reshape(nb * grp, DN_DK)
        qe_ref[1, pl.ds(r, nb * grp), :] = qe[nb:].reshape(nb * grp, DN_DK)
        cpt = nb * DN_GROUP
        cum_c = cum.reshape(2 * cpt, CHUNK, LANES)
        edge = jnp.concatenate([cum_c[0:cpt, CHUNK - 1:CHUNK, :], cum_c[cpt:, 0:1, :]], axis=0)
        kdec16 = (k2.reshape(2 * cpt, CHUNK, LANES) * jnp.exp(edge - cum_c)).astype(BF16)
        bk = bdot_tn(kdec16, sol16.reshape(2 * cpt, CHUNK, 2 * DN_DK))
        dec = jnp.broadcast_to(jnp.exp(edge), (2 * cpt, 8, LANES))
        c0 = ti * cpt
        for d in range(2):
            kb_ref[d, pl.ds(c0, cpt)] = bk[d * cpt:(d + 1) * cpt]
            dec_ref[d, pl.ds(c0, cpt)] = dec[d * cpt:(d + 1) * cpt]
        return carry

    lax.fori_loop(0, n_chunks // (DN_GROUP * nb), trip, 0)

    def step(i, states):
        new = []
        for d in range(2):
            c = i if d == 0 else jnp.where(i < n_ctx_chunks, n_ctx_chunks - 1 - i, n_chunks + n_ctx_chunks - 1 - i)
            r = pl.multiple_of(c * CHUNK, CHUNK)
            s16 = states[d].astype(BF16)
            osum_ref[pl.ds(r, CHUNK), :] += _dot(qe_ref[d, pl.ds(r, CHUNK), :].astype(BF16), s16)
            bk = kb_ref[d, c]
            new.append(states[d] * dec_ref[d, c][0:1, :] - _dot(bk[:, DN_DK:].astype(BF16), s16) + bk[:, 0:DN_DK])
        return tuple(new)

    zero = jnp.zeros((DN_DK, DN_DK), F32)
    lax.fori_loop(0, n_chunks, step, (zero, zero))
    oc_ref[...] = osum_ref[0:n_ctx, :]
    o_ref[...] = osum_ref[n_ctx:, :]


def dn_gate_vectors(a_log, dt_bias):
    def vec(v):
        return jnp.pad(v.astype(F32).reshape(1, -1), ((0, 0), (2 * DN_HEADS, LANES - 4 * DN_HEADS)))
    return vec(-jnp.exp(a_log.astype(F32))), vec(dt_bias)


def deltanet(dqkv, dg, conv_w, alv, dtv, n_batch, seq, ctx_len):
    nh = DN_HEADS
    ctx_blk0 = n_batch * seq // ctx_len
    n_rows = seq + ctx_len
    lat = lambda off: pl.BlockSpec((seq, LANES), lambda b, h: (b, off + h))
    ctx = lambda off: pl.BlockSpec((ctx_len, LANES), lambda b, h: (ctx_blk0 + b, off + h))
    cw = lambda off: pl.BlockSpec((CONV_K, LANES), lambda b, h: (0, off + h))
    vec = pl.BlockSpec((1, LANES), lambda b, h: (0, 0))
    seq_buf = pltpu.VMEM((n_rows, LANES), F32)
    return pl.pallas_call(
        _dn_body,
        out_shape=(jax.ShapeDtypeStruct((n_batch * seq, nh * DN_DK), F32), jax.ShapeDtypeStruct((n_batch * ctx_len, nh * DN_DK), F32)),
        grid=(n_batch, nh),
        in_specs=[lat(0), lat(nh), lat(2 * nh), pl.BlockSpec((seq, LANES), lambda b, h: (b, 0)),
                  ctx(0), ctx(nh), ctx(2 * nh), pl.BlockSpec((ctx_len, LANES), lambda b, h: (ctx_blk0 + b, 0)),
                  cw(0), cw(nh), cw(2 * nh), vec, vec],
        out_specs=(pl.BlockSpec((seq, LANES), lambda b, h: (b, h)), pl.BlockSpec((ctx_len, LANES), lambda b, h: (b, h))),
        scratch_shapes=[pltpu.VMEM((seq + 2 * CONV_HALO, LANES), F32), seq_buf, seq_buf, seq_buf, seq_buf, seq_buf,
                        pltpu.VMEM((2, n_rows, LANES), F32),
                        pltpu.VMEM((2, n_rows // CHUNK, DN_DK, 2 * DN_DK), F32),
                        pltpu.VMEM((2, n_rows // CHUNK, 8, LANES), F32)],
        compiler_params=_params("parallel", "arbitrary"),
        name="deltanet",
    )(dqkv, dqkv, dqkv, dg, dqkv, dqkv, dqkv, dg, conv_w, conv_w, conv_w, alv, dtv)


def _ssd_body(xbc_ref, dt_ref, xbc_c_ref, dt_c_ref, cw_ref, av_ref, dtb_ref, dsk_ref, y_ref, yc_ref,
              pad_ref, s_ref, dts_ref, da_ref, hs_ref):
    n_ctx, n_lat = xbc_c_ref.shape[0], xbc_ref.shape[0]
    inner = SSD_HEADS * SSD_HD
    hpg = SSD_HEADS // SSD_GROUPS
    gw = hpg * SSD_HD
    for cb in range(xbc_ref.shape[1] // LANES):
        cols = slice(cb * LANES, (cb + 1) * LANES)
        _conv_silu_cols(xbc_c_ref, cols, cw_ref, cols, pad_ref, s_ref, 0, cols)
        _conv_silu_cols(xbc_ref, cols, cw_ref, cols, pad_ref, s_ref, n_ctx, cols)
    dts_ref[0:n_ctx, :] = _softplus(dt_c_ref[...] + dtb_ref[...])
    dts_ref[n_ctx:, :] = _softplus(dt_ref[...] + dtb_ref[...])
    da_ref[...] = dts_ref[...] * av_ref[...]
    yc_ref[...] = s_ref[0:n_ctx, 0:inner] * dsk_ref[...]
    y_ref[...] = s_ref[n_ctx:, 0:inner] * dsk_ref[...]
    hs_ref[...] = jnp.zeros(hs_ref.shape, F32)

    def chunk(d, c, row0, out_ref):
        r = pl.multiple_of(row0 + c * CHUNK, CHUNK)
        ro = pl.multiple_of(c * CHUNK, CHUNK)
        keep = _tri(upper=(d == 1))
        acum = jnp.dot(keep.astype(F32), da_ref[pl.ds(r, CHUNK), :], precision=HIGHEST, preferred_element_type=F32)
        acum_t = _rows_as_lanes(acum)
        edge = acum[CHUNK - 1:CHUNK, :] if d == 0 else acum[0:1, :]
        dt = dts_ref[pl.ds(r, CHUNK), :]
        for g in range(SSD_GROUPS):
            lanes = [d * SSD_HEADS + g * hpg + h for h in range(hpg)]

            def expand(v, rows):
                return jnp.concatenate([jnp.broadcast_to(v[:, l:l + 1], (rows, SSD_HD)) for l in lanes], axis=1)

            x4 = s_ref[pl.ds(r, CHUNK), g * gw:(g + 1) * gw]
            bm = s_ref[pl.ds(r, CHUNK), inner + g * SSD_STATE:inner + (g + 1) * SSD_STATE].astype(BF16)
            cm = s_ref[pl.ds(r, CHUNK), inner + (SSD_GROUPS + g) * SSD_STATE:inner + (SSD_GROUPS + g + 1) * SSD_STATE].astype(BF16)
            cb_mat = _dot_nt(cm, bm)
            a4 = expand(acum, CHUNK)
            e4 = expand(edge, 1)
            xd = x4 * expand(dt, CHUNK)
            prev = hs_ref[d, g]
            y = _dot_nt(cm, prev.astype(BF16)) * jnp.exp(a4)
            diag = []
            for h, l in enumerate(lanes):
                lmat = jnp.where(keep, jnp.exp(acum[:, l:l + 1] - acum_t[l:l + 1, :]), 0.0)
                diag.append(_dot((cb_mat * lmat).astype(BF16), xd[:, h * SSD_HD:(h + 1) * SSD_HD].astype(BF16)))
            y = y + jnp.concatenate(diag, axis=1)
            out_ref[pl.ds(ro, CHUNK), g * gw:(g + 1) * gw] += y
            states = _dot_tn((xd * jnp.exp(e4 - a4)).astype(BF16), bm)
            decay = jnp.concatenate([jnp.broadcast_to(jnp.exp(edge[:, l:l + 1]), (SSD_HD, SSD_STATE)) for l in lanes], axis=0)
            hs_ref[d, g] = prev * decay + states

    def segment(n_chunks, row0, out_ref):
        def body(i, carry):
            chunk(0, i, row0, out_ref)
            chunk(1, n_chunks - 1 - i, row0, out_ref)
            return carry
        lax.fori_loop(0, n_chunks, body, 0)

    segment(n_ctx // CHUNK, 0, yc_ref)
    segment(n_lat // CHUNK, n_ctx, y_ref)


def ssd(xbc, dt, conv_w, a_vec, dtb_vec, skip_vec, n_batch, seq, ctx_len):
    width = xbc.shape[1]
    inner = SSD_HEADS * SSD_HD
    ctx_blk0 = n_batch * seq // ctx_len
    n_rows = seq + ctx_len
    return pl.pallas_call(
        _ssd_body,
        out_shape=(jax.ShapeDtypeStruct((n_batch * seq, inner), F32), jax.ShapeDtypeStruct((n_batch * ctx_len, inner), F32)),
        grid=(n_batch,),
        in_specs=[pl.BlockSpec((seq, width), lambda b: (b, 0)), pl.BlockSpec((seq, LANES), lambda b: (b, 0)),
                  pl.BlockSpec((ctx_len, width), lambda b: (ctx_blk0 + b, 0)), pl.BlockSpec((ctx_len, LANES), lambda b: (ctx_blk0 + b, 0)),
                  _resident(conv_w.shape), _resident(a_vec.shape), _resident(dtb_vec.shape), _resident(skip_vec.shape)],
        out_specs=(pl.BlockSpec((seq, inner), lambda b: (b, 0)), pl.BlockSpec((ctx_len, inner), lambda b: (b, 0))),
        scratch_shapes=[pltpu.VMEM((seq + 2 * CONV_HALO, LANES), F32), pltpu.VMEM((n_rows, width), F32),
                        pltpu.VMEM((n_rows, LANES), F32), pltpu.VMEM((n_rows, LANES), F32),
                        pltpu.VMEM((2, SSD_GROUPS, (SSD_HEADS // SSD_GROUPS) * SSD_HD, SSD_STATE), F32)],
        compiler_params=_params("parallel"),
        name="ssd",
    )(xbc, dt, xbc, dt, conv_w, a_vec, dtb_vec, skip_vec)


def ssd_vectors(a_log, dt_bias, d_skip):
    def vec(v):
        return jnp.pad(v.astype(F32).reshape(1, -1), ((0, 0), (0, LANES - 2 * SSD_HEADS)))
    return vec(-jnp.exp(a_log.astype(F32))), vec(dt_bias), jnp.repeat(d_skip.astype(F32), SSD_HD)[None, :]


def kernel(x, c, ctx, c_ctx, ada_w, ada_b, norm_w, ev_w_in, ev_w_out, na_rpb, dn_conv, dn_a_log, dn_dt_bias, dn_norm, ffn_w1, ffn_w3, ffn_w2, od_w_in, od_w_out, wa_sink, ssd_conv, ssd_a_log, ssd_dt_bias, ssd_d, ssd_norm, moe_router, moe_w1, moe_w3, moe_w2):
    n_batch, seq, d = x.shape
    ctx_len = ctx.shape[1]
    depth = ada_w.shape[0]
    n_lat, n_ctx = n_batch * seq, n_batch * ctx_len
    n_cond = -(-(n_batch + 1) // 8) * 8
    cond = jnp.concatenate([c, c_ctx[None, :], jnp.zeros((n_cond - n_batch - 1, d), F32)], axis=0)
    mod_all = ada_modulation(cond, ada_w, ada_b)
    x_all = jnp.concatenate([x.reshape(n_lat, d), ctx.reshape(n_ctx, d)], axis=0)
    cos_t, sin_t = rope_tables(seq)
    moe_w1_16, moe_w3_16, moe_w2_16 = moe_w1.astype(BF16), moe_w3.astype(BF16), moe_w2.astype(BF16)
    for layer in range(depth):
        i = layer // 2
        n_tok = n_lat + n_ctx if layer < depth - 1 else n_lat
        mod = mod_all[layer, :n_batch + 1].reshape(n_batch + 1, 6, d)
        nw = norm_w[layer]
        if layer % 2 == 0:
            att, dqkv, dz, dg = proj_in_even(x_all, mod, nw, prep_w_in_even(ev_w_in[i]), n_batch, seq)
            o_a, oc_a = na_attention(att, na_bias_tables(na_rpb[i], seq // GRID_W), n_batch, seq, ctx_len)
            alv, dtv = dn_gate_vectors(dn_a_log[i], dn_dt_bias[i])
            o_b, oc_b = deltanet(dqkv, dg, dn_conv[i], alv, dtv, n_batch, seq, ctx_len)
            x_all = proj_out(o_a, oc_a, o_b, oc_b, dz,
                             jnp.tile(dn_norm[i], DN_HEADS)[None, :], ev_w_out[i].astype(BF16), x_all, mod, nw,
                             n_batch, seq, n_tok, DN_DK, False)
            x_all = ffn_dense(x_all, mod, nw, ffn_w1[i].astype(BF16), ffn_w3[i].astype(BF16), ffn_w2[i].astype(BF16),
                              n_batch, seq, n_tok)
        else:
            qkv, z, xbc, dt = proj_in_odd(x_all, mod, nw, prep_w_in_odd(od_w_in[i]), cos_t, sin_t, n_batch, seq)
            o_c, oc_c = window_attention(qkv, wa_sink[i], n_batch, seq, ctx_len, n_tok > n_lat)
            a_vec, dtb_vec, skip_vec = ssd_vectors(ssd_a_log[i], ssd_dt_bias[i], ssd_d[i])
            y_d, yc_d = ssd(xbc, dt, ssd_conv[i], a_vec, dtb_vec, skip_vec, n_batch, seq, ctx_len)
            x_all = proj_out(o_c, oc_c, y_d, yc_d, z,
                             ssd_norm[i][None, :], od_w_out[i].astype(BF16), x_all, mod, nw,
                             n_batch, seq, n_tok, SSD_HEADS * SSD_HD // SSD_GROUPS, True)
            router = jnp.pad(moe_router[i], ((0, 0), (0, LANES - N_EXPERTS)))
            x_all = moe_layer(x_all, mod, nw, router, moe_w1_16, moe_w3_16, moe_w2_16, i, n_batch, seq, n_tok)
    return x_all[:n_lat].reshape(n_batch, seq, d)
```

```python
import functools
import math

import jax
import jax.numpy as jnp
import numpy as np
from jax import lax
from jax.experimental import pallas as pl
from jax.experimental.pallas import tpu as pltpu

F32 = jnp.float32
BF16 = jnp.bfloat16
HIGHEST = lax.Precision.HIGHEST

EPS = 1e-6
NEG_INF = -1e30
GRID_W = 64
CHUNK = 64
CONV_K = 5
ROPE_BASE = 10000.0
NA_HEADS, NA_HD, NA_WIN_ROWS, NA_WIN_COLS = 8, 64, 8, 16
DN_HEADS, DN_DK = 4, 128
WA_HEADS, WA_KV_HEADS, WA_HD, WINDOW = 8, 2, 64, 128
WA_GROUP = WA_HEADS // WA_KV_HEADS
SSD_HEADS, SSD_HD, SSD_GROUPS, SSD_STATE = 8, 64, 2, 128
N_EXPERTS, TOP_K = 8, 2

LANES = 128
VMEM_LIMIT = 56 * 1024 * 1024
TOK_TILE = 512


def _params(*sem):
    return pltpu.CompilerParams(dimension_semantics=sem, vmem_limit_bytes=VMEM_LIMIT)


def _resident(shape):
    nd = len(shape)
    return pl.BlockSpec(shape, lambda *_: (0,) * nd, pipeline_mode=pl.Buffered(1))


def _silu(x):
    return x * jax.nn.sigmoid(x)


def _rms(x, w):
    return x * lax.rsqrt(jnp.mean(x * x, axis=-1, keepdims=True) + EPS) * w


def _dot(a, b):
    return jnp.dot(a, b, preferred_element_type=F32)


def _dot_nt(a, b):
    return lax.dot_general(a, b, (((1,), (1,)), ((), ())), preferred_element_type=F32)


def _dot_tn(a, b):
    return lax.dot_general(a, b, (((0,), (0,)), ((), ())), preferred_element_type=F32)


def _ada_body(s_ref, w_ref, b_ref, o_ref):
    s = _silu(s_ref[...])
    o_ref[0] = jnp.dot(s, w_ref[0], precision=HIGHEST, preferred_element_type=F32) + b_ref[0]


def ada_modulation(cond, ada_w, ada_b):
    depth, d, n = ada_w.shape
    r = cond.shape[0]
    tn = 1536
    return pl.pallas_call(
        _ada_body,
        out_shape=jax.ShapeDtypeStruct((depth, r, n), F32),
        grid=(depth, n // tn),
        in_specs=[
            pl.BlockSpec((r, d), lambda l, j: (0, 0)),
            pl.BlockSpec((1, d, tn), lambda l, j: (l, 0, j)),
            pl.BlockSpec((1, 1, tn), lambda l, j: (l, 0, j)),
        ],
        out_specs=pl.BlockSpec((1, r, tn), lambda l, j: (l, 0, j)),
        compiler_params=_params("arbitrary", "arbitrary"),
        name="ada_modulation",
    )(cond, ada_w, ada_b.reshape(depth, 1, n))


def _mod_rows(n_lat_tiles, tiles_per_batch, n_batch):
    def index(i):
        return (jnp.minimum(i // tiles_per_batch, n_batch), 0, 0)
    del n_lat_tiles
    return index


def _norm_mod(x, mod_ref, nw_ref, row):
    y = _rms(x, nw_ref[row:row + 1, :])
    return y * (1.0 + mod_ref[0, row + 1:row + 2, :]) + mod_ref[0, row:row + 1, :]


def _proj_even_body(x_ref, mod_ref, nw_ref, w_ref, att_ref, dqkv_ref, dz_ref, dg_ref):
    hb = _norm_mod(x_ref[...], mod_ref, nw_ref, 0).astype(BF16)
    att_ref[...] = _dot(hb, w_ref[:, 0:1536]).astype(BF16)
    dqkv_ref[...] = _dot(hb, w_ref[:, 1536:3072])
    dz_ref[...] = _dot(hb, w_ref[:, 3072:3584])
    dg_ref[...] = _dot(hb, w_ref[:, 3584:3712])


def _swap16(x):
    lane = lax.broadcasted_iota(jnp.int32, x.shape, 1)
    up = pltpu.roll(x, LANES - 16, 1)
    down = pltpu.roll(x, 16, 1)
    return jnp.where((lane % 32) < 16, up, down)


def _proj_odd_body(x_ref, mod_ref, nw_ref, w_ref, cos_ref, sin_ref, qkv_ref, z_ref, xbc_ref, dt_ref):
    hb = _norm_mod(x_ref[...], mod_ref, nw_ref, 0).astype(BF16)
    cos, sin = cos_ref[...], sin_ref[...]
    for c in range(5):
        t = _dot(hb, w_ref[:, c * LANES:(c + 1) * LANES])
        qkv_ref[:, c * LANES:(c + 1) * LANES] = (t * cos + _swap16(t) * sin).astype(BF16)
    qkv_ref[:, 640:768] = _dot(hb, w_ref[:, 640:768]).astype(BF16)
    z_ref[...] = _dot(hb, w_ref[:, 768:1280])
    xbc_ref[...] = _dot(hb, w_ref[:, 1280:2304])
    dt_ref[...] = _dot(hb, w_ref[:, 2304:2432])


def _pad_cols(w, n):
    return jnp.pad(w, ((0, 0), (0, n - w.shape[1])))


def prep_w_in_even(w):
    q_scale = jnp.concatenate([jnp.full((NA_HEADS * NA_HD,), NA_HD ** -0.5, F32), jnp.ones((w.shape[1] - NA_HEADS * NA_HD,), F32)])
    return _pad_cols(w * q_scale[None, :], 3712).astype(BF16)


def prep_w_in_odd(w):
    q_scale = jnp.concatenate([jnp.full((WA_HEADS * WA_HD,), WA_HD ** -0.5, F32), jnp.ones((w.shape[1] - WA_HEADS * WA_HD,), F32)])
    return _pad_cols(w * q_scale[None, :], 2432).astype(BF16)


def _tok_spec(width, tm=TOK_TILE):
    return pl.BlockSpec((tm, width), lambda i: (i, 0))


def proj_in_even(x_all, mod, nw, w, n_batch, seq):
    t_all, d = x_all.shape
    tm = TOK_TILE
    tpb = seq // tm
    return pl.pallas_call(
        _proj_even_body,
        out_shape=(jax.ShapeDtypeStruct((t_all, 1536), BF16), jax.ShapeDtypeStruct((t_all, 1536), F32),
                   jax.ShapeDtypeStruct((t_all, 512), F32), jax.ShapeDtypeStruct((t_all, LANES), F32)),
        grid=(t_all // tm,),
        in_specs=[_tok_spec(d), pl.BlockSpec((1, 6, d), _mod_rows(None, tpb, n_batch)),
                  _resident(nw.shape), _resident(w.shape)],
        out_specs=(_tok_spec(1536), _tok_spec(1536), _tok_spec(512), _tok_spec(LANES)),
        compiler_params=_params("parallel"),
        name="proj_in_even",
    )(x_all, mod, nw, w)


def proj_in_odd(x_all, mod, nw, w, cos_t, sin_t, n_batch, seq):
    t_all, d = x_all.shape
    tm = TOK_TILE
    tpb = seq // tm

    def rope_rows(i):
        return (jnp.where(i < n_batch * tpb, i % tpb, tpb), 0)

    return pl.pallas_call(
        _proj_odd_body,
        out_shape=(jax.ShapeDtypeStruct((t_all, 768), BF16), jax.ShapeDtypeStruct((t_all, 512), F32),
                   jax.ShapeDtypeStruct((t_all, 1024), F32), jax.ShapeDtypeStruct((t_all, LANES), F32)),
        grid=(t_all // tm,),
        in_specs=[_tok_spec(d), pl.BlockSpec((1, 6, d), _mod_rows(None, tpb, n_batch)),
                  _resident(nw.shape), _resident(w.shape),
                  pl.BlockSpec((tm, LANES), rope_rows), pl.BlockSpec((tm, LANES), rope_rows)],
        out_specs=(_tok_spec(768), _tok_spec(512), _tok_spec(1024), _tok_spec(LANES)),
        compiler_params=_params("parallel"),
        name="proj_in_odd",
    )(x_all, mod, nw, w, cos_t, sin_t)


def rope_tables(seq, tm=TOK_TILE):
    half = WA_HD // 4
    freqs = ROPE_BASE ** (-jnp.arange(half, dtype=F32) / half)
    t = jnp.arange(seq)
    ang_r = (t // GRID_W).astype(F32)[:, None] * freqs[None, :]
    ang_c = (t % GRID_W).astype(F32)[:, None] * freqs[None, :]
    cos = jnp.concatenate([jnp.cos(ang_r)] * 2 + [jnp.cos(ang_c)] * 2, axis=-1)
    sin = jnp.concatenate([-jnp.sin(ang_r), jnp.sin(ang_r), -jnp.sin(ang_c), jnp.sin(ang_c)], axis=-1)
    cos = jnp.concatenate([jnp.tile(cos, (1, 2)), jnp.ones((tm, LANES), F32)], axis=0)
    sin = jnp.concatenate([jnp.tile(sin, (1, 2)), jnp.zeros((tm, LANES), F32)], axis=0)
    return cos, sin


def _proj_out_body(group, gate_first, n_lat_tiles, a_ref, ac_ref, r_ref, rc_ref, z_ref, gw_ref, w_ref, x_ref, mod_ref,
                   nw_ref, o_ref):
    is_ctx = pl.program_id(0) >= n_lat_tiles
    a = jnp.where(is_ctx, ac_ref[...], a_ref[...])
    r = jnp.where(is_ctx, rc_ref[...], r_ref[...])
    gate = _silu(z_ref[...])
    if gate_first:
        r = r * gate
    parts = []
    for c in range(r.shape[1] // group):
        rc = r[:, c * group:(c + 1) * group]
        parts.append(_rms(rc, gw_ref[:, c * group:(c + 1) * group]))
    r = jnp.concatenate(parts, axis=1)
    if not gate_first:
        r = r * gate
    half = a.shape[1]
    y = _dot(a, w_ref[0:half, :]) + _dot(r.astype(BF16), w_ref[half:, :])
    o_ref[...] = x_ref[...] + mod_ref[0, 2:3, :] * _rms(y, nw_ref[1:2, :])


def proj_out(att, att_c, rec, rec_c, z, gate_w, w_out, x_all, mod, nw, n_batch, seq, n_tok, group, gate_first):
    d = x_all.shape[1]
    tm = TOK_TILE
    tpb = seq // tm
    half = att.shape[1]
    n_lat_tiles = att.shape[0] // tm
    lat = pl.BlockSpec((tm, half), lambda i: (jnp.minimum(i, n_lat_tiles - 1), 0))
    ctx = pl.BlockSpec((tm, half), lambda i: (jnp.maximum(i - n_lat_tiles, 0), 0))
    return pl.pallas_call(
        functools.partial(_proj_out_body, group, gate_first, n_lat_tiles),
        out_shape=jax.ShapeDtypeStruct((n_tok, d), F32),
        grid=(n_tok // tm,),
        in_specs=[lat, ctx, lat, ctx, _tok_spec(half), _resident(gate_w.shape),
                  _resident(w_out.shape), _tok_spec(d), pl.BlockSpec((1, 6, d), _mod_rows(None, tpb, n_batch)),
                  _resident(nw.shape)],
        out_specs=_tok_spec(d),
        compiler_params=_params("parallel"),
        name="proj_out",
    )(att, att_c, rec, rec_c, z, gate_w, w_out, x_all, mod, nw)


def _ffn_dense_body(n_split, x_ref, mod_ref, nw_ref, w1_ref, w3_ref, w2_ref, o_ref):
    x = x_ref[...]
    y = _rms(x, nw_ref[2:3, :])
    hb = (y * (1.0 + mod_ref[0, 4:5, :]) + mod_ref[0, 3:4, :]).astype(BF16)
    ff = w1_ref.shape[1]
    step = ff // n_split
    acc = None
    for c in range(n_split):
        sl = slice(c * step, (c + 1) * step)
        act = (_silu(_dot(hb, w1_ref[:, sl])) * _dot(hb, w3_ref[:, sl])).astype(BF16)
        part = _dot(act, w2_ref[sl, :])
        acc = part if acc is None else acc + part
    o_ref[...] = x + mod_ref[0, 5:6, :] * _rms(acc, nw_ref[3:4, :])


def ffn_dense(x_all, mod, nw, w1, w3, w2, n_batch, seq, n_tok):
    d = x_all.shape[1]
    tm = TOK_TILE
    tpb = seq // tm
    return pl.pallas_call(
        functools.partial(_ffn_dense_body, 2),
        out_shape=jax.ShapeDtypeStruct((n_tok, d), F32),
        grid=(n_tok // tm,),
        in_specs=[_tok_spec(d), pl.BlockSpec((1, 6, d), _mod_rows(None, tpb, n_batch)), _resident(nw.shape),
                  _resident(w1.shape), _resident(w3.shape), _resident(w2.shape)],
        out_specs=_tok_spec(d),
        compiler_params=_params("parallel"),
        name="ffn_dense",
    )(x_all, mod, nw, w1, w3, w2)


MOE_TILE = 1024
MOE_FF_TILE = 896


def _moe_route_body(x_ref, mod_ref, nw_ref, r_ref, h_ref, idx_ref, gate_ref):
    y = _rms(x_ref[...], nw_ref[2:3, :])
    h = y * (1.0 + mod_ref[0, 4:5, :]) + mod_ref[0, 3:4, :]
    h_ref[...] = h
    logits = jnp.dot(h, r_ref[...], precision=HIGHEST, preferred_element_type=F32)
    lane = lax.broadcasted_iota(jnp.int32, logits.shape, 1)
    logits = jnp.where(lane < N_EXPERTS, logits, -jnp.inf)
    v1 = jnp.max(logits, axis=-1, keepdims=True)
    i1 = jnp.min(jnp.where(logits == v1, lane, LANES), axis=-1, keepdims=True)
    rest = jnp.where(lane == i1, -jnp.inf, logits)
    v2 = jnp.max(rest, axis=-1, keepdims=True)
    i2 = jnp.min(jnp.where(rest == v2, lane, LANES), axis=-1, keepdims=True)
    e2 = jnp.exp(v2 - v1)
    g1 = 1.0 / (1.0 + e2)
    idx_ref[...] = jnp.where(lane == 0, i1, jnp.where(lane == 1, i2, 0))
    gate_ref[...] = jnp.where(lane == 0, g1, jnp.where(lane == 1, e2 * g1, 0.0))


def moe_route(x_all, mod, nw, router, n_batch, seq, n_tok):
    d = x_all.shape[1]
    tm = TOK_TILE
    tpb = seq // tm
    return pl.pallas_call(
        _moe_route_body,
        out_shape=(jax.ShapeDtypeStruct((n_tok, d), F32), jax.ShapeDtypeStruct((n_tok, LANES), jnp.int32),
                   jax.ShapeDtypeStruct((n_tok, LANES), F32)),
        grid=(n_tok // tm,),
        in_specs=[_tok_spec(d), pl.BlockSpec((1, 6, d), _mod_rows(None, tpb, n_batch)), _resident(nw.shape),
                  _resident(router.shape)],
        out_specs=(_tok_spec(d), _tok_spec(LANES), _tok_spec(LANES)),
        compiler_params=_params("parallel"),
        name="moe_route",
    )(x_all, mod, nw, router)


SUBLANES = 8


def _for_rows(n_rows, fn):
    def body(g, carry):
        base = pl.multiple_of(g * SUBLANES, SUBLANES)
        for j in range(SUBLANES):
            fn(base + j)
        return carry
    lax.fori_loop(0, n_rows // SUBLANES, body, 0)


def _moe_gmm_body(te_ref, nv_ref, src_ref, src_next_ref, h_ref, w1_ref, w3_ref, w2_ref, o_ref, xbuf, x16, sem):
    i, f = pl.program_id(0), pl.program_id(1)
    n_live = nv_ref[0]
    tm = x16.shape[0]
    slot = i % 2

    def row_copy(s_ref, s, r):
        return pltpu.make_async_copy(h_ref.at[pl.ds(s_ref[0, 0, r], 1), :], xbuf.at[s, pl.ds(r, 1), :], sem.at[s])

    for_rows = functools.partial(_for_rows, tm)

    @pl.when((f == 0) & (i == 0) & (n_live > 0))
    def _():
        for_rows(lambda r: row_copy(src_ref, 0, r).start())

    @pl.when((f == 0) & (i < n_live))
    def _():
        for_rows(lambda r: row_copy(src_ref, slot, r).wait())
        x16[...] = xbuf[slot].astype(BF16)

    @pl.when((f == 1) & (i + 1 < n_live))
    def _():
        for_rows(lambda r: row_copy(src_next_ref, 1 - slot, r).start())

    @pl.when(i < n_live)
    def _():
        x = x16[...]
        act = (_silu(_dot(x, w1_ref[0, 0])) * _dot(x, w3_ref[0, 0])).astype(BF16)
        part = _dot(act, w2_ref[0, 0])

        @pl.when(f == 0)
        def _():
            o_ref[...] = part

        @pl.when(f > 0)
        def _():
            o_ref[...] += part

    @pl.when((i >= n_live) & (f == 0))
    def _():
        o_ref[...] = jnp.zeros(o_ref.shape, o_ref.dtype)


def moe_gmm(tile_expert, n_valid, src_tok, h, w1, w3, w2, layer):
    p = src_tok.shape[0]
    d = h.shape[1]
    ff = w1.shape[3]
    tm, tf = MOE_TILE, MOE_FF_TILE
    nt, nf = p // tm, ff // tf
    assert nf >= 2

    def wcol(i, f, te, nv):
        live = i < nv[0]
        return (layer, te[i], 0, jnp.where(live, f, nf - 1))

    def wrow(i, f, te, nv):
        live = i < nv[0]
        return (layer, te[i], jnp.where(live, f, nf - 1), 0)

    src_tiles = src_tok.reshape(nt, 1, tm)
    smem = lambda index: pl.BlockSpec((1, 1, tm), index, memory_space=pltpu.SMEM)
    return pl.pallas_call(
        _moe_gmm_body,
        out_shape=jax.ShapeDtypeStruct((p, d), F32),
        grid_spec=pltpu.PrefetchScalarGridSpec(
            num_scalar_prefetch=2,
            grid=(nt, nf),
            in_specs=[smem(lambda i, f, te, nv: (i, 0, 0)), smem(lambda i, f, te, nv: (jnp.minimum(i + 1, nt - 1), 0, 0)),
                      pl.BlockSpec(memory_space=pl.ANY),
                      pl.BlockSpec((1, 1, d, tf), wcol), pl.BlockSpec((1, 1, d, tf), wcol), pl.BlockSpec((1, 1, tf, d), wrow)],
            out_specs=pl.BlockSpec((tm, d), lambda i, f, te, nv: (i, 0)),
            scratch_shapes=[pltpu.VMEM((2, tm, d), F32), pltpu.VMEM((tm, d), BF16), pltpu.SemaphoreType.DMA((2,))]),
        compiler_params=_params("arbitrary", "arbitrary"),
        name="moe_gmm",
    )(tile_expert, n_valid, src_tiles, src_tiles, h, w1, w3, w2)


def moe_dispatch(idx, n_tok):
    tm = MOE_TILE
    n_assign = n_tok * TOP_K
    p = n_assign + N_EXPERTS * tm
    e_flat = idx[:, :TOP_K].reshape(n_assign)
    onehot = (e_flat[:, None] == jnp.arange(N_EXPERTS)[None, :]).astype(jnp.int32)
    counts = jnp.sum(onehot, axis=0)
    padded = ((counts + tm - 1) // tm) * tm
    ends = jnp.cumsum(padded)
    starts = ends - padded
    pos = jnp.sum(onehot * (jnp.cumsum(onehot, axis=0) - onehot + starts[None, :]), axis=1)
    src_tok = jnp.zeros((p,), jnp.int32).at[pos].set(jnp.arange(n_assign, dtype=jnp.int32) // TOP_K)
    tile_start = jnp.arange(p // tm, dtype=jnp.int32) * tm
    tile_expert = jnp.minimum(jnp.sum((tile_start[:, None] >= ends[None, :]).astype(jnp.int32), axis=1), N_EXPERTS - 1)
    n_valid = (ends[-1] // tm).astype(jnp.int32).reshape(1)
    last_live = jnp.take(tile_expert, jnp.maximum(n_valid - 1, 0))
    tile_expert = jnp.where(tile_start < ends[-1], tile_expert, last_live).astype(jnp.int32)
    return src_tok, pos.reshape(n_tok, TOP_K), tile_expert, n_valid


def _moe_combine_body(pos_ref, pos_next_ref, ys_ref, gate_ref, x_ref, mod_ref, nw_ref, o_ref, ybuf, sem):
    i = pl.program_id(0)
    n_tiles = pl.num_programs(0)
    tm = x_ref.shape[0]
    slot = i % 2

    def row_copy(p_ref, s, k, r):
        return pltpu.make_async_copy(ys_ref.at[pl.ds(p_ref[0, 0, k * tm + r], 1), :], ybuf.at[s, k, pl.ds(r, 1), :], sem.at[s])

    for_rows = functools.partial(_for_rows, tm)

    def start_tile(p_ref, s):
        def fn(r):
            row_copy(p_ref, s, 0, r).start()
            row_copy(p_ref, s, 1, r).start()
        for_rows(fn)

    @pl.when(i == 0)
    def _():
        start_tile(pos_ref, 0)

    @pl.when(i + 1 < n_tiles)
    def _():
        start_tile(pos_next_ref, 1 - slot)

    def wait_row(r):
        row_copy(pos_ref, slot, 0, r).wait()
        row_copy(pos_ref, slot, 1, r).wait()
    for_rows(wait_row)

    g = gate_ref[...]
    ffn = ybuf[slot, 0] * g[:, 0:1] + ybuf[slot, 1] * g[:, 1:2]
    o_ref[...] = x_ref[...] + mod_ref[0, 5:6, :] * _rms(ffn, nw_ref[3:4, :])


def moe_combine(ys, pos, gates, x_all, mod, nw, n_batch, seq, n_tok):
    d = x_all.shape[1]
    tm = TOK_TILE
    tpb = seq // tm
    n_tiles = n_tok // tm
    pos_tiles = pos.reshape(n_tiles, tm, TOP_K).transpose(0, 2, 1).reshape(n_tiles, 1, TOP_K * tm)
    smem = lambda index: pl.BlockSpec((1, 1, TOP_K * tm), index, memory_space=pltpu.SMEM)
    return pl.pallas_call(
        _moe_combine_body,
        out_shape=jax.ShapeDtypeStruct((n_tok, d), F32),
        grid=(n_tiles,),
        in_specs=[smem(lambda i: (i, 0, 0)), smem(lambda i: (jnp.minimum(i + 1, n_tiles - 1), 0, 0)),
                  pl.BlockSpec(memory_space=pl.ANY), _tok_spec(LANES), _tok_spec(d),
                  pl.BlockSpec((1, 6, d), _mod_rows(None, tpb, n_batch)), _resident(nw.shape)],
        out_specs=_tok_spec(d),
        scratch_shapes=[pltpu.VMEM((2, TOP_K, tm, d), F32), pltpu.SemaphoreType.DMA((2,))],
        compiler_params=_params("arbitrary"),
        name="moe_combine",
    )(pos_tiles, pos_tiles, ys, gates, x_all, mod, nw)


def moe_layer(x_all, mod, nw, router, w1, w3, w2, layer, n_batch, seq, n_tok):
    h, idx, gates = moe_route(x_all, mod, nw, router, n_batch, seq, n_tok)
    src_tok, pos, tile_expert, n_valid = moe_dispatch(idx, n_tok)
    ys = moe_gmm(tile_expert, n_valid, src_tok, h, w1, w3, w2, layer)
    return moe_combine(ys, pos, gates, x_all, mod, nw, n_batch, seq, n_tok)


NA_QROWS = 4
NA_KROWS = NA_QROWS + NA_WIN_ROWS - 1


def _na_key_start(blk, rows):
    return jnp.clip(blk * NA_QROWS - NA_WIN_ROWS // 2, 0, rows - NA_KROWS)


def na_bias_tables(rpb, rows):
    n_blk = rows // NA_QROWS
    blk = jnp.array([0, 1, n_blk - 1])
    q_row = blk[:, None] * NA_QROWS + jnp.arange(NA_QROWS)[None, :]
    k_row = _na_key_start(blk, rows)[:, None] + jnp.arange(NA_KROWS)[None, :]
    q_start = jnp.clip(q_row - NA_WIN_ROWS // 2, 0, rows - NA_WIN_ROWS)
    row_ok = (k_row[:, None, :] >= q_start[:, :, None]) & (k_row[:, None, :] < q_start[:, :, None] + NA_WIN_ROWS)
    dr = jnp.clip(k_row[:, None, :] - q_row[:, :, None] + NA_WIN_ROWS - 1, 0, 2 * NA_WIN_ROWS - 2)
    col = jnp.arange(GRID_W)
    col_start = jnp.clip(col - NA_WIN_COLS // 2, 0, GRID_W - NA_WIN_COLS)
    col_ok = (col[None, :] >= col_start[:, None]) & (col[None, :] < col_start[:, None] + NA_WIN_COLS)
    dc = jnp.clip(col[None, :] - col[:, None] + NA_WIN_COLS - 1, 0, 2 * NA_WIN_COLS - 2)
    pick_col = jax.nn.one_hot(dc, 2 * NA_WIN_COLS - 1, dtype=F32)
    bias = jnp.einsum('hpqkc,xyc->hpqkxy', rpb.astype(F32)[:, dr], pick_col, precision=HIGHEST)
    ok = row_ok[None, :, :, :, None, None] & col_ok[None, None, None, None, :, :]
    bias = jnp.where(ok, bias, NEG_INF).transpose(0, 1, 2, 4, 3, 5)
    return bias.reshape(rpb.shape[0], 3, NA_QROWS * GRID_W, NA_KROWS * GRID_W)


def _row_reduce(blocks, combine, reduce):
    tiles, out = [], None
    for b in blocks:
        if b.shape[1] % LANES == 0:
            tiles += [b[:, c * LANES:(c + 1) * LANES] for c in range(b.shape[1] // LANES)]
        else:
            r = reduce(b, axis=-1, keepdims=True)
            out = r if out is None else combine(out, r)
    if tiles:
        r = reduce(functools.reduce(combine, tiles), axis=-1, keepdims=True)
        out = r if out is None else combine(out, r)
    return out


def _softmax_pv(scores, values, sink=None):
    m = _row_reduce(scores, jnp.maximum, jnp.max)
    if sink is not None:
        m = jnp.maximum(m, sink)
    ps = [jnp.exp(s - m) for s in scores]
    denom = _row_reduce(ps, jnp.add, jnp.sum)
    if sink is not None:
        denom = denom + jnp.exp(sink - m)
    o = functools.reduce(lambda a, b: a + b, [_dot(p.astype(BF16), v) for p, v in zip(ps, values)])
    return o / denom


def _na_body(rows, q_ref, k_ref, v_ref, qc_ref, kc_ref, vc_ref, bias_ref, o_ref, oc_ref):
    hd = NA_HD
    n_q, n_k = NA_QROWS * GRID_W, NA_KROWS * GRID_W
    n_blk = rows // NA_QROWS
    kc = kc_ref[...]
    vc = vc_ref[...]

    def blk_body(blk, carry):
        pattern = jnp.where(blk == 0, 0, jnp.where(blk == n_blk - 1, 2, 1))
        qs = pl.multiple_of(blk * n_q, n_q)
        ks = pl.multiple_of(_na_key_start(blk, rows) * GRID_W, GRID_W)
        q2 = q_ref[pl.ds(qs, n_q), :]
        k2 = k_ref[pl.ds(ks, n_k), :]
        v2 = v_ref[pl.ds(ks, n_k), :]
        outs = []
        for h in range(2):
            sl = slice(h * hd, (h + 1) * hd)
            q = q2[:, sl]
            s_loc = _dot_nt(q, k2[:, sl]) + bias_ref[h, pattern]
            s_ctx = _dot_nt(q, kc[:, sl])
            outs.append(_softmax_pv([s_loc, s_ctx], [v2[:, sl], vc[:, sl]]))
        o_ref[pl.ds(qs, n_q), :] = jnp.concatenate(outs, axis=1).astype(o_ref.dtype)
        return carry

    lax.fori_loop(0, n_blk, blk_body, 0)
    outs = []
    for h in range(2):
        sl = slice(h * hd, (h + 1) * hd)
        outs.append(_softmax_pv([_dot_nt(qc_ref[:, sl], kc[:, sl])], [vc[:, sl]]))
    oc_ref[...] = jnp.concatenate(outs, axis=1).astype(oc_ref.dtype)


def na_attention(att, bias, n_batch, seq, ctx_len):
    n_pair = NA_HEADS // 2
    ctx_blk0 = n_batch * seq // ctx_len
    lat = lambda off: pl.BlockSpec((seq, LANES), lambda b, j: (b, off + j))
    ctx = lambda off: pl.BlockSpec((ctx_len, LANES), lambda b, j: (ctx_blk0 + b, off + j))
    return pl.pallas_call(
        functools.partial(_na_body, seq // GRID_W),
        out_shape=(jax.ShapeDtypeStruct((n_batch * seq, NA_HEADS * NA_HD), BF16),
                   jax.ShapeDtypeStruct((n_batch * ctx_len, NA_HEADS * NA_HD), BF16)),
        grid=(n_batch, n_pair),
        in_specs=[lat(0), lat(n_pair), lat(2 * n_pair), ctx(0), ctx(n_pair), ctx(2 * n_pair),
                  pl.BlockSpec((2,) + bias.shape[1:], lambda b, j: (j, 0, 0, 0))],
        out_specs=(pl.BlockSpec((seq, LANES), lambda b, j: (b, j)), pl.BlockSpec((ctx_len, LANES), lambda b, j: (b, j))),
        compiler_params=_params("parallel", "arbitrary"),
        name="na_attention",
    )(att, att, att, att, att, att, bias)


def _wa_body(n_blk, with_ctx, sink_ref, q_ref, k_ref, v_ref, qc_ref, kc_ref, vc_ref, o_ref, oc_ref, kt_ref, kct_ref):
    hd, blk, grp = WA_HD, WINDOW, WA_GROUP
    rows = lax.broadcasted_iota(jnp.int32, (grp * blk, blk), 0) % blk
    cols = lax.broadcasted_iota(jnp.int32, (grp * blk, blk), 1)
    kt_ref[...] = k_ref[...].astype(F32).T.astype(BF16)
    kct_ref[...] = kc_ref[...].astype(F32).T.astype(BF16)
    n_ctx = qc_ref.shape[0]

    def stack(ref, kv, start, size):
        return jnp.concatenate([ref[pl.ds(start, size), (kv * grp + g) * hd:(kv * grp + g + 1) * hd]
                                for g in range(grp)], axis=0)

    def unstack(o, size):
        return jnp.concatenate([o[g * size:(g + 1) * size] for g in range(grp)], axis=1)

    def sink_col(kv, size):
        return jnp.concatenate([jnp.full((size, 1), sink_ref[kv * grp + g], F32) for g in range(grp)], axis=0)

    sink_q = [sink_col(kv, blk) for kv in range(WA_KV_HEADS)]

    def blk_body(n, carry):
        qs = pl.multiple_of(n * blk, blk)
        ps = pl.multiple_of(jnp.maximum(n - 1, 0) * blk, blk)
        ns = pl.multiple_of(jnp.minimum(n + 1, n_blk - 1) * blk, blk)
        for kv in range(WA_KV_HEADS):
            ksl = slice(kv * hd, (kv + 1) * hd)
            q = stack(q_ref, kv, qs, blk)
            s_prev = jnp.where((cols >= rows) & (n > 0), _dot(q, kt_ref[ksl, pl.ds(ps, blk)]), NEG_INF)
            s_self = _dot(q, kt_ref[ksl, pl.ds(qs, blk)])
            s_next = jnp.where((cols <= rows) & (n < n_blk - 1), _dot(q, kt_ref[ksl, pl.ds(ns, blk)]), NEG_INF)
            s_ctx = _dot(q, kct_ref[ksl, :])
            o = _softmax_pv([s_prev, s_self, s_next, s_ctx],
                            [v_ref[pl.ds(ps, blk), ksl], v_ref[pl.ds(qs, blk), ksl], v_ref[pl.ds(ns, blk), ksl], vc_ref[:, ksl]],
                            sink=sink_q[kv])
            o_ref[pl.ds(qs, blk), kv * grp * hd:(kv + 1) * grp * hd] = unstack(o, blk).astype(o_ref.dtype)
        return carry

    lax.fori_loop(0, n_blk, blk_body, 0)
    for kv in range(WA_KV_HEADS):
        ksl = slice(kv * hd, (kv + 1) * hd)
        if with_ctx:
            qc = stack(qc_ref, kv, 0, n_ctx)
            oc = _softmax_pv([_dot(qc, kct_ref[ksl, :])], [vc_ref[:, ksl]], sink=sink_col(kv, n_ctx))
            oc_ref[:, kv * grp * hd:(kv + 1) * grp * hd] = unstack(oc, n_ctx).astype(oc_ref.dtype)
        else:
            oc_ref[:, kv * grp * hd:(kv + 1) * grp * hd] = jnp.zeros((n_ctx, grp * hd), oc_ref.dtype)


def window_attention(qkv, sink, n_batch, seq, ctx_len, with_ctx):
    width = WA_HEADS * WA_HD
    ctx_blk0 = n_batch * seq // ctx_len
    kcol, vcol = width // LANES, width // LANES + 1
    return pl.pallas_call(
        functools.partial(_wa_body, seq // WINDOW, with_ctx),
        out_shape=(jax.ShapeDtypeStruct((n_batch * seq, width), BF16), jax.ShapeDtypeStruct((n_batch * ctx_len, width), BF16)),
        grid=(n_batch,),
        in_specs=[pl.BlockSpec(memory_space=pltpu.SMEM),
                  pl.BlockSpec((seq, width), lambda b: (b, 0)),
                  pl.BlockSpec((seq, LANES), lambda b: (b, kcol)),
                  pl.BlockSpec((seq, LANES), lambda b: (b, vcol)),
                  pl.BlockSpec((ctx_len, width), lambda b: (ctx_blk0 + b, 0)),
                  pl.BlockSpec((ctx_len, LANES), lambda b: (ctx_blk0 + b, kcol)),
                  pl.BlockSpec((ctx_len, LANES), lambda b: (ctx_blk0 + b, vcol))],
        out_specs=(pl.BlockSpec((seq, width), lambda b: (b, 0)), pl.BlockSpec((ctx_len, width), lambda b: (b, 0))),
        scratch_shapes=[pltpu.VMEM((LANES, seq), BF16), pltpu.VMEM((LANES, ctx_len), BF16)],
        compiler_params=_params("parallel"),
        name="window_attention",
    )(sink, qkv, qkv, qkv, qkv, qkv, qkv)


CONV_ROWS = 256
CONV_HALO = 8


def _conv_silu_cols(src_ref, src_cols, w_ref, w_cols, pad_ref, dst_ref, dst_row0, dst_cols):
    n_rows = src_ref.shape[0]
    n_blk = n_rows // CONV_ROWS
    zeros = jnp.zeros((CONV_HALO, LANES), F32)
    pad_ref[0:CONV_HALO, :] = zeros
    pad_ref[CONV_HALO + n_rows:2 * CONV_HALO + n_rows, :] = zeros

    def copy(i, carry):
        r0 = pl.multiple_of(i * CONV_ROWS, CONV_ROWS)
        pad_ref[pl.ds(CONV_HALO + r0, CONV_ROWS), :] = src_ref[pl.ds(r0, CONV_ROWS), src_cols]
        return carry

    lax.fori_loop(0, n_blk, copy, 0)
    first = CONV_HALO - CONV_K // 2

    def conv(i, carry):
        r0 = pl.multiple_of(i * CONV_ROWS, CONV_ROWS)
        win = pad_ref[pl.ds(r0, CONV_ROWS + 2 * CONV_HALO), :]
        acc = win[first:first + CONV_ROWS] * w_ref[0:1, w_cols]
        for j in range(1, CONV_K):
            acc = acc + win[first + j:first + j + CONV_ROWS] * w_ref[j:j + 1, w_cols]
        dst_ref[pl.ds(dst_row0 + r0, CONV_ROWS), dst_cols] = _silu(acc)
        return carry

    lax.fori_loop(0, n_blk, conv, 0)


def _tri(upper):
    r = lax.broadcasted_iota(jnp.int32, (CHUNK, CHUNK), 0)
    c = lax.broadcasted_iota(jnp.int32, (CHUNK, CHUNK), 1)
    return (r <= c) if upper else (r >= c)


def _rows_as_lanes(a):
    return jnp.concatenate([a, jnp.zeros((LANES - CHUNK, LANES), F32)], axis=0).T[:, 0:CHUNK]


def _softplus(x):
    return jnp.maximum(x, 0.0) + jnp.log1p(jnp.exp(-jnp.abs(x)))


NEUMANN_STEPS = 5
DN_GROUP = 4
DN_BATCH = 3

def _dn_body(q_ref, k_ref, v_ref, g_ref, qc_ref, kc_ref, vc_ref, gc_ref, wq_ref, wk_ref, wv_ref, alv_ref, dtv_ref,
             o_ref, oc_ref, pad_ref, qs_ref, ks_ref, vs_ref, gb_ref, osum_ref, qe_ref, kb_ref, dec_ref):
    n_ctx, n_lat = qc_ref.shape[0], q_ref.shape[0]
    n_rows = n_ctx + n_lat
    head = pl.program_id(1)
    full = slice(0, LANES)
    for src_c, src, w_ref, dst in ((qc_ref, q_ref, wq_ref, qs_ref), (kc_ref, k_ref, wk_ref, ks_ref), (vc_ref, v_ref, wv_ref, vs_ref)):
        _conv_silu_cols(src_c, full, w_ref, full, pad_ref, dst, 0, full)
        _conv_silu_cols(src, full, w_ref, full, pad_ref, dst, n_ctx, full)

    def l2(i, carry):
        r = pl.multiple_of(i * CONV_ROWS, CONV_ROWS)
        q = qs_ref[pl.ds(r, CONV_ROWS), :]
        qs_ref[pl.ds(r, CONV_ROWS), :] = q * lax.rsqrt(jnp.sum(q * q, axis=-1, keepdims=True) + EPS) * (DN_DK ** -0.5)
        k = ks_ref[pl.ds(r, CONV_ROWS), :]
        ks_ref[pl.ds(r, CONV_ROWS), :] = k * lax.rsqrt(jnp.sum(k * k, axis=-1, keepdims=True) + EPS)
        return carry

    lax.fori_loop(0, n_rows // CONV_ROWS, l2, 0)

    def gates(raw):
        lane = lax.broadcasted_iota(jnp.int32, raw.shape, 1)
        return jnp.where(lane < 2 * DN_HEADS, jax.nn.sigmoid(raw), alv_ref[...] * _softplus(raw + dtv_ref[...]))

    gb_ref[0:n_ctx, :] = gates(gc_ref[...])
    gb_ref[n_ctx:, :] = gates(g_ref[...])
    osum_ref[...] = jnp.zeros(osum_ref.shape, F32)
    grp = DN_GROUP * CHUNK
    nb = DN_BATCH
    n_chunks = n_rows // CHUNK
    n_ctx_chunks = n_ctx // CHUNK
    lane = lax.broadcasted_iota(jnp.int32, (nb, grp, LANES), 2)
    row_i = lax.broadcasted_iota(jnp.int32, (2 * nb, grp, grp), 1)
    col_i = lax.broadcasted_iota(jnp.int32, (2 * nb, grp, grp), 2)
    rev = lax.broadcasted_iota(jnp.int32, (2 * nb, grp, grp), 0) >= nb
    tri = (rev & (row_i <= col_i)) | (jnp.logical_not(rev) & (row_i >= col_i))
    keep = ((row_i // CHUNK) == (col_i // CHUNK)) & tri
    strict = keep & (row_i != col_i)
    ones = jnp.where(keep, 1.0, 0.0).astype(BF16)

    def bdot(a, b):
        return lax.dot_general(a, b, (((2,), (1,)), ((0,), (0,))), preferred_element_type=F32)

    def bdot_nt(a, b):
        return lax.dot_general(a, b, (((2,), (2,)), ((0,), (0,))), preferred_element_type=F32)

    def bdot_tn(a, b):
        return lax.dot_general(a, b, (((1,), (1,)), ((0,), (0,))), preferred_element_type=F32)

    def both(x):
        return jnp.concatenate([x, x], axis=0)

    def trip(ti, carry):
        r = pl.multiple_of(ti * (nb * grp), nb * grp)
        load = lambda ref: ref[pl.ds(r, nb * grp), :].reshape(nb, grp, LANES)
        q, k, v, gb = load(qs_ref), load(ks_ref), load(vs_ref), load(gb_ref)
        pick = lambda l: jnp.sum(jnp.where(lane == l, gb, 0.0), axis=-1, keepdims=True)
        beta = jnp.concatenate([pick(head), pick(DN_HEADS + head)], axis=0)
        g = jnp.broadcast_to(jnp.concatenate([pick(2 * DN_HEADS + head), pick(3 * DN_HEADS + head)], axis=0), (2 * nb, grp, LANES))
        g_hi = g.astype(BF16)
        g_r = g - g_hi.astype(F32)
        g_mid = g_r.astype(BF16)
        g_lo = (g_r - g_mid.astype(F32)).astype(BF16)
        cum = bdot(ones, g_hi) + bdot(ones, g_mid) + bdot(ones, g_lo)
        cum_row = jnp.stack([cum[b].T[0:1, :] for b in range(2 * nb)], axis=0)
        decay = jnp.where(keep, jnp.exp(jnp.concatenate([cum] * (grp // LANES), axis=2) - cum_row), 0.0)
        q2, k2, v2 = both(q), both(k), both(v)
        k16 = k2.astype(BF16)
        kb = k2 * beta
        a = jnp.where(strict, bdot_nt(kb.astype(BF16), k16) * decay, 0.0)
        attn16 = (bdot_nt(q2.astype(BF16), k16) * decay).astype(BF16)
        e_cum = jnp.exp(cum)
        rhs = jnp.concatenate([v2 * beta, kb * e_cum], axis=2)
        n = -a
        a16 = a.astype(BF16)
        p = bdot(a16, a16)
        for it in range(NEUMANN_STEPS):
            p16 = p.astype(BF16)
            n = n + p + bdot(n.astype(BF16), p16)
            if it + 1 < NEUMANN_STEPS:
                p = bdot(p16, p16)
        sol16 = (rhs + bdot(n.astype(BF16), rhs.astype(BF16))).astype(BF16)
        aw = bdot(attn16, sol16)
        o0 = aw[0:nb, :, 0:DN_DK] + aw[nb:, :, 0:DN_DK]
        osum_ref[pl.ds(r, nb * grp), :] += o0.reshape(nb * grp, DN_DK)
        qe = q2 * e_cum - aw[:, :, DN_DK:]
        qe_ref[0, pl.ds(r, nb * grp), :] = qe[0:nb].reshape(nb * grp, DN_DK)
        qe_ref[1, pl.ds(r, nb * grp), :] = qe[nb:].reshape(nb * grp, DN_DK)
        cpt = nb * DN_GROUP
        cum_c = cum.reshape(2 * cpt, CHUNK, LANES)
        edge = jnp.concatenate([cum_c[0:cpt, CHUNK - 1:CHUNK, :], cum_c[cpt:, 0:1, :]], axis=0)
        kdec16 = (k2.reshape(2 * cpt, CHUNK, LANES) * jnp.exp(edge - cum_c)).astype(BF16)
        bk = bdot_tn(kdec16, sol16.reshape(2 * cpt, CHUNK, 2 * DN_DK))
        dec = jnp.broadcast_to(jnp.exp(edge), (2 * cpt, 8, LANES))
        c0 = ti * cpt
        for d in range(2):
            kb_ref[d, pl.ds(c0, cpt)] = bk[d * cpt:(d + 1) * cpt]
            dec_ref[d, pl.ds(c0, cpt)] = dec[d * cpt:(d + 1) * cpt]
        return carry

    lax.fori_loop(0, n_chunks // (DN_GROUP * nb), trip, 0)

    def step(i, states):
        new = []
        for d in range(2):
            c = i if d == 0 else jnp.where(i < n_ctx_chunks, n_ctx_chunks - 1 - i, n_chunks + n_ctx_chunks - 1 - i)
            r = pl.multiple_of(c * CHUNK, CHUNK)
            s16 = states[d].astype(BF16)
            osum_ref[pl.ds(r, CHUNK), :] += _dot(qe_ref[d, pl.ds(r, CHUNK), :].astype(BF16), s16)
            bk = kb_ref[d, c]
            new.append(states[d] * dec_ref[d, c][0:1, :] - _dot(bk[:, DN_DK:].astype(BF16), s16) + bk[:, 0:DN_DK])
        return tuple(new)

    zero = jnp.zeros((DN_DK, DN_DK), F32)
    lax.fori_loop(0, n_chunks, step, (zero, zero))
    oc_ref[...] = osum_ref[0:n_ctx, :]
    o_ref[...] = osum_ref[n_ctx:, :]


def dn_gate_vectors(a_log, dt_bias):
    def vec(v):
        return jnp.pad(v.astype(F32).reshape(1, -1), ((0, 0), (2 * DN_HEADS, LANES - 4 * DN_HEADS)))
    return vec(-jnp.exp(a_log.astype(F32))), vec(dt_bias)


def deltanet(dqkv, dg, conv_w, alv, dtv, n_batch, seq, ctx_len):
    nh = DN_HEADS
    ctx_blk0 = n_batch * seq // ctx_len
    n_rows = seq + ctx_len
    lat = lambda off: pl.BlockSpec((seq, LANES), lambda b, h: (b, off + h))
    ctx = lambda off: pl.BlockSpec((ctx_len, LANES), lambda b, h: (ctx_blk0 + b, off + h))
    cw = lambda off: pl.BlockSpec((CONV_K, LANES), lambda b, h: (0, off + h))
    vec = pl.BlockSpec((1, LANES), lambda b, h: (0, 0))
    seq_buf = pltpu.VMEM((n_rows, LANES), F32)
    return pl.pallas_call(
        _dn_body,
        out_shape=(jax.ShapeDtypeStruct((n_batch * seq, nh * DN_DK), F32), jax.ShapeDtypeStruct((n_batch * ctx_len, nh * DN_DK), F32)),
        grid=(n_batch, nh),
        in_specs=[lat(0), lat(nh), lat(2 * nh), pl.BlockSpec((seq, LANES), lambda b, h: (b, 0)),
                  ctx(0), ctx(nh), ctx(2 * nh), pl.BlockSpec((ctx_len, LANES), lambda b, h: (ctx_blk0 + b, 0)),
                  cw(0), cw(nh), cw(2 * nh), vec, vec],
        out_specs=(pl.BlockSpec((seq, LANES), lambda b, h: (b, h)), pl.BlockSpec((ctx_len, LANES), lambda b, h: (b, h))),
        scratch_shapes=[pltpu.VMEM((seq + 2 * CONV_HALO, LANES), F32), seq_buf, seq_buf, seq_buf, seq_buf, seq_buf,
                        pltpu.VMEM((2, n_rows, LANES), F32),
                        pltpu.VMEM((2, n_rows // CHUNK, DN_DK, 2 * DN_DK), F32),
                        pltpu.VMEM((2, n_rows // CHUNK, 8, LANES), F32)],
        compiler_params=_params("parallel", "arbitrary"),
        name="deltanet",
    )(dqkv, dqkv, dqkv, dg, dqkv, dqkv, dqkv, dg, conv_w, conv_w, conv_w, alv, dtv)


def _ssd_body(xbc_ref, dt_ref, xbc_c_ref, dt_c_ref, cw_ref, av_ref, dtb_ref, dsk_ref, y_ref, yc_ref,
              pad_ref, s_ref, dts_ref, da_ref, hs_ref):
    n_ctx, n_lat = xbc_c_ref.shape[0], xbc_ref.shape[0]
    inner = SSD_HEADS * SSD_HD
    hpg = SSD_HEADS // SSD_GROUPS
    gw = hpg * SSD_HD
    for cb in range(xbc_ref.shape[1] // LANES):
        cols = slice(cb * LANES, (cb + 1) * LANES)
        _conv_silu_cols(xbc_c_ref, cols, cw_ref, cols, pad_ref, s_ref, 0, cols)
        _conv_silu_cols(xbc_ref, cols, cw_ref, cols, pad_ref, s_ref, n_ctx, cols)
    dts_ref[0:n_ctx, :] = _softplus(dt_c_ref[...] + dtb_ref[...])
    dts_ref[n_ctx:, :] = _softplus(dt_ref[...] + dtb_ref[...])
    da_ref[...] = dts_ref[...] * av_ref[...]
    yc_ref[...] = s_ref[0:n_ctx, 0:inner] * dsk_ref[...]
    y_ref[...] = s_ref[n_ctx:, 0:inner] * dsk_ref[...]
    hs_ref[...] = jnp.zeros(hs_ref.shape, F32)

    def chunk(d, c, row0, out_ref):
        r = pl.multiple_of(row0 + c * CHUNK, CHUNK)
        ro = pl.multiple_of(c * CHUNK, CHUNK)
        keep = _tri(upper=(d == 1))
        acum = jnp.dot(keep.astype(F32), da_ref[pl.ds(r, CHUNK), :], precision=HIGHEST, preferred_element_type=F32)
        acum_t = _rows_as_lanes(acum)
        edge = acum[CHUNK - 1:CHUNK, :] if d == 0 else acum[0:1, :]
        dt = dts_ref[pl.ds(r, CHUNK), :]
        for g in range(SSD_GROUPS):
            lanes = [d * SSD_HEADS + g * hpg + h for h in range(hpg)]

            def expand(v, rows):
                return jnp.concatenate([jnp.broadcast_to(v[:, l:l + 1], (rows, SSD_HD)) for l in lanes], axis=1)

            x4 = s_ref[pl.ds(r, CHUNK), g * gw:(g + 1) * gw]
            bm = s_ref[pl.ds(r, CHUNK), inner + g * SSD_STATE:inner + (g + 1) * SSD_STATE].astype(BF16)
            cm = s_ref[pl.ds(r, CHUNK), inner + (SSD_GROUPS + g) * SSD_STATE:inner + (SSD_GROUPS + g + 1) * SSD_STATE].astype(BF16)
            cb_mat = _dot_nt(cm, bm)
            a4 = expand(acum, CHUNK)
            e4 = expand(edge, 1)
            xd = x4 * expand(dt, CHUNK)
            prev = hs_ref[d, g]
            y = _dot_nt(cm, prev.astype(BF16)) * jnp.exp(a4)
            diag = []
            for h, l in enumerate(lanes):
                lmat = jnp.where(keep, jnp.exp(acum[:, l:l + 1] - acum_t[l:l + 1, :]), 0.0)
                diag.append(_dot((cb_mat * lmat).astype(BF16), xd[:, h * SSD_HD:(h + 1) * SSD_HD].astype(BF16)))
            y = y + jnp.concatenate(diag, axis=1)
            out_ref[pl.ds(ro, CHUNK), g * gw:(g + 1) * gw] += y
            states = _dot_tn((xd * jnp.exp(e4 - a4)).astype(BF16), bm)
            decay = jnp.concatenate([jnp.broadcast_to(jnp.exp(edge[:, l:l + 1]), (SSD_HD, SSD_STATE)) for l in lanes], axis=0)
            hs_ref[d, g] = prev * decay + states

    def segment(n_chunks, row0, out_ref):
        def body(i, carry):
            chunk(0, i, row0, out_ref)
            chunk(1, n_chunks - 1 - i, row0, out_ref)
            return carry
        lax.fori_loop(0, n_chunks, body, 0)

    segment(n_ctx // CHUNK, 0, yc_ref)
    segment(n_lat // CHUNK, n_ctx, y_ref)


def ssd(xbc, dt, conv_w, a_vec, dtb_vec, skip_vec, n_batch, seq, ctx_len):
    width = xbc.shape[1]
    inner = SSD_HEADS * SSD_HD
    ctx_blk0 = n_batch * seq // ctx_len
    n_rows = seq + ctx_len
    return pl.pallas_call(
        _ssd_body,
        out_shape=(jax.ShapeDtypeStruct((n_batch * seq, inner), F32), jax.ShapeDtypeStruct((n_batch * ctx_len, inner), F32)),
        grid=(n_batch,),
        in_specs=[pl.BlockSpec((seq, width), lambda b: (b, 0)), pl.BlockSpec((seq, LANES), lambda b: (b, 0)),
                  pl.BlockSpec((ctx_len, width), lambda b: (ctx_blk0 + b, 0)), pl.BlockSpec((ctx_len, LANES), lambda b: (ctx_blk0 + b, 0)),
                  _resident(conv_w.shape), _resident(a_vec.shape), _resident(dtb_vec.shape), _resident(skip_vec.shape)],
        out_specs=(pl.BlockSpec((seq, inner), lambda b: (b, 0)), pl.BlockSpec((ctx_len, inner), lambda b: (b, 0))),
        scratch_shapes=[pltpu.VMEM((seq + 2 * CONV_HALO, LANES), F32), pltpu.VMEM((n_rows, width), F32),
                        pltpu.VMEM((n_rows, LANES), F32), pltpu.VMEM((n_rows, LANES), F32),
                        pltpu.VMEM((2, SSD_GROUPS, (SSD_HEADS // SSD_GROUPS) * SSD_HD, SSD_STATE), F32)],
        compiler_params=_params("parallel"),
        name="ssd",
    )(xbc, dt, xbc, dt, conv_w, a_vec, dtb_vec, skip_vec)


def ssd_vectors(a_log, dt_bias, d_skip):
    def vec(v):
        return jnp.pad(v.astype(F32).reshape(1, -1), ((0, 0), (0, LANES - 2 * SSD_HEADS)))
    return vec(-jnp.exp(a_log.astype(F32))), vec(dt_bias), jnp.repeat(d_skip.astype(F32), SSD_HD)[None, :]


def kernel(x, c, ctx, c_ctx, ada_w, ada_b, norm_w, ev_w_in, ev_w_out, na_rpb, dn_conv, dn_a_log, dn_dt_bias, dn_norm, ffn_w1, ffn_w3, ffn_w2, od_w_in, od_w_out, wa_sink, ssd_conv, ssd_a_log, ssd_dt_bias, ssd_d, ssd_norm, moe_router, moe_w1, moe_w3, moe_w2):
    n_batch, seq, d = x.shape
    ctx_len = ctx.shape[1]
    depth = ada_w.shape[0]
    n_lat, n_ctx = n_batch * seq, n_batch * ctx_len
    n_cond = -(-(n_batch + 1) // 8) * 8
    cond = jnp.concatenate([c, c_ctx[None, :], jnp.zeros((n_cond - n_batch - 1, d), F32)], axis=0)
    mod_all = ada_modulation(cond, ada_w, ada_b)
    x_all = jnp.concatenate([x.reshape(n_lat, d), ctx.reshape(n_ctx, d)], axis=0)
    cos_t, sin_t = rope_tables(seq)
    moe_w1_16, moe_w3_16, moe_w2_16 = moe_w1.astype(BF16), moe_w3.astype(BF16), moe_w2.astype(BF16)
    for layer in range(depth):
        i = layer // 2
        n_tok = n_lat + n_ctx if layer < depth - 1 else n_lat
        mod = mod_all[layer, :n_batch + 1].reshape(n_batch + 1, 6, d)
        nw = norm_w[layer]
        if layer % 2 == 0:
            att, dqkv, dz, dg = proj_in_even(x_all, mod, nw, prep_w_in_even(ev_w_in[i]), n_batch, seq)
            o_a, oc_a = na_attention(att, na_bias_tables(na_rpb[i], seq // GRID_W), n_batch, seq, ctx_len)
            alv, dtv = dn_gate_vectors(dn_a_log[i], dn_dt_bias[i])
            o_b, oc_b = deltanet(dqkv, dg, dn_conv[i], alv, dtv, n_batch, seq, ctx_len)
            x_all = proj_out(o_a, oc_a, o_b, oc_b, dz,
                             jnp.tile(dn_norm[i], DN_HEADS)[None, :], ev_w_out[i].astype(BF16), x_all, mod, nw,
                             n_batch, seq, n_tok, DN_DK, False)
            x_all = ffn_dense(x_all, mod, nw, ffn_w1[i].astype(BF16), ffn_w3[i].astype(BF16), ffn_w2[i].astype(BF16),
                              n_batch, seq, n_tok)
        else:
            qkv, z, xbc, dt = proj_in_odd(x_all, mod, nw, prep_w_in_odd(od_w_in[i]), cos_t, sin_t, n_batch, seq)
            o_c, oc_c = window_attention(qkv, wa_sink[i], n_batch, seq, ctx_len, n_tok > n_lat)
            a_vec, dtb_vec, skip_vec = ssd_vectors(ssd_a_log[i], ssd_dt_bias[i], ssd_d[i])
            y_d, yc_d = ssd(xbc, dt, ssd_conv[i], a_vec, dtb_vec, skip_vec, n_batch, seq, ctx_len)
            x_all = proj_out(o_c, oc_c, y_d, yc_d, z,
                             ssd_norm[i][None, :], od_w_out[i].astype(BF16), x_all, mod, nw,
                             n_batch, seq, n_tok, SSD_HEADS * SSD_HD // SSD_GROUPS, True)
            router = jnp.pad(moe_router[i], ((0, 0), (0, LANES - N_EXPERTS)))
            x_all = moe_layer(x_all, mod, nw, router, moe_w1_16, moe_w3_16, moe_w2_16, i, n_batch, seq, n_tok)
    return x_all[:n_lat].reshape(n_batch, seq, d)
```

```python
import functools
import math

import jax
import jax.numpy as jnp
import numpy as np
from jax import lax
from jax.experimental import pallas as pl
from jax.experimental.pallas import tpu as pltpu

F32 = jnp.float32
BF16 = jnp.bfloat16
HIGHEST = lax.Precision.HIGHEST

EPS = 1e-6
NEG_INF = -1e30
GRID_W = 64
CHUNK = 64
CONV_K = 5
ROPE_BASE = 10000.0
NA_HEADS, NA_HD, NA_WIN_ROWS, NA_WIN_COLS = 8, 64, 8, 16
DN_HEADS, DN_DK = 4, 128
WA_HEADS, WA_KV_HEADS, WA_HD, WINDOW = 8, 2, 64, 128
WA_GROUP = WA_HEADS // WA_KV_HEADS
SSD_HEADS, SSD_HD, SSD_GROUPS, SSD_STATE = 8, 64, 2, 128
N_EXPERTS, TOP_K = 8, 2

LANES = 128
VMEM_LIMIT = 56 * 1024 * 1024
TOK_TILE = 512


def _params(*sem):
    return pltpu.CompilerParams(dimension_semantics=sem, vmem_limit_bytes=VMEM_LIMIT)


def _resident(shape):
    nd = len(shape)
    return pl.BlockSpec(shape, lambda *_: (0,) * nd, pipeline_mode=pl.Buffered(1))


def _silu(x):
    return x * jax.nn.sigmoid(x)


def _rms(x, w):
    return x * lax.rsqrt(jnp.mean(x * x, axis=-1, keepdims=True) + EPS) * w


def _dot(a, b):
    return jnp.dot(a, b, preferred_element_type=F32)


def _dot_nt(a, b):
    return lax.dot_general(a, b, (((1,), (1,)), ((), ())), preferred_element_type=F32)


def _dot_tn(a, b):
    return lax.dot_general(a, b, (((0,), (0,)), ((), ())), preferred_element_type=F32)


def _ada_body(s_ref, w_ref, b_ref, o_ref):
    s = _silu(s_ref[...])
    o_ref[0] = jnp.dot(s, w_ref[0], precision=HIGHEST, preferred_element_type=F32) + b_ref[0]


def ada_modulation(cond, ada_w, ada_b):
    depth, d, n = ada_w.shape
    r = cond.shape[0]
    tn = 1536
    return pl.pallas_call(
        _ada_body,
        out_shape=jax.ShapeDtypeStruct((depth, r, n), F32),
        grid=(depth, n // tn),
        in_specs=[
            pl.BlockSpec((r, d), lambda l, j: (0, 0)),
            pl.BlockSpec((1, d, tn), lambda l, j: (l, 0, j)),
            pl.BlockSpec((1, 1, tn), lambda l, j: (l, 0, j)),
        ],
        out_specs=pl.BlockSpec((1, r, tn), lambda l, j: (l, 0, j)),
        compiler_params=_params("arbitrary", "arbitrary"),
        name="ada_modulation",
    )(cond, ada_w, ada_b.reshape(depth, 1, n))


def _mod_rows(n_lat_tiles, tiles_per_batch, n_batch):
    def index(i):
        return (jnp.minimum(i // tiles_per_batch, n_batch), 0, 0)
    del n_lat_tiles
    return index


def _norm_mod(x, mod_ref, nw_ref, row):
    y = _rms(x, nw_ref[row:row + 1, :])
    return y * (1.0 + mod_ref[0, row + 1:row + 2, :]) + mod_ref[0, row:row + 1, :]


def _proj_even_body(x_ref, mod_ref, nw_ref, w_ref, att_ref, dqkv_ref, dz_ref, dg_ref):
    hb = _norm_mod(x_ref[...], mod_ref, nw_ref, 0).astype(BF16)
    att_ref[...] = _dot(hb, w_ref[:, 0:1536]).astype(BF16)
    dqkv_ref[...] = _dot(hb, w_ref[:, 1536:3072])
    dz_ref[...] = _dot(hb, w_ref[:, 3072:3584])
    dg_ref[...] = _dot(hb, w_ref[:, 3584:3712])


def _swap16(x):
    lane = lax.broadcasted_iota(jnp.int32, x.shape, 1)
    up = pltpu.roll(x, LANES - 16, 1)
    down = pltpu.roll(x, 16, 1)
    return jnp.where((lane % 32) < 16, up, down)


def _proj_odd_body(x_ref, mod_ref, nw_ref, w_ref, cos_ref, sin_ref, qkv_ref, z_ref, xbc_ref, dt_ref):
    hb = _norm_mod(x_ref[...], mod_ref, nw_ref, 0).astype(BF16)
    cos, sin = cos_ref[...], sin_ref[...]
    for c in range(5):
        t = _dot(hb, w_ref[:, c * LANES:(c + 1) * LANES])
        qkv_ref[:, c * LANES:(c + 1) * LANES] = (t * cos + _swap16(t) * sin).astype(BF16)
    qkv_ref[:, 640:768] = _dot(hb, w_ref[:, 640:768]).astype(BF16)
    z_ref[...] = _dot(hb, w_ref[:, 768:1280])
    xbc_ref[...] = _dot(hb, w_ref[:, 1280:2304])
    dt_ref[...] = _dot(hb, w_ref[:, 2304:2432])


def _pad_cols(w, n):
    return jnp.pad(w, ((0, 0), (0, n - w.shape[1])))


def prep_w_in_even(w):
    q_scale = jnp.concatenate([jnp.full((NA_HEADS * NA_HD,), NA_HD ** -0.5, F32), jnp.ones((w.shape[1] - NA_HEADS * NA_HD,), F32)])
    return _pad_cols(w * q_scale[None, :], 3712).astype(BF16)


def prep_w_in_odd(w):
    q_scale = jnp.concatenate([jnp.full((WA_HEADS * WA_HD,), WA_HD ** -0.5, F32), jnp.ones((w.shape[1] - WA_HEADS * WA_HD,), F32)])
    return _pad_cols(w * q_scale[None, :], 2432).astype(BF16)


def _tok_spec(width, tm=TOK_TILE):
    return pl.BlockSpec((tm, width), lambda i: (i, 0))


def proj_in_even(x_all, mod, nw, w, n_batch, seq):
    t_all, d = x_all.shape
    tm = TOK_TILE
    tpb = seq // tm
    return pl.pallas_call(
        _proj_even_body,
        out_shape=(jax.ShapeDtypeStruct((t_all, 1536), BF16), jax.ShapeDtypeStruct((t_all, 1536), F32),
                   jax.ShapeDtypeStruct((t_all, 512), F32), jax.ShapeDtypeStruct((t_all, LANES), F32)),
        grid=(t_all // tm,),
        in_specs=[_tok_spec(d), pl.BlockSpec((1, 6, d), _mod_rows(None, tpb, n_batch)),
                  _resident(nw.shape), _resident(w.shape)],
        out_specs=(_tok_spec(1536), _tok_spec(1536), _tok_spec(512), _tok_spec(LANES)),
        compiler_params=_params("parallel"),
        name="proj_in_even",
    )(x_all, mod, nw, w)


def proj_in_odd(x_all, mod, nw, w, cos_t, sin_t, n_batch, seq):
    t_all, d = x_all.shape
    tm = TOK_TILE
    tpb = seq // tm

    def rope_rows(i):
        return (jnp.where(i < n_batch * tpb, i % tpb, tpb), 0)

    return pl.pallas_call(
        _proj_odd_body,
        out_shape=(jax.ShapeDtypeStruct((t_all, 768), BF16), jax.ShapeDtypeStruct((t_all, 512), F32),
                   jax.ShapeDtypeStruct((t_all, 1024), F32), jax.ShapeDtypeStruct((t_all, LANES), F32)),
        grid=(t_all // tm,),
        in_specs=[_tok_spec(d), pl.BlockSpec((1, 6, d), _mod_rows(None, tpb, n_batch)),
                  _resident(nw.shape), _resident(w.shape),
                  pl.BlockSpec((tm, LANES), rope_rows), pl.BlockSpec((tm, LANES), rope_rows)],
        out_specs=(_tok_spec(768), _tok_spec(512), _tok_spec(1024), _tok_spec(LANES)),
        compiler_params=_params("parallel"),
        name="proj_in_odd",
    )(x_all, mod, nw, w, cos_t, sin_t)


def rope_tables(seq, tm=TOK_TILE):
    half = WA_HD // 4
    freqs = ROPE_BASE ** (-jnp.arange(half, dtype=F32) / half)
    t = jnp.arange(seq)
    ang_r = (t // GRID_W).astype(F32)[:, None] * freqs[None, :]
    ang_c = (t % GRID_W).astype(F32)[:, None] * freqs[None, :]
    cos = jnp.concatenate([jnp.cos(ang_r)] * 2 + [jnp.cos(ang_c)] * 2, axis=-1)
    sin = jnp.concatenate([-jnp.sin(ang_r), jnp.sin(ang_r), -jnp.sin(ang_c), jnp.sin(ang_c)], axis=-1)
    cos = jnp.concatenate([jnp.tile(cos, (1, 2)), jnp.ones((tm, LANES), F32)], axis=0)
    sin = jnp.concatenate([jnp.tile(sin, (1, 2)), jnp.zeros((tm, LANES), F32)], axis=0)
    return cos, sin


def _proj_out_body(group, gate_first, n_lat_tiles, a_ref, ac_ref, r_ref, rc_ref, z_ref, gw_ref, w_ref, x_ref, mod_ref,
                   nw_ref, o_ref):
    is_ctx = pl.program_id(0) >= n_lat_tiles
    a = jnp.where(is_ctx, ac_ref[...], a_ref[...])
    r = jnp.where(is_ctx, rc_ref[...], r_ref[...])
    gate = _silu(z_ref[...])
    if gate_first:
        r = r * gate
    parts = []
    for c in range(r.shape[1] // group):
        rc = r[:, c * group:(c + 1) * group]
        parts.append(_rms(rc, gw_ref[:, c * group:(c + 1) * group]))
    r = jnp.concatenate(parts, axis=1)
    if not gate_first:
        r = r * gate
    half = a.shape[1]
    y = _dot(a, w_ref[0:half, :]) + _dot(r.astype(BF16), w_ref[half:, :])
    o_ref[...] = x_ref[...] + mod_ref[0, 2:3, :] * _rms(y, nw_ref[1:2, :])


def proj_out(att, att_c, rec, rec_c, z, gate_w, w_out, x_all, mod, nw, n_batch, seq, n_tok, group, gate_first):
    d = x_all.shape[1]
    tm = TOK_TILE
    tpb = seq // tm
    half = att.shape[1]
    n_lat_tiles = att.shape[0] // tm
    lat = pl.BlockSpec((tm, half), lambda i: (jnp.minimum(i, n_lat_tiles - 1), 0))
    ctx = pl.BlockSpec((tm, half), lambda i: (jnp.maximum(i - n_lat_tiles, 0), 0))
    return pl.pallas_call(
        functools.partial(_proj_out_body, group, gate_first, n_lat_tiles),
        out_shape=jax.ShapeDtypeStruct((n_tok, d), F32),
        grid=(n_tok // tm,),
        in_specs=[lat, ctx, lat, ctx, _tok_spec(half), _resident(gate_w.shape),
                  _resident(w_out.shape), _tok_spec(d), pl.BlockSpec((1, 6, d), _mod_rows(None, tpb, n_batch)),
                  _resident(nw.shape)],
        out_specs=_tok_spec(d),
        compiler_params=_params("parallel"),
        name="proj_out",
    )(att, att_c, rec, rec_c, z, gate_w, w_out, x_all, mod, nw)


def _ffn_dense_body(n_split, x_ref, mod_ref, nw_ref, w1_ref, w3_ref, w2_ref, o_ref):
    x = x_ref[...]
    y = _rms(x, nw_ref[2:3, :])
    hb = (y * (1.0 + mod_ref[0, 4:5, :]) + mod_ref[0, 3:4, :]).astype(BF16)
    ff = w1_ref.shape[1]
    step = ff // n_split
    acc = None
    for c in range(n_split):
        sl = slice(c * step, (c + 1) * step)
        act = (_silu(_dot(hb, w1_ref[:, sl])) * _dot(hb, w3_ref[:, sl])).astype(BF16)
        part = _dot(act, w2_ref[sl, :])
        acc = part if acc is None else acc + part
    o_ref[...] = x + mod_ref[0, 5:6, :] * _rms(acc, nw_ref[3:4, :])


def ffn_dense(x_all, mod, nw, w1, w3, w2, n_batch, seq, n_tok):
    d = x_all.shape[1]
    tm = TOK_TILE
    tpb = seq // tm
    return pl.pallas_call(
        functools.partial(_ffn_dense_body, 2),
        out_shape=jax.ShapeDtypeStruct((n_tok, d), F32),
        grid=(n_tok // tm,),
        in_specs=[_tok_spec(d), pl.BlockSpec((1, 6, d), _mod_rows(None, tpb, n_batch)), _resident(nw.shape),
                  _resident(w1.shape), _resident(w3.shape), _resident(w2.shape)],
        out_specs=_tok_spec(d),
        compiler_params=_params("parallel"),
        name="ffn_dense",
    )(x_all, mod, nw, w1, w3, w2)


MOE_TILE = 1024
MOE_FF_TILE = 896


def _moe_route_body(x_ref, mod_ref, nw_ref, r_ref, h_ref, idx_ref, gate_ref):
    y = _rms(x_ref[...], nw_ref[2:3, :])
    h = y * (1.0 + mod_ref[0, 4:5, :]) + mod_ref[0, 3:4, :]
    h_ref[...] = h
    logits = jnp.dot(h, r_ref[...], precision=HIGHEST, preferred_element_type=F32)
    lane = lax.broadcasted_iota(jnp.int32, logits.shape, 1)
    logits = jnp.where(lane < N_EXPERTS, logits, -jnp.inf)
    v1 = jnp.max(logits, axis=-1, keepdims=True)
    i1 = jnp.min(jnp.where(logits == v1, lane, LANES), axis=-1, keepdims=True)
    rest = jnp.where(lane == i1, -jnp.inf, logits)
    v2 = jnp.max(rest, axis=-1, keepdims=True)
    i2 = jnp.min(jnp.where(rest == v2, lane, LANES), axis=-1, keepdims=True)
    e2 = jnp.exp(v2 - v1)
    g1 = 1.0 / (1.0 + e2)
    idx_ref[...] = jnp.where(lane == 0, i1, jnp.where(lane == 1, i2, 0))
    gate_ref[...] = jnp.where(lane == 0, g1, jnp.where(lane == 1, e2 * g1, 0.0))


def moe_route(x_all, mod, nw, router, n_batch, seq, n_tok):
    d = x_all.shape[1]
    tm = TOK_TILE
    tpb = seq // tm
    return pl.pallas_call(
        _moe_route_body,
        out_shape=(jax.ShapeDtypeStruct((n_tok, d), F32), jax.ShapeDtypeStruct((n_tok, LANES), jnp.int32),
                   jax.ShapeDtypeStruct((n_tok, LANES), F32)),
        grid=(n_tok // tm,),
        in_specs=[_tok_spec(d), pl.BlockSpec((1, 6, d), _mod_rows(None, tpb, n_batch)), _resident(nw.shape),
                  _resident(router.shape)],
        out_specs=(_tok_spec(d), _tok_spec(LANES), _tok_spec(LANES)),
        compiler_params=_params("parallel"),
        name="moe_route",
    )(x_all, mod, nw, router)


SUBLANES = 8


def _for_rows(n_rows, fn):
    def body(g, carry):
        base = pl.multiple_of(g * SUBLANES, SUBLANES)
        for j in range(SUBLANES):
            fn(base + j, j % 2)
        return carry
    lax.fori_loop(0, n_rows // SUBLANES, body, 0)


def _moe_gmm_body(te_ref, nv_ref, src_ref, src_next_ref, h_ref, w1_ref, w3_ref, w2_ref, o_ref, xbuf, x16, sem):
    i, f = pl.program_id(0), pl.program_id(1)
    n_live = nv_ref[0]
    tm = x16.shape[0]
    slot = i % 2

    def row_copy(s_ref, s, r):
        return pltpu.make_async_copy(h_ref.at[pl.ds(s_ref[0, 0, r], 1), :], xbuf.at[s, pl.ds(r, 1), :], sem.at[s])

    for_rows = functools.partial(_for_rows, tm)

    @pl.when((f == 0) & (i == 0) & (n_live > 0))
    def _():
        for_rows(lambda r, lane: row_copy(src_ref, 0, r).start(priority=lane))

    @pl.when((f == 0) & (i < n_live))
    def _():
        for_rows(lambda r, lane: row_copy(src_ref, slot, r).wait())
        x16[...] = xbuf[slot].astype(BF16)

    @pl.when((f == 1) & (i + 1 < n_live))
    def _():
        for_rows(lambda r, lane: row_copy(src_next_ref, 1 - slot, r).start(priority=lane))

    @pl.when(i < n_live)
    def _():
        x = x16[...]
        act = (_silu(_dot(x, w1_ref[0, 0])) * _dot(x, w3_ref[0, 0])).astype(BF16)
        part = _dot(act, w2_ref[0, 0])

        @pl.when(f == 0)
        def _():
            o_ref[...] = part

        @pl.when(f > 0)
        def _():
            o_ref[...] += part

    @pl.when((i >= n_live) & (f == 0))
    def _():
        o_ref[...] = jnp.zeros(o_ref.shape, o_ref.dtype)


def moe_gmm(tile_expert, n_valid, src_tok, h, w1, w3, w2, layer):
    p = src_tok.shape[0]
    d = h.shape[1]
    ff = w1.shape[3]
    tm, tf = MOE_TILE, MOE_FF_TILE
    nt, nf = p // tm, ff // tf
    assert nf >= 2

    def wcol(i, f, te, nv):
        live = i < nv[0]
        return (layer, te[i], 0, jnp.where(live, f, nf - 1))

    def wrow(i, f, te, nv):
        live = i < nv[0]
        return (layer, te[i], jnp.where(live, f, nf - 1), 0)

    src_tiles = src_tok.reshape(nt, 1, tm)
    smem = lambda index: pl.BlockSpec((1, 1, tm), index, memory_space=pltpu.SMEM)
    return pl.pallas_call(
        _moe_gmm_body,
        out_shape=jax.ShapeDtypeStruct((p, d), F32),
        grid_spec=pltpu.PrefetchScalarGridSpec(
            num_scalar_prefetch=2,
            grid=(nt, nf),
            in_specs=[smem(lambda i, f, te, nv: (i, 0, 0)), smem(lambda i, f, te, nv: (jnp.minimum(i + 1, nt - 1), 0, 0)),
                      pl.BlockSpec(memory_space=pl.ANY),
                      pl.BlockSpec((1, 1, d, tf), wcol), pl.BlockSpec((1, 1, d, tf), wcol), pl.BlockSpec((1, 1, tf, d), wrow)],
            out_specs=pl.BlockSpec((tm, d), lambda i, f, te, nv: (i, 0)),
            scratch_shapes=[pltpu.VMEM((2, tm, d), F32), pltpu.VMEM((tm, d), BF16), pltpu.SemaphoreType.DMA((2,))]),
        compiler_params=_params("arbitrary", "arbitrary"),
        name="moe_gmm",
    )(tile_expert, n_valid, src_tiles, src_tiles, h, w1, w3, w2)


def moe_dispatch(idx, n_tok):
    tm = MOE_TILE
    n_assign = n_tok * TOP_K
    p = n_assign + N_EXPERTS * tm
    e_flat = idx[:, :TOP_K].reshape(n_assign)
    onehot = (e_flat[:, None] == jnp.arange(N_EXPERTS)[None, :]).astype(jnp.int32)
    counts = jnp.sum(onehot, axis=0)
    padded = ((counts + tm - 1) // tm) * tm
    ends = jnp.cumsum(padded)
    starts = ends - padded
    pos = jnp.sum(onehot * (jnp.cumsum(onehot, axis=0) - onehot + starts[None, :]), axis=1)
    src_tok = jnp.zeros((p,), jnp.int32).at[pos].set(jnp.arange(n_assign, dtype=jnp.int32) // TOP_K)
    tile_start = jnp.arange(p // tm, dtype=jnp.int32) * tm
    tile_expert = jnp.minimum(jnp.sum((tile_start[:, None] >= ends[None, :]).astype(jnp.int32), axis=1), N_EXPERTS - 1)
    n_valid = (ends[-1] // tm).astype(jnp.int32).reshape(1)
    last_live = jnp.take(tile_expert, jnp.maximum(n_valid - 1, 0))
    tile_expert = jnp.where(tile_start < ends[-1], tile_expert, last_live).astype(jnp.int32)
    return src_tok, pos.reshape(n_tok, TOP_K), tile_expert, n_valid


def _moe_combine_body(pos_ref, pos_next_ref, ys_ref, gate_ref, x_ref, mod_ref, nw_ref, o_ref, ybuf, sem):
    i = pl.program_id(0)
    n_tiles = pl.num_programs(0)
    tm = x_ref.shape[0]
    slot = i % 2

    def row_copy(p_ref, s, k, r):
        return pltpu.make_async_copy(ys_ref.at[pl.ds(p_ref[0, 0, k * tm + r], 1), :], ybuf.at[s, k, pl.ds(r, 1), :], sem.at[s])

    for_rows = functools.partial(_for_rows, tm)

    def start_tile(p_ref, s):
        def fn(r, lane):
            row_copy(p_ref, s, 0, r).start(priority=0)
            row_copy(p_ref, s, 1, r).start(priority=1)
        for_rows(fn)

    @pl.when(i == 0)
    def _():
        start_tile(pos_ref, 0)

    @pl.when(i + 1 < n_tiles)
    def _():
        start_tile(pos_next_ref, 1 - slot)

    def wait_row(r, lane):
        row_copy(pos_ref, slot, 0, r).wait()
        row_copy(pos_ref, slot, 1, r).wait()
    for_rows(wait_row)

    g = gate_ref[...]
    ffn = ybuf[slot, 0] * g[:, 0:1] + ybuf[slot, 1] * g[:, 1:2]
    o_ref[...] = x_ref[...] + mod_ref[0, 5:6, :] * _rms(ffn, nw_ref[3:4, :])


def moe_combine(ys, pos, gates, x_all, mod, nw, n_batch, seq, n_tok):
    d = x_all.shape[1]
    tm = TOK_TILE
    tpb = seq // tm
    n_tiles = n_tok // tm
    pos_tiles = pos.reshape(n_tiles, tm, TOP_K).transpose(0, 2, 1).reshape(n_tiles, 1, TOP_K * tm)
    smem = lambda index: pl.BlockSpec((1, 1, TOP_K * tm), index, memory_space=pltpu.SMEM)
    return pl.pallas_call(
        _moe_combine_body,
        out_shape=jax.ShapeDtypeStruct((n_tok, d), F32),
        grid=(n_tiles,),
        in_specs=[smem(lambda i: (i, 0, 0)), smem(lambda i: (jnp.minimum(i + 1, n_tiles - 1), 0, 0)),
                  pl.BlockSpec(memory_space=pl.ANY), _tok_spec(LANES), _tok_spec(d),
                  pl.BlockSpec((1, 6, d), _mod_rows(None, tpb, n_batch)), _resident(nw.shape)],
        out_specs=_tok_spec(d),
        scratch_shapes=[pltpu.VMEM((2, TOP_K, tm, d), F32), pltpu.SemaphoreType.DMA((2,))],
        compiler_params=_params("arbitrary"),
        name="moe_combine",
    )(pos_tiles, pos_tiles, ys, gates, x_all, mod, nw)


def moe_layer(x_all, mod, nw, router, w1, w3, w2, layer, n_batch, seq, n_tok):
    h, idx, gates = moe_route(x_all, mod, nw, router, n_batch, seq, n_tok)
    src_tok, pos, tile_expert, n_valid = moe_dispatch(idx, n_tok)
    ys = moe_gmm(tile_expert, n_valid, src_tok, h, w1, w3, w2, layer)
    return moe_combine(ys, pos, gates, x_all, mod, nw, n_batch, seq, n_tok)


NA_QROWS = 4
NA_KROWS = NA_QROWS + NA_WIN_ROWS - 1


def _na_key_start(blk, rows):
    return jnp.clip(blk * NA_QROWS - NA_WIN_ROWS // 2, 0, rows - NA_KROWS)


def na_bias_tables(rpb, rows):
    n_blk = rows // NA_QROWS
    blk = jnp.array([0, 1, n_blk - 1])
    q_row = blk[:, None] * NA_QROWS + jnp.arange(NA_QROWS)[None, :]
    k_row = _na_key_start(blk, rows)[:, None] + jnp.arange(NA_KROWS)[None, :]
    q_start = jnp.clip(q_row - NA_WIN_ROWS // 2, 0, rows - NA_WIN_ROWS)
    row_ok = (k_row[:, None, :] >= q_start[:, :, None]) & (k_row[:, None, :] < q_start[:, :, None] + NA_WIN_ROWS)
    dr = jnp.clip(k_row[:, None, :] - q_row[:, :, None] + NA_WIN_ROWS - 1, 0, 2 * NA_WIN_ROWS - 2)
    col = jnp.arange(GRID_W)
    col_start = jnp.clip(col - NA_WIN_COLS // 2, 0, GRID_W - NA_WIN_COLS)
    col_ok = (col[None, :] >= col_start[:, None]) & (col[None, :] < col_start[:, None] + NA_WIN_COLS)
    dc = jnp.clip(col[None, :] - col[:, None] + NA_WIN_COLS - 1, 0, 2 * NA_WIN_COLS - 2)
    pick_col = jax.nn.one_hot(dc, 2 * NA_WIN_COLS - 1, dtype=F32)
    bias = jnp.einsum('hpqkc,xyc->hpqkxy', rpb.astype(F32)[:, dr], pick_col, precision=HIGHEST)
    ok = row_ok[None, :, :, :, None, None] & col_ok[None, None, None, None, :, :]
    bias = jnp.where(ok, bias, NEG_INF).transpose(0, 1, 2, 4, 3, 5)
    return bias.reshape(rpb.shape[0], 3, NA_QROWS * GRID_W, NA_KROWS * GRID_W)


def _row_reduce(blocks, combine, reduce):
    tiles, out = [], None
    for b in blocks:
        if b.shape[1] % LANES == 0:
            tiles += [b[:, c * LANES:(c + 1) * LANES] for c in range(b.shape[1] // LANES)]
        else:
            r = reduce(b, axis=-1, keepdims=True)
            out = r if out is None else combine(out, r)
    if tiles:
        r = reduce(functools.reduce(combine, tiles), axis=-1, keepdims=True)
        out = r if out is None else combine(out, r)
    return out


def _softmax_pv(scores, values, sink=None):
    m = _row_reduce(scores, jnp.maximum, jnp.max)
    if sink is not None:
        m = jnp.maximum(m, sink)
    ps = [jnp.exp(s - m) for s in scores]
    denom = _row_reduce(ps, jnp.add, jnp.sum)
    if sink is not None:
        denom = denom + jnp.exp(sink - m)
    o = functools.reduce(lambda a, b: a + b, [_dot(p.astype(BF16), v) for p, v in zip(ps, values)])
    return o / denom


def _na_body(rows, q_ref, k_ref, v_ref, qc_ref, kc_ref, vc_ref, bias_ref, o_ref, oc_ref):
    hd = NA_HD
    n_q, n_k = NA_QROWS * GRID_W, NA_KROWS * GRID_W
    n_blk = rows // NA_QROWS
    kc = kc_ref[...]
    vc = vc_ref[...]

    def blk_body(blk, carry):
        pattern = jnp.where(blk == 0, 0, jnp.where(blk == n_blk - 1, 2, 1))
        qs = pl.multiple_of(blk * n_q, n_q)
        ks = pl.multiple_of(_na_key_start(blk, rows) * GRID_W, GRID_W)
        q2 = q_ref[pl.ds(qs, n_q), :]
        k2 = k_ref[pl.ds(ks, n_k), :]
        v2 = v_ref[pl.ds(ks, n_k), :]
        outs = []
        for h in range(2):
            sl = slice(h * hd, (h + 1) * hd)
            q = q2[:, sl]
            s_loc = _dot_nt(q, k2[:, sl]) + bias_ref[h, pattern]
            s_ctx = _dot_nt(q, kc[:, sl])
            outs.append(_softmax_pv([s_loc, s_ctx], [v2[:, sl], vc[:, sl]]))
        o_ref[pl.ds(qs, n_q), :] = jnp.concatenate(outs, axis=1).astype(o_ref.dtype)
        return carry

    lax.fori_loop(0, n_blk, blk_body, 0)
    outs = []
    for h in range(2):
        sl = slice(h * hd, (h + 1) * hd)
        outs.append(_softmax_pv([_dot_nt(qc_ref[:, sl], kc[:, sl])], [vc[:, sl]]))
    oc_ref[...] = jnp.concatenate(outs, axis=1).astype(oc_ref.dtype)


def na_attention(att, bias, n_batch, seq, ctx_len):
    n_pair = NA_HEADS // 2
    ctx_blk0 = n_batch * seq // ctx_len
    lat = lambda off: pl.BlockSpec((seq, LANES), lambda b, j: (b, off + j))
    ctx = lambda off: pl.BlockSpec((ctx_len, LANES), lambda b, j: (ctx_blk0 + b, off + j))
    return pl.pallas_call(
        functools.partial(_na_body, seq // GRID_W),
        out_shape=(jax.ShapeDtypeStruct((n_batch * seq, NA_HEADS * NA_HD), BF16),
                   jax.ShapeDtypeStruct((n_batch * ctx_len, NA_HEADS * NA_HD), BF16)),
        grid=(n_batch, n_pair),
        in_specs=[lat(0), lat(n_pair), lat(2 * n_pair), ctx(0), ctx(n_pair), ctx(2 * n_pair),
                  pl.BlockSpec((2,) + bias.shape[1:], lambda b, j: (j, 0, 0, 0))],
        out_specs=(pl.BlockSpec((seq, LANES), lambda b, j: (b, j)), pl.BlockSpec((ctx_len, LANES), lambda b, j: (b, j))),
        compiler_params=_params("parallel", "arbitrary"),
        name="na_attention",
    )(att, att, att, att, att, att, bias)


def _wa_body(n_blk, with_ctx, sink_ref, q_ref, k_ref, v_ref, qc_ref, kc_ref, vc_ref, o_ref, oc_ref, kt_ref, kct_ref):
    hd, blk, grp = WA_HD, WINDOW, WA_GROUP
    rows = lax.broadcasted_iota(jnp.int32, (grp * blk, blk), 0) % blk
    cols = lax.broadcasted_iota(jnp.int32, (grp * blk, blk), 1)
    kt_ref[...] = k_ref[...].astype(F32).T.astype(BF16)
    kct_ref[...] = kc_ref[...].astype(F32).T.astype(BF16)
    n_ctx = qc_ref.shape[0]

    def stack(ref, kv, start, size):
        return jnp.concatenate([ref[pl.ds(start, size), (kv * grp + g) * hd:(kv * grp + g + 1) * hd]
                                for g in range(grp)], axis=0)

    def unstack(o, size):
        return jnp.concatenate([o[g * size:(g + 1) * size] for g in range(grp)], axis=1)

    def sink_col(kv, size):
        return jnp.concatenate([jnp.full((size, 1), sink_ref[kv * grp + g], F32) for g in range(grp)], axis=0)

    sink_q = [sink_col(kv, blk) for kv in range(WA_KV_HEADS)]

    def blk_body(n, carry):
        qs = pl.multiple_of(n * blk, blk)
        ps = pl.multiple_of(jnp.maximum(n - 1, 0) * blk, blk)
        ns = pl.multiple_of(jnp.minimum(n + 1, n_blk - 1) * blk, blk)
        for kv in range(WA_KV_HEADS):
            ksl = slice(kv * hd, (kv + 1) * hd)
            q = stack(q_ref, kv, qs, blk)
            s_prev = jnp.where((cols >= rows) & (n > 0), _dot(q, kt_ref[ksl, pl.ds(ps, blk)]), NEG_INF)
            s_self = _dot(q, kt_ref[ksl, pl.ds(qs, blk)])
            s_next = jnp.where((cols <= rows) & (n < n_blk - 1), _dot(q, kt_ref[ksl, pl.ds(ns, blk)]), NEG_INF)
            s_ctx = _dot(q, kct_ref[ksl, :])
            o = _softmax_pv([s_prev, s_self, s_next, s_ctx],
                            [v_ref[pl.ds(ps, blk), ksl], v_ref[pl.ds(qs, blk), ksl], v_ref[pl.ds(ns, blk), ksl], vc_ref[:, ksl]],
                            sink=sink_q[kv])
            o_ref[pl.ds(qs, blk), kv * grp * hd:(kv + 1) * grp * hd] = unstack(o, blk).astype(o_ref.dtype)
        return carry

    lax.fori_loop(0, n_blk, blk_body, 0)
    for kv in range(WA_KV_HEADS):
        ksl = slice(kv * hd, (kv + 1) * hd)
        if with_ctx:
            qc = stack(qc_ref, kv, 0, n_ctx)
            oc = _softmax_pv([_dot(qc, kct_ref[ksl, :])], [vc_ref[:, ksl]], sink=sink_col(kv, n_ctx))
            oc_ref[:, kv * grp * hd:(kv + 1) * grp * hd] = unstack(oc, n_ctx).astype(oc_ref.dtype)
        else:
            oc_ref[:, kv * grp * hd:(kv + 1) * grp * hd] = jnp.zeros((n_ctx, grp * hd), oc_ref.dtype)


def window_attention(qkv, sink, n_batch, seq, ctx_len, with_ctx):
    width = WA_HEADS * WA_HD
    ctx_blk0 = n_batch * seq // ctx_len
    kcol, vcol = width // LANES, width // LANES + 1
    return pl.pallas_call(
        functools.partial(_wa_body, seq // WINDOW, with_ctx),
        out_shape=(jax.ShapeDtypeStruct((n_batch * seq, width), BF16), jax.ShapeDtypeStruct((n_batch * ctx_len, width), BF16)),
        grid=(n_batch,),
        in_specs=[pl.BlockSpec(memory_space=pltpu.SMEM),
                  pl.BlockSpec((seq, width), lambda b: (b, 0)),
                  pl.BlockSpec((seq, LANES), lambda b: (b, kcol)),
                  pl.BlockSpec((seq, LANES), lambda b: (b, vcol)),
                  pl.BlockSpec((ctx_len, width), lambda b: (ctx_blk0 + b, 0)),
                  pl.BlockSpec((ctx_len, LANES), lambda b: (ctx_blk0 + b, kcol)),
                  pl.BlockSpec((ctx_len, LANES), lambda b: (ctx_blk0 + b, vcol))],
        out_specs=(pl.BlockSpec((seq, width), lambda b: (b, 0)), pl.BlockSpec((ctx_len, width), lambda b: (b, 0))),
        scratch_shapes=[pltpu.VMEM((LANES, seq), BF16), pltpu.VMEM((LANES, ctx_len), BF16)],
        compiler_params=_params("parallel"),
        name="window_attention",
    )(sink, qkv, qkv, qkv, qkv, qkv, qkv)


CONV_ROWS = 256
CONV_HALO = 8


def _conv_silu_cols(src_ref, src_cols, w_ref, w_cols, pad_ref, dst_ref, dst_row0, dst_cols):
    n_rows = src_ref.shape[0]
    n_blk = n_rows // CONV_ROWS
    zeros = jnp.zeros((CONV_HALO, LANES), F32)
    pad_ref[0:CONV_HALO, :] = zeros
    pad_ref[CONV_HALO + n_rows:2 * CONV_HALO + n_rows, :] = zeros

    def copy(i, carry):
        r0 = pl.multiple_of(i * CONV_ROWS, CONV_ROWS)
        pad_ref[pl.ds(CONV_HALO + r0, CONV_ROWS), :] = src_ref[pl.ds(r0, CONV_ROWS), src_cols]
        return carry

    lax.fori_loop(0, n_blk, copy, 0)
    first = CONV_HALO - CONV_K // 2

    def conv(i, carry):
        r0 = pl.multiple_of(i * CONV_ROWS, CONV_ROWS)
        win = pad_ref[pl.ds(r0, CONV_ROWS + 2 * CONV_HALO), :]
        acc = win[first:first + CONV_ROWS] * w_ref[0:1, w_cols]
        for j in range(1, CONV_K):
            acc = acc + win[first + j:first + j + CONV_ROWS] * w_ref[j:j + 1, w_cols]
        dst_ref[pl.ds(dst_row0 + r0, CONV_ROWS), dst_cols] = _silu(acc)
        return carry

    lax.fori_loop(0, n_blk, conv, 0)


def _tri(upper):
    r = lax.broadcasted_iota(jnp.int32, (CHUNK, CHUNK), 0)
    c = lax.broadcasted_iota(jnp.int32, (CHUNK, CHUNK), 1)
    return (r <= c) if upper else (r >= c)


def _rows_as_lanes(a):
    return jnp.concatenate([a, jnp.zeros((LANES - CHUNK, LANES), F32)], axis=0).T[:, 0:CHUNK]


def _softplus(x):
    return jnp.maximum(x, 0.0) + jnp.log1p(jnp.exp(-jnp.abs(x)))


NEUMANN_STEPS = 5
DN_GROUP = 4
DN_BATCH = 3

def _dn_body(q_ref, k_ref, v_ref, g_ref, qc_ref, kc_ref, vc_ref, gc_ref, wq_ref, wk_ref, wv_ref, alv_ref, dtv_ref,
             o_ref, oc_ref, pad_ref, qs_ref, ks_ref, vs_ref, gb_ref, osum_ref, qe_ref, kb_ref, dec_ref):
    n_ctx, n_lat = qc_ref.shape[0], q_ref.shape[0]
    n_rows = n_ctx + n_lat
    head = pl.program_id(1)
    full = slice(0, LANES)
    for src_c, src, w_ref, dst in ((qc_ref, q_ref, wq_ref, qs_ref), (kc_ref, k_ref, wk_ref, ks_ref), (vc_ref, v_ref, wv_ref, vs_ref)):
        _conv_silu_cols(src_c, full, w_ref, full, pad_ref, dst, 0, full)
        _conv_silu_cols(src, full, w_ref, full, pad_ref, dst, n_ctx, full)

    def l2(i, carry):
        r = pl.multiple_of(i * CONV_ROWS, CONV_ROWS)
        q = qs_ref[pl.ds(r, CONV_ROWS), :]
        qs_ref[pl.ds(r, CONV_ROWS), :] = q * lax.rsqrt(jnp.sum(q * q, axis=-1, keepdims=True) + EPS) * (DN_DK ** -0.5)
        k = ks_ref[pl.ds(r, CONV_ROWS), :]
        ks_ref[pl.ds(r, CONV_ROWS), :] = k * lax.rsqrt(jnp.sum(k * k, axis=-1, keepdims=True) + EPS)
        return carry

    lax.fori_loop(0, n_rows // CONV_ROWS, l2, 0)

    def gates(raw):
        lane = lax.broadcasted_iota(jnp.int32, raw.shape, 1)
        return jnp.where(lane < 2 * DN_HEADS, jax.nn.sigmoid(raw), alv_ref[...] * _softplus(raw + dtv_ref[...]))

    gb_ref[0:n_ctx, :] = gates(gc_ref[...])
    gb_ref[n_ctx:, :] = gates(g_ref[...])
    osum_ref[...] = jnp.zeros(osum_ref.shape, F32)
    grp = DN_GROUP * CHUNK
    nb = DN_BATCH
    n_chunks = n_rows // CHUNK
    n_ctx_chunks = n_ctx // CHUNK
    lane = lax.broadcasted_iota(jnp.int32, (nb, grp, LANES), 2)
    row_i = lax.broadcasted_iota(jnp.int32, (2 * nb, grp, grp), 1)
    col_i = lax.broadcasted_iota(jnp.int32, (2 * nb, grp, grp), 2)
    rev = lax.broadcasted_iota(jnp.int32, (2 * nb, grp, grp), 0) >= nb
    tri = (rev & (row_i <= col_i)) | (jnp.logical_not(rev) & (row_i >= col_i))
    keep = ((row_i // CHUNK) == (col_i // CHUNK)) & tri
    strict = keep & (row_i != col_i)
    ones = jnp.where(keep, 1.0, 0.0).astype(BF16)

    def bdot(a, b):
        return lax.dot_general(a, b, (((2,), (1,)), ((0,), (0,))), preferred_element_type=F32)

    def bdot_nt(a, b):
        return lax.dot_general(a, b, (((2,), (2,)), ((0,), (0,))), preferred_element_type=F32)

    def bdot_tn(a, b):
        return lax.dot_general(a, b, (((1,), (1,)), ((0,), (0,))), preferred_element_type=F32)

    def both(x):
        return jnp.concatenate([x, x], axis=0)

    def trip(ti, carry):
        r = pl.multiple_of(ti * (nb * grp), nb * grp)
        load = lambda ref: ref[pl.ds(r, nb * grp), :].reshape(nb, grp, LANES)
        q, k, v, gb = load(qs_ref), load(ks_ref), load(vs_ref), load(gb_ref)
        pick = lambda l: jnp.sum(jnp.where(lane == l, gb, 0.0), axis=-1, keepdims=True)
        beta = jnp.concatenate([pick(head), pick(DN_HEADS + head)], axis=0)
        g = jnp.broadcast_to(jnp.concatenate([pick(2 * DN_HEADS + head), pick(3 * DN_HEADS + head)], axis=0), (2 * nb, grp, LANES))
        g_hi = g.astype(BF16)
        g_r = g - g_hi.astype(F32)
        g_mid = g_r.astype(BF16)
        g_lo = (g_r - g_mid.astype(F32)).astype(BF16)
        cum = bdot(ones, g_hi) + bdot(ones, g_mid) + bdot(ones, g_lo)
        cum_row = jnp.stack([cum[b].T[0:1, :] for b in range(2 * nb)], axis=0)
        decay = jnp.where(keep, jnp.exp(jnp.concatenate([cum] * (grp // LANES), axis=2) - cum_row), 0.0)
        q2, k2, v2 = both(q), both(k), both(v)
        k16 = k2.astype(BF16)
        kb = k2 * beta
        a = jnp.where(strict, bdot_nt(kb.astype(BF16), k16) * decay, 0.0)
        attn16 = (bdot_nt(q2.astype(BF16), k16) * decay).astype(BF16)
        e_cum = jnp.exp(cum)
        rhs = jnp.concatenate([v2 * beta, kb * e_cum], axis=2)
        n = -a
        a16 = a.astype(BF16)
        p = bdot(a16, a16)
        for it in range(NEUMANN_STEPS):
            p16 = p.astype(BF16)
            n = n + p + bdot(n.astype(BF16), p16)
            if it + 1 < NEUMANN_STEPS:
                p = bdot(p16, p16)
        sol16 = (rhs + bdot(n.astype(BF16), rhs.astype(BF16))).astype(BF16)
        aw = bdot(attn16, sol16)
        o0 = aw[0:nb, :, 0:DN_DK] + aw[nb:, :, 0:DN_DK]
        osum_ref[pl.ds(r, nb * grp), :] += o0.reshape(nb * grp, DN_DK)
        qe = q2 * e_cum - aw[:, :, DN_DK:]
        qe_ref[0, pl.ds(r, nb * grp), :] = qe[0:nb].reshape(nb * grp, DN_DK)
        qe_ref[1, pl.ds(r, nb * grp), :] = qe[nb:].reshape(nb * grp, DN_DK)
        cpt = nb * DN_GROUP
        cum_c = cum.reshape(2 * cpt, CHUNK, LANES)
        edge = jnp.concatenate([cum_c[0:cpt, CHUNK - 1:CHUNK, :], cum_c[cpt:, 0:1, :]], axis=0)
        kdec16 = (k2.reshape(2 * cpt, CHUNK, LANES) * jnp.exp(edge - cum_c)).astype(BF16)
        bk = bdot_tn(kdec16, sol16.reshape(2 * cpt, CHUNK, 2 * DN_DK))
        dec = jnp.broadcast_to(jnp.exp(edge), (2 * cpt, 8, LANES))
        c0 = ti * cpt
        for d in range(2):
            kb_ref[d, pl.ds(c0, cpt)] = bk[d * cpt:(d + 1) * cpt]
            dec_ref[d, pl.ds(c0, cpt)] = dec[d * cpt:(d + 1) * cpt]
        return carry

    lax.fori_loop(0, n_chunks // (DN_GROUP * nb), trip, 0)

    def step(i, states):
        new = []
        for d in range(2):
            c = i if d == 0 else jnp.where(i < n_ctx_chunks, n_ctx_chunks - 1 - i, n_chunks + n_ctx_chunks - 1 - i)
            r = pl.multiple_of(c * CHUNK, CHUNK)
            s16 = states[d].astype(BF16)
            osum_ref[pl.ds(r, CHUNK), :] += _dot(qe_ref[d, pl.ds(r, CHUNK), :].astype(BF16), s16)
            bk = kb_ref[d, c]
            new.append(states[d] * dec_ref[d, c][0:1, :] - _dot(bk[:, DN_DK:].astype(BF16), s16) + bk[:, 0:DN_DK])
        return tuple(new)

    zero = jnp.zeros((DN_DK, DN_DK), F32)
    lax.fori_loop(0, n_chunks, step, (zero, zero))
    oc_ref[...] = osum_ref[0:n_ctx, :]
    o_ref[...] = osum_ref[n_ctx:, :]


def dn_gate_vectors(a_log, dt_bias):
    def vec(v):
        return jnp.pad(v.astype(F32).reshape(1, -1), ((0, 0), (2 * DN_HEADS, LANES - 4 * DN_HEADS)))
    return vec(-jnp.exp(a_log.astype(F32))), vec(dt_bias)


def deltanet(dqkv, dg, conv_w, alv, dtv, n_batch, seq, ctx_len):
    nh = DN_HEADS
    ctx_blk0 = n_batch * seq // ctx_len
    n_rows = seq + ctx_len
    lat = lambda off: pl.BlockSpec((seq, LANES), lambda b, h: (b, off + h))
    ctx = lambda off: pl.BlockSpec((ctx_len, LANES), lambda b, h: (ctx_blk0 + b, off + h))
    cw = lambda off: pl.BlockSpec((CONV_K, LANES), lambda b, h: (0, off + h))
    vec = pl.BlockSpec((1, LANES), lambda b, h: (0, 0))
    seq_buf = pltpu.VMEM((n_rows, LANES), F32)
    return pl.pallas_call(
        _dn_body,
        out_shape=(jax.ShapeDtypeStruct((n_batch * seq, nh * DN_DK), F32), jax.ShapeDtypeStruct((n_batch * ctx_len, nh * DN_DK), F32)),
        grid=(n_batch, nh),
        in_specs=[lat(0), lat(nh), lat(2 * nh), pl.BlockSpec((seq, LANES), lambda b, h: (b, 0)),
                  ctx(0), ctx(nh), ctx(2 * nh), pl.BlockSpec((ctx_len, LANES), lambda b, h: (ctx_blk0 + b, 0)),
                  cw(0), cw(nh), cw(2 * nh), vec, vec],
        out_specs=(pl.BlockSpec((seq, LANES), lambda b, h: (b, h)), pl.BlockSpec((ctx_len, LANES), lambda b, h: (b, h))),
        scratch_shapes=[pltpu.VMEM((seq + 2 * CONV_HALO, LANES), F32), seq_buf, seq_buf, seq_buf, seq_buf, seq_buf,
                        pltpu.VMEM((2, n_rows, LANES), F32),
                        pltpu.VMEM((2, n_rows // CHUNK, DN_DK, 2 * DN_DK), F32),
                        pltpu.VMEM((2, n_rows // CHUNK, 8, LANES), F32)],
        compiler_params=_params("parallel", "arbitrary"),
        name="deltanet",
    )(dqkv, dqkv, dqkv, dg, dqkv, dqkv, dqkv, dg, conv_w, conv_w, conv_w, alv, dtv)


def _ssd_body(xbc_ref, dt_ref, xbc_c_ref, dt_c_ref, cw_ref, av_ref, dtb_ref, dsk_ref, y_ref, yc_ref,
              pad_ref, s_ref, dts_ref, da_ref, hs_ref):
    n_ctx, n_lat = xbc_c_ref.shape[0], xbc_ref.shape[0]
    inner = SSD_HEADS * SSD_HD
    hpg = SSD_HEADS // SSD_GROUPS
    gw = hpg * SSD_HD
    for cb in range(xbc_ref.shape[1] // LANES):
        cols = slice(cb * LANES, (cb + 1) * LANES)
        _conv_silu_cols(xbc_c_ref, cols, cw_ref, cols, pad_ref, s_ref, 0, cols)
        _conv_silu_cols(xbc_ref, cols, cw_ref, cols, pad_ref, s_ref, n_ctx, cols)
    dts_ref[0:n_ctx, :] = _softplus(dt_c_ref[...] + dtb_ref[...])
    dts_ref[n_ctx:, :] = _softplus(dt_ref[...] + dtb_ref[...])
    da_ref[...] = dts_ref[...] * av_ref[...]
    yc_ref[...] = s_ref[0:n_ctx, 0:inner] * dsk_ref[...]
    y_ref[...] = s_ref[n_ctx:, 0:inner] * dsk_ref[...]
    hs_ref[...] = jnp.zeros(hs_ref.shape, F32)

    def chunk(d, c, row0, out_ref):
        r = pl.multiple_of(row0 + c * CHUNK, CHUNK)
        ro = pl.multiple_of(c * CHUNK, CHUNK)
        keep = _tri(upper=(d == 1))
        acum = jnp.dot(keep.astype(F32), da_ref[pl.ds(r, CHUNK), :], precision=HIGHEST, preferred_element_type=F32)
        acum_t = _rows_as_lanes(acum)
        edge = acum[CHUNK - 1:CHUNK, :] if d == 0 else acum[0:1, :]
        dt = dts_ref[pl.ds(r, CHUNK), :]
        for g in range(SSD_GROUPS):
            lanes = [d * SSD_HEADS + g * hpg + h for h in range(hpg)]

            def expand(v, rows):
                return jnp.concatenate([jnp.broadcast_to(v[:, l:l + 1], (rows, SSD_HD)) for l in lanes], axis=1)

            x4 = s_ref[pl.ds(r, CHUNK), g * gw:(g + 1) * gw]
            bm = s_ref[pl.ds(r, CHUNK), inner + g * SSD_STATE:inner + (g + 1) * SSD_STATE].astype(BF16)
            cm = s_ref[pl.ds(r, CHUNK), inner + (SSD_GROUPS + g) * SSD_STATE:inner + (SSD_GROUPS + g + 1) * SSD_STATE].astype(BF16)
            cb_mat = _dot_nt(cm, bm)
            a4 = expand(acum, CHUNK)
            e4 = expand(edge, 1)
            xd = x4 * expand(dt, CHUNK)
            prev = hs_ref[d, g]
            y = _dot_nt(cm, prev.astype(BF16)) * jnp.exp(a4)
            diag = []
            for h, l in enumerate(lanes):
                lmat = jnp.where(keep, jnp.exp(acum[:, l:l + 1] - acum_t[l:l + 1, :]), 0.0)
                diag.append(_dot((cb_mat * lmat).astype(BF16), xd[:, h * SSD_HD:(h + 1) * SSD_HD].astype(BF16)))
            y = y + jnp.concatenate(diag, axis=1)
            out_ref[pl.ds(ro, CHUNK), g * gw:(g + 1) * gw] += y
            states = _dot_tn((xd * jnp.exp(e4 - a4)).astype(BF16), bm)
            decay = jnp.concatenate([jnp.broadcast_to(jnp.exp(edge[:, l:l + 1]), (SSD_HD, SSD_STATE)) for l in lanes], axis=0)
            hs_ref[d, g] = prev * decay + states

    def segment(n_chunks, row0, out_ref):
        def body(i, carry):
            chunk(0, i, row0, out_ref)
            chunk(1, n_chunks - 1 - i, row0, out_ref)
            return carry
        lax.fori_loop(0, n_chunks, body, 0)

    segment(n_ctx // CHUNK, 0, yc_ref)
    segment(n_lat // CHUNK, n_ctx, y_ref)


def ssd(xbc, dt, conv_w, a_vec, dtb_vec, skip_vec, n_batch, seq, ctx_len):
    width = xbc.shape[1]
    inner = SSD_HEADS * SSD_HD
    ctx_blk0 = n_batch * seq // ctx_len
    n_rows = seq + ctx_len
    return pl.pallas_call(
        _ssd_body,
        out_shape=(jax.ShapeDtypeStruct((n_batch * seq, inner), F32), jax.ShapeDtypeStruct((n_batch * ctx_len, inner), F32)),
        grid=(n_batch,),
        in_specs=[pl.BlockSpec((seq, width), lambda b: (b, 0)), pl.BlockSpec((seq, LANES), lambda b: (b, 0)),
                  pl.BlockSpec((ctx_len, width), lambda b: (ctx_blk0 + b, 0)), pl.BlockSpec((ctx_len, LANES), lambda b: (ctx_blk0 + b, 0)),
                  _resident(conv_w.shape), _resident(a_vec.shape), _resident(dtb_vec.shape), _resident(skip_vec.shape)],
        out_specs=(pl.BlockSpec((seq, inner), lambda b: (b, 0)), pl.BlockSpec((ctx_len, inner), lambda b: (b, 0))),
        scratch_shapes=[pltpu.VMEM((seq + 2 * CONV_HALO, LANES), F32), pltpu.VMEM((n_rows, width), F32),
                        pltpu.VMEM((n_rows, LANES), F32), pltpu.VMEM((n_rows, LANES), F32),
                        pltpu.VMEM((2, SSD_GROUPS, (SSD_HEADS // SSD_GROUPS) * SSD_HD, SSD_STATE), F32)],
        compiler_params=_params("parallel"),
        name="ssd",
    )(xbc, dt, xbc, dt, conv_w, a_vec, dtb_vec, skip_vec)


def ssd_vectors(a_log, dt_bias, d_skip):
    def vec(v):
        return jnp.pad(v.astype(F32).reshape(1, -1), ((0, 0), (0, LANES - 2 * SSD_HEADS)))
    return vec(-jnp.exp(a_log.astype(F32))), vec(dt_bias), jnp.repeat(d_skip.astype(F32), SSD_HD)[None, :]


def kernel(x, c, ctx, c_ctx, ada_w, ada_b, norm_w, ev_w_in, ev_w_out, na_rpb, dn_conv, dn_a_log, dn_dt_bias, dn_norm, ffn_w1, ffn_w3, ffn_w2, od_w_in, od_w_out, wa_sink, ssd_conv, ssd_a_log, ssd_dt_bias, ssd_d, ssd_norm, moe_router, moe_w1, moe_w3, moe_w2):
    n_batch, seq, d = x.shape
    ctx_len = ctx.shape[1]
    depth = ada_w.shape[0]
    n_lat, n_ctx = n_batch * seq, n_batch * ctx_len
    n_cond = -(-(n_batch + 1) // 8) * 8
    cond = jnp.concatenate([c, c_ctx[None, :], jnp.zeros((n_cond - n_batch - 1, d), F32)], axis=0)
    mod_all = ada_modulation(cond, ada_w, ada_b)
    x_all = jnp.concatenate([x.reshape(n_lat, d), ctx.reshape(n_ctx, d)], axis=0)
    cos_t, sin_t = rope_tables(seq)
    moe_w1_16, moe_w3_16, moe_w2_16 = moe_w1.astype(BF16), moe_w3.astype(BF16), moe_w2.astype(BF16)
    for layer in range(depth):
        i = layer // 2
        n_tok = n_lat + n_ctx if layer < depth - 1 else n_lat
        mod = mod_all[layer, :n_batch + 1].reshape(n_batch + 1, 6, d)
        nw = norm_w[layer]
        if layer % 2 == 0:
            att, dqkv, dz, dg = proj_in_even(x_all, mod, nw, prep_w_in_even(ev_w_in[i]), n_batch, seq)
            o_a, oc_a = na_attention(att, na_bias_tables(na_rpb[i], seq // GRID_W), n_batch, seq, ctx_len)
            alv, dtv = dn_gate_vectors(dn_a_log[i], dn_dt_bias[i])
            o_b, oc_b = deltanet(dqkv, dg, dn_conv[i], alv, dtv, n_batch, seq, ctx_len)
            x_all = proj_out(o_a, oc_a, o_b, oc_b, dz,
                             jnp.tile(dn_norm[i], DN_HEADS)[None, :], ev_w_out[i].astype(BF16), x_all, mod, nw,
                             n_batch, seq, n_tok, DN_DK, False)
            x_all = ffn_dense(x_all, mod, nw, ffn_w1[i].astype(BF16), ffn_w3[i].astype(BF16), ffn_w2[i].astype(BF16),
                              n_batch, seq, n_tok)
        else:
            qkv, z, xbc, dt = proj_in_odd(x_all, mod, nw, prep_w_in_odd(od_w_in[i]), cos_t, sin_t, n_batch, seq)
            o_c, oc_c = window_attention(qkv, wa_sink[i], n_batch, seq, ctx_len, n_tok > n_lat)
            a_vec, dtb_vec, skip_vec = ssd_vectors(ssd_a_log[i], ssd_dt_bias[i], ssd_d[i])
            y_d, yc_d = ssd(xbc, dt, ssd_conv[i], a_vec, dtb_vec, skip_vec, n_batch, seq, ctx_len)
            x_all = proj_out(o_c, oc_c, y_d, yc_d, z,
                             ssd_norm[i][None, :], od_w_out[i].astype(BF16), x_all, mod, nw,
                             n_batch, seq, n_tok, SSD_HEADS * SSD_HD // SSD_GROUPS, True)
            router = jnp.pad(moe_router[i], ((0, 0), (0, LANES - N_EXPERTS)))
            x_all = moe_layer(x_all, mod, nw, router, moe_w1_16, moe_w3_16, moe_w2_16, i, n_batch, seq, n_tok)
    return x_all[:n_lat].reshape(n_batch, seq, d)
```
